```python
import math
import jax
import jax.numpy as jnp
from jax import lax
import numpy as np

D_MODEL = 2048
BATCH = 4
SEQ = 8192
DEPTH = 2
DEC_BATCH = 2
DEC_SEQ = 16384
PAST_LEN = 128

HEAD_DIM = 64
A_HEADS = 12
A_BRANCHES = ((128, 1), (512, 4), (2048, 16))
A_BLOCK = 64
R_HEADS = 8
R_QK_DIM = 32
R_V_DIM = 64
R_CHUNK = 128
ROPE_BASE = 10000.0
C_HEADS = 12
C_KV_HEADS = 4
C_GROUP = C_HEADS // C_KV_HEADS
C_RADIUS = 128
C_BLOCK = 128
REL_BUCKETS = 32
REL_MAX_DIST = 1024
D_FF = ((8 * D_MODEL + 3 * 256 - 1) // (3 * 256)) * 256
A_W = A_HEADS * HEAD_DIM
R_QK_W = R_HEADS * R_QK_DIM
R_W = R_HEADS * R_V_DIM
C_W = C_HEADS * HEAD_DIM
C_KV_W = C_KV_HEADS * HEAD_DIM
MIX_W = A_W + R_W + C_W
SPLIT_SIZES = (A_W, A_W, A_W, R_QK_W, R_QK_W, R_W, R_W, C_W, C_KV_W, C_KV_W)
IN_COLS = 3 * A_W + 2 * R_QK_W + 2 * R_W + C_W + 2 * C_KV_W
EPS = 1e-6
GN_EPS = 1e-5
NEG = -1e30

kernel_name = 'hybrid_dilated_retention_swa_encoder'


def rms_norm(x, g):
    xf = x.astype(jnp.float32)
    y = xf * lax.rsqrt(jnp.mean(xf * xf, axis=-1, keepdims=True) + EPS) * g.astype(jnp.float32)
    return y.astype(x.dtype)


def t5_bucket(rel):
    nb = REL_BUCKETS // 2
    max_exact = nb // 2
    ret = np.where(rel > 0, nb, 0)
    n = np.abs(rel)
    nf = np.maximum(n, 1).astype(np.float32)
    large = max_exact + (np.log(nf / max_exact) / math.log(REL_MAX_DIST / max_exact) * (nb - max_exact)).astype(np.int32)
    large = np.minimum(large, nb - 1)
    return (ret + np.where(n < max_exact, n, large)).astype(np.int32)


def rel_bias(table, bucket):
    return jnp.take(table.astype(jnp.float32), jnp.asarray(bucket), axis=0).transpose(2, 0, 1)


def dilated_branch(q, k, v, window, dil, bias_tab):
    B, S, H, Dh = q.shape
    R = window // (2 * dil)
    L = S // dil
    nblk = -(-L // A_BLOCK)
    Lp = nblk * A_BLOCK
    pad = Lp - L

    def sub(t):
        return t.reshape(B, L, dil, H, Dh).transpose(0, 2, 1, 3, 4)

    qs = jnp.pad(sub(q), ((0, 0), (0, 0), (0, pad), (0, 0), (0, 0))).reshape(B, dil, nblk, A_BLOCK, H, Dh)

    def windows(t):
        tp = jnp.pad(sub(t), ((0, 0), (0, 0), (A_BLOCK, pad + A_BLOCK), (0, 0), (0, 0)))
        tp = tp.reshape(B, dil, nblk + 2, A_BLOCK, H, Dh)
        return jnp.concatenate([tp[:, :, :-2], tp[:, :, 1:-1], tp[:, :, 2:]], axis=3)

    kw, vw = windows(k), windows(v)
    qq = np.arange(A_BLOCK)[:, None]
    kk = np.arange(3 * A_BLOCK)[None, :]
    off = kk - A_BLOCK - qq
    kpos = np.arange(nblk)[:, None, None] * A_BLOCK + kk[None] - A_BLOCK
    valid = (np.abs(off) <= R)[None] & (kpos >= 0) & (kpos < L)
    bias = rel_bias(bias_tab, t5_bucket(off * dil))
    s = jnp.einsum('bdnqhe,bdnkhe->bdnhqk', qs, kw, preferred_element_type=jnp.float32) * (HEAD_DIM ** -0.5) + bias
    s = jnp.where(jnp.asarray(valid)[:, None], s, NEG)
    m = jnp.max(s, axis=-1)
    p = jnp.exp(s - m[..., None])
    l = jnp.sum(p, axis=-1)
    num = jnp.einsum('bdnhqk,bdnkhe->bdnqhe', p, vw.astype(jnp.float32))
    num = num.reshape(B, dil, Lp, H, Dh)[:, :, :L].transpose(0, 2, 1, 3, 4).reshape(B, S, H, Dh)

    def back(t):
        t = t.transpose(0, 1, 2, 4, 3).reshape(B, dil, Lp, H)[:, :, :L]
        return t.transpose(0, 2, 1, 3).reshape(B, S, H)

    return num, back(m), back(l)


def dilated_attention(q, k, v, bias_tab):
    parts = [dilated_branch(q, k, v, w, d, bias_tab) for (w, d) in A_BRANCHES]
    M = parts[0][1]
    for _, m, _ in parts[1:]:
        M = jnp.maximum(M, m)
    num = 0.0
    den = 0.0
    for n_i, m_i, l_i in parts:
        w_i = jnp.exp(m_i - M)
        num = num + n_i * w_i[..., None]
        den = den + l_i * w_i
    return num / den[..., None]


def rope(x, pos):
    half = R_QK_DIM // 2
    freqs = ROPE_BASE ** (-jnp.arange(half, dtype=jnp.float32) / half)
    ang = pos[:, None] * freqs[None]
    cos = jnp.cos(ang)[None, :, None, :]
    sin = jnp.sin(ang)[None, :, None, :]
    x1, x2 = x[..., :half], x[..., half:]
    return jnp.concatenate([x1 * cos - x2 * sin, x1 * sin + x2 * cos], axis=-1)


def retention_dir(q, k, v, lg, inclusive):
    B, S, H, dk = q.shape
    dv = v.shape[-1]
    C = R_CHUNK
    n = S // C
    qc = q.reshape(B, n, C, H, dk)
    kc = k.reshape(B, n, C, H, dk)
    vc = v.reshape(B, n, C, H, dv)
    t = np.arange(C)
    diff = t[:, None] - t[None, :]
    mask = (diff >= 0) if inclusive else (diff > 0)
    decay = jnp.where(jnp.asarray(mask)[None], jnp.exp(lg[:, None, None] * np.maximum(diff, 0).astype(np.float32)), 0.0)
    s = jnp.einsum('bnthd,bnshd->bnhts', qc, kc) * decay
    intra = jnp.einsum('bnhts,bnshe->bnthe', s, vc)
    tf = t.astype(np.float32)
    kdec = jnp.exp(lg[None, :] * (C - 1 - tf)[:, None])
    kv = jnp.einsum('bnshd,sh,bnshe->bnhde', kc, kdec, vc)
    g_chunk = jnp.exp(lg * C)

    def step(R, kv_i):
        return R * g_chunk[:, None, None] + kv_i, R

    _, r_prev = lax.scan(step, jnp.zeros((B, H, dk, dv), jnp.float32), kv.transpose(1, 0, 2, 3, 4))
    r_prev = r_prev.transpose(1, 0, 2, 3, 4)
    qdec = jnp.exp(lg[None, :] * (tf + 1.0)[:, None])
    cross = jnp.einsum('bnthd,th,bnhde->bnthe', qc, qdec, r_prev)
    return (intra + cross).reshape(B, S, H, dv)


def bidir_retention(q, k, v, gate, a_fwd, a_bwd):
    B, S, H, _ = q.shape
    pos = jnp.arange(S, dtype=jnp.float32)
    q = rope(q.astype(jnp.float32), pos)
    k = rope(k.astype(jnp.float32), pos) * (R_QK_DIM ** -0.5)
    v = v.astype(jnp.float32)
    lg_f = jnp.log1p(-jnp.exp2(-a_fwd.astype(jnp.float32)))
    lg_b = jnp.log1p(-jnp.exp2(-a_bwd.astype(jnp.float32)))
    fwd = retention_dir(q, k, v, lg_f, True)
    bwd = jnp.flip(retention_dir(jnp.flip(q, 1), jnp.flip(k, 1), jnp.flip(v, 1), lg_b, False), 1)
    o = fwd + bwd
    mu = jnp.mean(o, axis=-1, keepdims=True)
    var = jnp.mean(jnp.square(o - mu), axis=-1, keepdims=True)
    o = (o - mu) * lax.rsqrt(var + GN_EPS)
    return o.reshape(B, S, H * R_V_DIM) * jax.nn.silu(gate.astype(jnp.float32))


def window_gqa_sink(q, k, v, bias_tab, sink):
    B, S, _, Dh = q.shape
    nblk = S // C_BLOCK
    qb = q.reshape(B, nblk, C_BLOCK, C_KV_HEADS, C_GROUP, Dh)

    def windows(t):
        tp = jnp.pad(t, ((0, 0), (C_BLOCK, C_BLOCK), (0, 0), (0, 0))).reshape(B, nblk + 2, C_BLOCK, C_KV_HEADS, Dh)
        return jnp.concatenate([tp[:, :-2], tp[:, 1:-1], tp[:, 2:]], axis=2)

    kw, vw = windows(k), windows(v)
    qq = np.arange(C_BLOCK)[:, None]
    kk = np.arange(3 * C_BLOCK)[None, :]
    off = kk - C_BLOCK - qq
    kpos = np.arange(nblk)[:, None, None] * C_BLOCK + kk[None] - C_BLOCK
    valid = (np.abs(off) <= C_RADIUS)[None] & (kpos >= 0) & (kpos < S)
    bias = rel_bias(bias_tab, t5_bucket(off)).reshape(C_KV_HEADS, C_GROUP, C_BLOCK, 3 * C_BLOCK)
    s = jnp.einsum('bnqhge,bnkhe->bnhgqk', qb, kw, preferred_element_type=jnp.float32) * (HEAD_DIM ** -0.5) + bias
    s = jnp.where(jnp.asarray(valid)[:, None, None], s, NEG)
    sk = sink.astype(jnp.float32).reshape(C_KV_HEADS, C_GROUP)[:, :, None]
    m = jnp.maximum(jnp.max(s, axis=-1), sk)
    p = jnp.exp(s - m[..., None])
    den = jnp.sum(p, axis=-1) + jnp.exp(sk - m)
    o = jnp.einsum('bnhgqk,bnkhe->bnqhge', p, vw.astype(jnp.float32))
    o = o / den.transpose(0, 1, 4, 2, 3)[..., None]
    return o.reshape(B, S, C_W)


def encoder_layer(x, bias_tab, g1, w_in, a_fwd, a_bwd, sink, w_out, g2, w_gate, w_up, w_down):
    B, S, _ = x.shape
    h = rms_norm(x, g1)
    proj = h @ w_in
    idx = []
    acc = 0
    for sz in SPLIT_SIZES[:-1]:
        acc += sz
        idx.append(acc)
    qa, ka, va, qr, kr, vr, gr, qc, kc, vc = jnp.split(proj, idx, axis=-1)
    a = dilated_attention(qa.reshape(B, S, A_HEADS, HEAD_DIM), ka.reshape(B, S, A_HEADS, HEAD_DIM),
                          va.reshape(B, S, A_HEADS, HEAD_DIM), bias_tab[:, :A_HEADS]).reshape(B, S, A_W)
    r = bidir_retention(qr.reshape(B, S, R_HEADS, R_QK_DIM), kr.reshape(B, S, R_HEADS, R_QK_DIM),
                        vr.reshape(B, S, R_HEADS, R_V_DIM), gr, a_fwd, a_bwd)
    c = window_gqa_sink(qc.reshape(B, S, C_HEADS, HEAD_DIM), kc.reshape(B, S, C_KV_HEADS, HEAD_DIM),
                        vc.reshape(B, S, C_KV_HEADS, HEAD_DIM), bias_tab[:, A_HEADS:], sink)
    mix = jnp.concatenate([a, r, c], axis=-1).astype(x.dtype) @ w_out
    x = x + mix
    h = rms_norm(x, g2)
    x = x + (jax.nn.silu(h @ w_gate) * (h @ w_up)) @ w_down
    return x


def setup_inputs(seed: int = 0) -> dict:
    key = jax.random.key(seed)
    ks = jax.random.split(key, 14)
    f32 = jnp.float32
    base_decay = 5.0 + jnp.arange(R_HEADS, dtype=f32)
    return {
        'x_prompt': jax.random.normal(ks[0], (BATCH, SEQ, D_MODEL), f32),
        'x_sample': jax.random.normal(ks[1], (DEC_BATCH, DEC_SEQ, D_MODEL), f32),
        'rel_bias': 0.5 * jax.random.normal(ks[2], (REL_BUCKETS, A_HEADS + C_HEADS), f32),
        'norm1_g': 1.0 + 0.01 * jax.random.normal(ks[3], (DEPTH, D_MODEL), f32),
        'w_in': jax.random.normal(ks[4], (DEPTH, D_MODEL, IN_COLS), f32) * D_MODEL ** -0.5,
        'ret_decay_fwd': base_decay + 0.1 * jax.random.normal(ks[5], (DEPTH, R_HEADS), f32),
        'ret_decay_bwd': base_decay + 0.1 * jax.random.normal(ks[6], (DEPTH, R_HEADS), f32),
        'attn_sink': 0.5 * jax.random.normal(ks[7], (DEPTH, C_HEADS), f32),
        'w_out': jax.random.normal(ks[8], (DEPTH, MIX_W, D_MODEL), f32) * MIX_W ** -0.5,
        'norm2_g': 1.0 + 0.01 * jax.random.normal(ks[9], (DEPTH, D_MODEL), f32),
        'w_gate': jax.random.normal(ks[10], (DEPTH, D_MODEL, D_FF), f32) * D_MODEL ** -0.5,
        'w_up': jax.random.normal(ks[11], (DEPTH, D_MODEL, D_FF), f32) * D_MODEL ** -0.5,
        'w_down': jax.random.normal(ks[12], (DEPTH, D_FF, D_MODEL), f32) * D_FF ** -0.5,
        'final_norm_g': 1.0 + 0.01 * jax.random.normal(ks[13], (D_MODEL,), f32),
    }


def reference(x_prompt, x_sample, rel_bias, norm1_g, w_in, ret_decay_fwd, ret_decay_bwd, attn_sink,
              w_out, norm2_g, w_gate, w_up, w_down, final_norm_g):
    def trunk(x):
        for i in range(DEPTH):
            x = encoder_layer(x, rel_bias, norm1_g[i], w_in[i], ret_decay_fwd[i], ret_decay_bwd[i],
                              attn_sink[i], w_out[i], norm2_g[i], w_gate[i], w_up[i], w_down[i])
        return rms_norm(x, final_norm_g)

    y_prompt = trunk(x_prompt)
    y_sample = trunk(x_sample)
    return (y_prompt, y_sample)
```

```python
import functools
import math

import numpy as np
import jax
import jax.numpy as jnp
from jax import lax
from jax.experimental import pallas as pl
from jax.experimental.pallas import tpu as pltpu

D_MODEL = 2048
DEPTH = 2
HEAD_DIM = 64
A_HEADS = 12
A_BRANCHES = ((128, 1), (512, 4), (2048, 16))
A_RADIUS = 64
R_HEADS = 8
R_QK_DIM = 32
R_V_DIM = 64
R_CHUNK = 128
ROPE_BASE = 10000.0
C_HEADS = 12
C_KV_HEADS = 4
C_GROUP = C_HEADS // C_KV_HEADS
C_RADIUS = 128
REL_BUCKETS = 32
REL_MAX_DIST = 1024
D_FF = 5632
EPS = 1e-6
GN_EPS = 1e-5
NEG = -1e30

A_W = A_HEADS * HEAD_DIM
R_QK_W = R_HEADS * R_QK_DIM
R_W = R_HEADS * R_V_DIM
C_W = C_HEADS * HEAD_DIM
C_KV_W = C_KV_HEADS * HEAD_DIM
PA_W = 3 * A_W
PB_W = 2 * R_QK_W + 2 * R_W
PC_W = C_W + 2 * C_KV_W
IN_COLS = PA_W + PB_W + PC_W
MIX_W = A_W + R_W + C_W

LANES = 128
V7X_VMEM_BYTES = 64 * 1024 * 1024
VMEM_LIMIT = V7X_VMEM_BYTES - 8 * 1024 * 1024

A_TQ = 128
A_TK = A_TQ + 2 * A_RADIUS
C_TQ = 128
C_TK = 3 * C_TQ
R_STEP_CHUNKS = 4

BF16 = jnp.bfloat16
F32 = jnp.float32


def _cparams(*sem):
    return pltpu.CompilerParams(dimension_semantics=sem, vmem_limit_bytes=VMEM_LIMIT)


def _t5_bucket(rel):
    nb = REL_BUCKETS // 2
    max_exact = nb // 2
    ret = np.where(rel > 0, nb, 0)
    n = np.abs(rel)
    nf = np.maximum(n, 1).astype(np.float32)
    large = max_exact + (np.log(nf / max_exact) / math.log(REL_MAX_DIST / max_exact) * (nb - max_exact)).astype(np.int32)
    large = np.minimum(large, nb - 1)
    return (ret + np.where(n < max_exact, n, large)).astype(np.int32)


def _band_bias(table, tq, tk, lead, radius, dil):
    off = np.arange(tk)[None, :] - lead - np.arange(tq)[:, None]
    bias = jnp.take(table.astype(F32), jnp.asarray(_t5_bucket(off * dil)), axis=0).transpose(2, 0, 1)
    return jnp.where(jnp.asarray(np.abs(off) <= radius)[None], bias, NEG)


_PROJ_CHUNKS = (
    (0, 0, 0, 768), (0, 768, 768, 768), (0, 1536, 1536, 768),
    (1, 0, 2304, 512), (1, 512, 2816, 512), (1, 1024, 3328, 512),
    (2, 0, 3840, 768), (2, 768, 4608, 512),
)


def _norm_proj_kernel(x_ref, g_ref, w_ref, pa_ref, pb_ref, pc_ref):
    x = x_ref[...]
    h = (x * lax.rsqrt(jnp.mean(x * x, axis=-1, keepdims=True) + EPS) * g_ref[...]).astype(BF16)
    outs = (pa_ref, pb_ref, pc_ref)
    for oi, oc, wc, width in _PROJ_CHUNKS:
        y = jnp.dot(h, w_ref[:, wc:wc + width], preferred_element_type=F32)
        outs[oi][:, oc:oc + width] = y.astype(BF16)


def _norm_proj(x, g, w, tm=512):
    t = x.shape[0]
    assert t % tm == 0
    return pl.pallas_call(
        _norm_proj_kernel,
        grid=(t // tm,),
        in_specs=[
            pl.BlockSpec((tm, D_MODEL), lambda i: (i, 0)),
            pl.BlockSpec((1, D_MODEL), lambda i: (0, 0)),
            pl.BlockSpec((D_MODEL, IN_COLS), lambda i: (0, 0), pipeline_mode=pl.Buffered(1)),
        ],
        out_specs=[
            pl.BlockSpec((tm, PA_W), lambda i: (i, 0)),
            pl.BlockSpec((tm, PB_W), lambda i: (i, 0)),
            pl.BlockSpec((tm, PC_W), lambda i: (i, 0)),
        ],
        out_shape=[
            jax.ShapeDtypeStruct((t, PA_W), BF16),
            jax.ShapeDtypeStruct((t, PB_W), BF16),
            jax.ShapeDtypeStruct((t, PC_W), BF16),
        ],
        compiler_params=_cparams("parallel"),
        name="norm_in_proj",
    )(x, g.reshape(1, D_MODEL), w)


def _edge_penalty(width, lead, tail, first, last):
    col = lax.broadcasted_iota(jnp.int32, (1, width), 1)
    lo = jnp.where(first, NEG, 0.0).astype(F32)
    hi = jnp.where(last, NEG, 0.0).astype(F32)
    return jnp.where(col < lead, lo, 0.0) + jnp.where(col >= width - tail, hi, 0.0)


def _dilated_attn_kernel(q_ref, kp_ref, kc_ref, kn_ref, vp_ref, vc_ref, vn_ref, bias_ref, o_ref, lse_ref):
    i = pl.program_id(2)
    pen = _edge_penalty(A_TK, A_RADIUS, A_RADIUS, i == 0, i == pl.num_programs(2) - 1)
    q = q_ref[0] * 0.125
    kw = jnp.concatenate([kp_ref[0], kc_ref[0], kn_ref[0]], axis=0)
    vw = jnp.concatenate([vp_ref[0], vc_ref[0], vn_ref[0]], axis=0)
    lane = lax.broadcasted_iota(jnp.int32, (A_TQ, LANES), 1)
    lse = jnp.zeros((A_TQ, LANES), F32)
    for h in range(A_HEADS):
        sl = slice(h * HEAD_DIM, (h + 1) * HEAD_DIM)
        s = lax.dot_general(q[:, sl], kw[:, sl], (((1,), (1,)), ((), ())), preferred_element_type=F32)
        s = s + bias_ref[h] + pen
        m = jnp.max(s, axis=-1, keepdims=True)
        p = jnp.exp(s - m)
        l = jnp.sum(p, axis=-1, keepdims=True)
        o = jnp.dot(p.astype(BF16), vw[:, sl], preferred_element_type=F32)
        o_ref[0, :, sl] = (o / l).astype(BF16)
        lse = jnp.where(lane == h, m + jnp.log(l), lse)
    lse_ref[0] = lse


def _dilated_branch(pa, bias, dil):
    b, s, _ = pa.shape
    l = s // dil
    assert s % dil == 0 and l % A_TQ == 0
    nt = l // A_TQ
    halo = A_RADIUS
    hb = A_TQ // halo
    pav = pa.reshape(b, l, dil * PA_W)
    nblk = PA_W // A_W

    def qmap(bi, r, i):
        return (bi, i, r * nblk)

    def cur(c):
        return lambda bi, r, i: (bi, i, r * nblk + c)

    def prev(c):
        return lambda bi, r, i: (bi, jnp.maximum(i * hb - 1, 0), r * nblk + c)

    def nxt(c):
        return lambda bi, r, i: (bi, jnp.minimum((i + 1) * hb, nt * hb - 1), r * nblk + c)

    o, lse = pl.pallas_call(
        _dilated_attn_kernel,
        grid=(b, dil, nt),
        in_specs=[
            pl.BlockSpec((1, A_TQ, A_W), qmap),
            pl.BlockSpec((1, halo, A_W), prev(1)),
            pl.BlockSpec((1, A_TQ, A_W), cur(1)),
            pl.BlockSpec((1, halo, A_W), nxt(1)),
            pl.BlockSpec((1, halo, A_W), prev(2)),
            pl.BlockSpec((1, A_TQ, A_W), cur(2)),
            pl.BlockSpec((1, halo, A_W), nxt(2)),
            pl.BlockSpec((A_HEADS, A_TQ, A_TK), lambda bi, r, i: (0, 0, 0)),
        ],
        out_specs=[
            pl.BlockSpec((1, A_TQ, A_W), lambda bi, r, i: (bi, i, r)),
            pl.BlockSpec((1, A_TQ, LANES), lambda bi, r, i: (bi, i, r)),
        ],
        out_shape=[
            jax.ShapeDtypeStruct((b, l, dil * A_W), BF16),
            jax.ShapeDtypeStruct((b, l, dil * LANES), F32),
        ],
        compiler_params=_cparams("parallel", "parallel", "arbitrary"),
        name=f"dilated_attn_d{dil}",
    )(pav, pav, pav, pav, pav, pav, pav, bias)
    return o.reshape(b * s, A_W), lse.reshape(b * s, LANES)


def _window_attn_kernel(sink_ref, q_ref, kp_ref, kc_ref, kn_ref, vp_ref, vc_ref, vn_ref, bias_ref, o_ref):
    i = pl.program_id(1)
    pen = _edge_penalty(C_TK, C_TQ, C_TQ, i == 0, i == pl.num_programs(1) - 1)
    q = q_ref[0] * 0.125
    kw = jnp.concatenate([kp_ref[0], kc_ref[0], kn_ref[0]], axis=0)
    vw = jnp.concatenate([vp_ref[0], vc_ref[0], vn_ref[0]], axis=0)
    for kvh in range(C_KV_HEADS):
        ksl = slice(kvh * HEAD_DIM, (kvh + 1) * HEAD_DIM)
        kh = kw[:, ksl]
        vh = vw[:, ksl]
        for g in range(C_GROUP):
            h = kvh * C_GROUP + g
            sl = slice(h * HEAD_DIM, (h + 1) * HEAD_DIM)
            s = lax.dot_general(q[:, sl], kh, (((1,), (1,)), ((), ())), preferred_element_type=F32)
            s = s + bias_ref[h] + pen
            sk = sink_ref[h]
            m = jnp.maximum(jnp.max(s, axis=-1, keepdims=True), sk)
            p = jnp.exp(s - m)
            den = jnp.sum(p, axis=-1, keepdims=True) + jnp.exp(sk - m)
            o = jnp.dot(p.astype(BF16), vh, preferred_element_type=F32)
            o_ref[0, :, sl] = (o / den).astype(BF16)


def _window_attn(pc, bias, sink):
    b, s, _ = pc.shape
    assert s % C_TQ == 0
    nt = s // C_TQ
    kblk = C_W // C_KV_W

    def cur(c):
        return lambda bi, i: (bi, i, c)

    def prev(c):
        return lambda bi, i: (bi, jnp.maximum(i - 1, 0), c)

    def nxt(c):
        return lambda bi, i: (bi, jnp.minimum(i + 1, nt - 1), c)

    kv_spec = lambda m: pl.BlockSpec((1, C_TQ, C_KV_W), m)
    o = pl.pallas_call(
        _window_attn_kernel,
        grid=(b, nt),
        in_specs=[
            pl.BlockSpec(memory_space=pltpu.SMEM),
            pl.BlockSpec((1, C_TQ, C_W), cur(0)),
            kv_spec(prev(kblk)), kv_spec(cur(kblk)), kv_spec(nxt(kblk)),
            kv_spec(prev(kblk + 1)), kv_spec(cur(kblk + 1)), kv_spec(nxt(kblk + 1)),
            pl.BlockSpec((C_HEADS, C_TQ, C_TK), lambda bi, i: (0, 0, 0)),
        ],
        out_specs=pl.BlockSpec((1, C_TQ, C_W), lambda bi, i: (bi, i, 0)),
        out_shape=jax.ShapeDtypeStruct((b, s, C_W), BF16),
        compiler_params=_cparams("parallel", "arbitrary"),
        name="window_gqa",
    )(sink.astype(F32), pc, pc, pc, pc, pc, pc, pc, bias)
    return o.reshape(b * s, C_W)


def _rope(x, cos, sin, lane_lo):
    parts = []
    for a in range(0, R_QK_W, LANES):
        xh = x[:, a:a + LANES]
        half = R_QK_DIM // 2
        parts.append(jnp.where(lane_lo, pltpu.roll(xh, LANES - half, 1), pltpu.roll(xh, half, 1)))
    return x * cos + jnp.concatenate(parts, axis=1) * sin


def _retention_chunk(q, k, v, cos, sin, decay, qdec, kdec, rdec, mask, state_ref):
    c = R_CHUNK
    lane = lax.broadcasted_iota(jnp.int32, (c, LANES), 1)
    lane_lo = (lane % R_QK_DIM) < (R_QK_DIM // 2)
    q = _rope(q.astype(F32), cos, sin, lane_lo)
    k = _rope(k.astype(F32), cos, sin, lane_lo) * (R_QK_DIM ** -0.5)
    state = state_ref[...]
    cross = jnp.dot((q * qdec).astype(BF16), state.astype(BF16), preferred_element_type=F32)

    qb = q.astype(BF16)
    kb = k.astype(BF16)
    head = lax.broadcasted_iota(jnp.int32, (c, R_QK_W), 1) // R_QK_DIM
    qs = jnp.concatenate([jnp.where(head == h, qb, jnp.zeros_like(qb)) for h in range(R_HEADS)], axis=0)
    s = lax.dot_general(qs, kb, (((1,), (1,)), ((), ())), preferred_element_type=F32)
    p = (s * decay).astype(BF16)

    vlane = lax.broadcasted_iota(jnp.int32, (c, R_W), 1) % LANES
    v_lo = jnp.where(vlane < R_V_DIM, v, jnp.zeros_like(v))
    v_hi = jnp.where(vlane >= R_V_DIM, v, jnp.zeros_like(v))
    pieces = []
    for j in range(R_HEADS // 2):
        cs = slice(j * LANES, (j + 1) * LANES)
        a = jnp.dot(p[(2 * j) * c:(2 * j + 1) * c], v_lo[:, cs], preferred_element_type=F32)
        a = a + jnp.dot(p[(2 * j + 1) * c:(2 * j + 2) * c], v_hi[:, cs], preferred_element_type=F32)
        pieces.append(a)
    intra = jnp.concatenate(pieces, axis=1)

    kv = lax.dot_general((k * kdec).astype(BF16), v, (((0,), (0,)), ((), ())), preferred_element_type=F32)
    state_ref[...] = state * rdec + kv * mask
    return cross + intra


def _retention_fwd_kernel(q_ref, k_ref, v_ref, cos_ref, sin_ref, decay_ref, qdec_ref, kdec_ref, rdec_ref,
                          mask_ref, o_ref, state_ref, *, n_chunks):
    @pl.when(pl.program_id(1) == 0)
    def _():
        state_ref[...] = jnp.zeros_like(state_ref)

    for ci in range(n_chunks):
        rows = slice(ci * R_CHUNK, (ci + 1) * R_CHUNK)
        o_ref[0, rows, :] = _retention_chunk(
            q_ref[0, rows, :], k_ref[0, rows, :], v_ref[0, rows, :], cos_ref[rows, :], sin_ref[rows, :],
            decay_ref[...], qdec_ref[...], kdec_ref[...], rdec_ref[...], mask_ref[...], state_ref)


def _split_dot(x, w):
    hi = x.astype(BF16)
    lo = (x - hi.astype(F32)).astype(BF16)
    return jnp.dot(hi, w, preferred_element_type=F32) + jnp.dot(lo, w, preferred_element_type=F32)


def _retention_bwd_kernel(q_ref, k_ref, v_ref, cos_ref, sin_ref, decay_ref, qdec_ref, kdec_ref, rdec_ref,
                          mask_ref, fwd_ref, gate_ref, avg_ref, o_ref, state_ref, *, n_chunks):
    @pl.when(pl.program_id(1) == 0)
    def _():
        state_ref[...] = jnp.zeros_like(state_ref)

    for ci in reversed(range(n_chunks)):
        rows = slice(ci * R_CHUNK, (ci + 1) * R_CHUNK)
        o = fwd_ref[0, rows, :] + _retention_chunk(
            q_ref[0, rows, :], k_ref[0, rows, :], v_ref[0, rows, :], cos_ref[rows, :], sin_ref[rows, :],
            decay_ref[...], qdec_ref[...], kdec_ref[...], rdec_ref[...], mask_ref[...], state_ref)
        avg = avg_ref[...]
        d = o - _split_dot(o, avg)
        y = d * lax.rsqrt(_split_dot(d * d, avg) + GN_EPS)
        g = gate_ref[0, rows, :].astype(F32)
        o_ref[0, rows, :] = (y * (g / (1.0 + jnp.exp(-g)))).astype(BF16)


def _retention_tables(lg, reverse):
    c = R_CHUNK
    t = np.arange(c)
    diff = (t[None, :] - t[:, None]) if reverse else (t[:, None] - t[None, :])
    keep = (diff > 0) if reverse else (diff >= 0)
    decay = jnp.where(jnp.asarray(keep)[None], jnp.exp(lg[:, None, None] * np.maximum(diff, 0).astype(np.float32)), 0.0)
    tf = t.astype(np.float32)
    qpow = (c - tf) if reverse else (tf + 1.0)
    kpow = tf if reverse else (c - 1.0 - tf)
    qdec = jnp.repeat(jnp.exp(lg[None, :] * qpow[:, None]), R_QK_DIM, axis=1)
    kdec = jnp.repeat(jnp.exp(lg[None, :] * kpow[:, None]), R_QK_DIM, axis=1)
    rdec = jnp.broadcast_to(jnp.repeat(jnp.exp(lg * c), R_QK_DIM)[:, None], (R_QK_W, R_W))
    return decay.reshape(R_HEADS * c, c), qdec, kdec, rdec


def _retention(pb, cos, sin, lg_f, lg_b):
    b, s, _ = pb.shape
    rows = R_CHUNK * R_STEP_CHUNKS
    assert s % rows == 0
    nt = s // rows
    hd = np.arange(R_QK_W)[:, None] // R_QK_DIM == np.arange(R_W)[None, :] // R_V_DIM
    mask = jnp.asarray(hd.astype(np.float32))
    gh = np.arange(R_W)[:, None] // R_V_DIM == np.arange(R_W)[None, :] // R_V_DIM
    avg = jnp.asarray(gh.astype(np.float32) / R_V_DIM, dtype=BF16)
    const = lambda shape: pl.BlockSpec(shape, lambda bi, i: (0,) * len(shape))

    def specs(order):
        seq = lambda w, cb: pl.BlockSpec((1, rows, w), lambda bi, i: (bi, order(i), cb))
        tab = pl.BlockSpec((rows, R_QK_W), lambda bi, i: (order(i), 0))
        return seq, [seq(R_QK_W, 0), seq(R_QK_W, 1), seq(R_W, 1), tab, tab,
                     const((R_HEADS * R_CHUNK, R_CHUNK)), const((R_CHUNK, R_QK_W)), const((R_CHUNK, R_QK_W)),
                     const((R_QK_W, R_W)), const((R_QK_W, R_W))]

    seq, in_specs = specs(lambda i: i)
    o_fwd = pl.pallas_call(
        functools.partial(_retention_fwd_kernel, n_chunks=R_STEP_CHUNKS),
        grid=(b, nt),
        in_specs=in_specs,
        out_specs=seq(R_W, 0),
        out_shape=jax.ShapeDtypeStruct((b, s, R_W), F32),
        scratch_shapes=[pltpu.VMEM((R_QK_W, R_W), F32)],
        compiler_params=_cparams("parallel", "arbitrary"),
        name="retention_fwd",
    )(pb, pb, pb, cos, sin, *_retention_tables(lg_f, False), mask)

    seq, in_specs = specs(lambda i: nt - 1 - i)
    out = pl.pallas_call(
        functools.partial(_retention_bwd_kernel, n_chunks=R_STEP_CHUNKS),
        grid=(b, nt),
        in_specs=in_specs + [seq(R_W, 0), seq(R_W, 2), const((R_W, R_W))],
        out_specs=seq(R_W, 0),
        out_shape=jax.ShapeDtypeStruct((b, s, R_W), BF16),
        scratch_shapes=[pltpu.VMEM((R_QK_W, R_W), F32)],
        compiler_params=_cparams("parallel", "arbitrary"),
        name="retention_bwd",
    )(pb, pb, pb, cos, sin, *_retention_tables(lg_b, True), mask, o_fwd, pb, avg)
    return out.reshape(b * s, R_W)


def _rope_tables(s):
    half = R_QK_DIM // 2
    freqs = ROPE_BASE ** (-jnp.arange(half, dtype=F32) / half)
    ang = jnp.arange(s, dtype=F32)[:, None] * freqs[None]
    cos, sin = jnp.cos(ang), jnp.sin(ang)
    return (jnp.tile(jnp.concatenate([cos, cos], axis=1), (1, R_HEADS)),
            jnp.tile(jnp.concatenate([-sin, sin], axis=1), (1, R_HEADS)))


def _merge_out_kernel(o1_ref, o2_ref, o3_ref, l1_ref, l2_ref, l3_ref, r_ref, c_ref, x_ref, w_ref, g_ref,
                      expand_ref, x1_ref, h2_ref):
    lses = (l1_ref[...], l2_ref[...], l3_ref[...])
    top = jnp.maximum(jnp.maximum(lses[0], lses[1]), lses[2])
    ws = [jnp.exp(l - top) for l in lses]
    den = ws[0] + ws[1] + ws[2]
    expand = expand_ref[...]
    a = None
    for wgt, o_ref in zip(ws, (o1_ref, o2_ref, o3_ref)):
        term = _split_dot(wgt / den, expand) * o_ref[...].astype(F32)
        a = term if a is None else a + term
    a = a.astype(BF16)
    r = r_ref[...]
    c = c_ref[...]
    nc = 512
    for n0 in range(0, D_MODEL, nc):
        cols = slice(n0, n0 + nc)
        y = jnp.dot(a, w_ref[0:A_W, cols], preferred_element_type=F32)
        y = y + jnp.dot(r, w_ref[A_W:A_W + R_W, cols], preferred_element_type=F32)
        y = y + jnp.dot(c, w_ref[A_W + R_W:MIX_W, cols], preferred_element_type=F32)
        x1_ref[:, cols] = x_ref[:, cols] + y
    x1 = x1_ref[...]
    h2 = x1 * lax.rsqrt(jnp.mean(x1 * x1, axis=-1, keepdims=True) + EPS) * g_ref[...]
    h2_ref[...] = h2.astype(BF16)


def _merge_out(branches, r, c, x, w, g, tm=512):
    t = x.shape[0]
    assert t % tm == 0
    expand = jnp.asarray((np.arange(LANES)[:, None] == np.arange(A_W)[None, :] // HEAD_DIM).astype(np.float32), dtype=BF16)
    row = lambda w_: pl.BlockSpec((tm, w_), lambda i: (i, 0))
    (o1, l1), (o2, l2), (o3, l3) = branches
    return pl.pallas_call(
        _merge_out_kernel,
        grid=(t // tm,),
        in_specs=[row(A_W), row(A_W), row(A_W), row(LANES), row(LANES), row(LANES), row(R_W), row(C_W),
                  row(D_MODEL),
                  pl.BlockSpec((MIX_W, D_MODEL), lambda i: (0, 0), pipeline_mode=pl.Buffered(1)),
                  pl.BlockSpec((1, D_MODEL), lambda i: (0, 0)),
                  pl.BlockSpec((LANES, A_W), lambda i: (0, 0))],
        out_specs=[row(D_MODEL), row(D_MODEL)],
        out_shape=[jax.ShapeDtypeStruct((t, D_MODEL), F32), jax.ShapeDtypeStruct((t, D_MODEL), BF16)],
        compiler_params=_cparams("parallel"),
        name="merge_out_proj",
    )(o1, o2, o3, l1, l2, l3, r, c, x, w, g.reshape(1, D_MODEL), expand)


def _ffn_kernel(h_ref, x_ref, wg_ref, wu_ref, wd_ref, o_ref):
    k = pl.program_id(1)
    h = h_ref[...]
    g = jnp.dot(h, wg_ref[...], preferred_element_type=F32)
    u = jnp.dot(h, wu_ref[...], preferred_element_type=F32)
    a = (g / (1.0 + jnp.exp(-g)) * u).astype(BF16)
    y = jnp.dot(a, wd_ref[...], preferred_element_type=F32)

    @pl.when(k == 0)
    def _():
        o_ref[...] = x_ref[...] + y

    @pl.when(k > 0)
    def _():
        o_ref[...] += y


def _ffn(h, x, wg, wu, wd, tm=512, tf=512):
    t = x.shape[0]
    assert t % tm == 0 and D_FF % tf == 0
    return pl.pallas_call(
        _ffn_kernel,
        grid=(t // tm, D_FF // tf),
        in_specs=[
            pl.BlockSpec((tm, D_MODEL), lambda i, k: (i, 0)),
            pl.BlockSpec((tm, D_MODEL), lambda i, k: (i, 0)),
            pl.BlockSpec((D_MODEL, tf), lambda i, k: (0, k)),
            pl.BlockSpec((D_MODEL, tf), lambda i, k: (0, k)),
            pl.BlockSpec((tf, D_MODEL), lambda i, k: (k, 0)),
        ],
        out_specs=pl.BlockSpec((tm, D_MODEL), lambda i, k: (i, 0)),
        out_shape=jax.ShapeDtypeStruct((t, D_MODEL), F32),
        compiler_params=_cparams("parallel", "arbitrary"),
        name="swiglu_ffn",
    )(h, x, wg, wu, wd)


def _final_norm_kernel(x_ref, g_ref, o_ref):
    x = x_ref[...]
    o_ref[...] = x * lax.rsqrt(jnp.mean(x * x, axis=-1, keepdims=True) + EPS) * g_ref[...]


def _final_norm(x, g, tm=512):
    t = x.shape[0]
    assert t % tm == 0
    return pl.pallas_call(
        _final_norm_kernel,
        grid=(t // tm,),
        in_specs=[pl.BlockSpec((tm, D_MODEL), lambda i: (i, 0)), pl.BlockSpec((1, D_MODEL), lambda i: (0, 0))],
        out_specs=pl.BlockSpec((tm, D_MODEL), lambda i: (i, 0)),
        out_shape=jax.ShapeDtypeStruct((t, D_MODEL), F32),
        compiler_params=_cparams("parallel"),
        name="final_norm",
    )(x, g.reshape(1, D_MODEL))


def _layer(x, b, s, p):
    pa, pb, pc = _norm_proj(x, p["g1"], p["w_in"])
    pa = pa.reshape(b, s, PA_W)
    branches = [_dilated_branch(pa, bias, dil) for bias, (_, dil) in zip(p["bias_a"], A_BRANCHES)]
    r = _retention(pb.reshape(b, s, PB_W), p["cos"], p["sin"], p["lg_f"], p["lg_b"])
    c = _window_attn(pc.reshape(b, s, PC_W), p["bias_c"], p["sink"])
    x1, h2 = _merge_out(branches, r, c, x, p["w_out"], p["g2"])
    return _ffn(h2, x1, p["w_gate"], p["w_up"], p["w_down"])


def kernel(x_prompt, x_sample, rel_bias, norm1_g, w_in, ret_decay_fwd, ret_decay_bwd, attn_sink, w_out, norm2_g,
           w_gate, w_up, w_down, final_norm_g):
    bias_a = [_band_bias(rel_bias[:, :A_HEADS], A_TQ, A_TK, A_RADIUS, A_RADIUS, dil) for _, dil in A_BRANCHES]
    bias_c = _band_bias(rel_bias[:, A_HEADS:], C_TQ, C_TK, C_TQ, C_RADIUS, 1)
    layers = []
    for i in range(DEPTH):
        layers.append(dict(
            g1=norm1_g[i], g2=norm2_g[i], sink=attn_sink[i], bias_a=bias_a, bias_c=bias_c,
            w_in=w_in[i].astype(BF16), w_out=w_out[i].astype(BF16),
            w_gate=w_gate[i].astype(BF16), w_up=w_up[i].astype(BF16), w_down=w_down[i].astype(BF16),
            lg_f=jnp.log1p(-jnp.exp2(-ret_decay_fwd[i].astype(F32))),
            lg_b=jnp.log1p(-jnp.exp2(-ret_decay_bwd[i].astype(F32))),
        ))

    def trunk(x):
        b, s, _ = x.shape
        cos, sin = _rope_tables(s)
        x = x.reshape(b * s, D_MODEL)
        for p in layers:
            x = _layer(x, b, s, dict(p, cos=cos, sin=sin))
        return _final_norm(x, final_norm_g).reshape(b, s, D_MODEL)

    return trunk(x_prompt), trunk(x_sample)
```

```python
import functools
import math

import numpy as np
import jax
import jax.numpy as jnp
from jax import lax
from jax.experimental import pallas as pl
from jax.experimental.pallas import tpu as pltpu

D_MODEL = 2048
DEPTH = 2
HEAD_DIM = 64
A_HEADS = 12
A_BRANCHES = ((128, 1), (512, 4), (2048, 16))
A_RADIUS = 64
R_HEADS = 8
R_QK_DIM = 32
R_V_DIM = 64
R_CHUNK = 128
ROPE_BASE = 10000.0
C_HEADS = 12
C_KV_HEADS = 4
C_GROUP = C_HEADS // C_KV_HEADS
C_RADIUS = 128
REL_BUCKETS = 32
REL_MAX_DIST = 1024
D_FF = 5632
EPS = 1e-6
GN_EPS = 1e-5
NEG = -1e30

A_W = A_HEADS * HEAD_DIM
R_QK_W = R_HEADS * R_QK_DIM
R_W = R_HEADS * R_V_DIM
C_W = C_HEADS * HEAD_DIM
C_KV_W = C_KV_HEADS * HEAD_DIM
PA_W = 3 * A_W
PB_W = 2 * R_QK_W + 2 * R_W
PC_W = C_W + 2 * C_KV_W
IN_COLS = PA_W + PB_W + PC_W
MIX_W = A_W + R_W + C_W

LANES = 128
V7X_VMEM_BYTES = 64 * 1024 * 1024
VMEM_LIMIT = V7X_VMEM_BYTES - 8 * 1024 * 1024

A_TQ = 128
A_TK = A_TQ + 2 * A_RADIUS
A_SUB = 2
C_TQ = 128
C_TK = 3 * C_TQ
C_Q_ORDER = tuple(kp * C_GROUP + j + e * C_GROUP
                  for kp in range(0, C_KV_HEADS, 2) for j in range(C_GROUP) for e in range(2))
R_STEP_CHUNKS = 4

LOG2E = 1.4426950408889634
QSCALE = HEAD_DIM ** -0.5 * LOG2E

BF16 = jnp.bfloat16
F32 = jnp.float32


def _cparams(*sem):
    return pltpu.CompilerParams(dimension_semantics=sem, vmem_limit_bytes=VMEM_LIMIT)


def _t5_bucket(rel):
    nb = REL_BUCKETS // 2
    max_exact = nb // 2
    ret = np.where(rel > 0, nb, 0)
    n = np.abs(rel)
    nf = np.maximum(n, 1).astype(np.float32)
    large = max_exact + (np.log(nf / max_exact) / math.log(REL_MAX_DIST / max_exact) * (nb - max_exact)).astype(np.int32)
    large = np.minimum(large, nb - 1)
    return (ret + np.where(n < max_exact, n, large)).astype(np.int32)


def _band_bias(table, tq, tk, lead, radius, dil):
    h = table.shape[1]
    n = tq + tk - 1
    offs = np.arange(n) - (tq - 1) - lead
    per_off = jnp.take(table.astype(F32), jnp.asarray(_t5_bucket(offs * dil)), axis=0).T * LOG2E
    padded = jnp.concatenate([per_off, jnp.zeros((h, 1), F32)], axis=1)
    toep = jnp.tile(padded, (1, tq))[:, :tq * n].reshape(h, tq, n)[:, :, tq - 1:tq - 1 + tk]
    off = np.arange(tk)[None, :] - lead - np.arange(tq)[:, None]
    band = np.abs(off) <= radius
    col = np.arange(tk)[None, :]
    first = band & (col >= lead)
    last = band & (col < tk - lead)
    masks = np.stack([band, first, last, first & last])
    return jnp.where(jnp.asarray(masks)[:, None], toep[None], NEG).swapaxes(-1, -2)


def _pair_heads(bias):
    v, h, tk, tq = bias.shape
    return bias.reshape(v, h // 2, 2, tk, tq).swapaxes(2, 3).reshape(v, h // 2, tk, 2 * tq)


def _edge_variant(i, n):
    return (i == 0).astype(jnp.int32) + 2 * (i == n - 1).astype(jnp.int32)


_PROJ_CHUNKS = (
    (0, 0, 0, 768, QSCALE), (0, 768, 768, 768, None), (0, 1536, 1536, 768, None),
    (1, 0, 2304, 512, None), (1, 512, 2816, 512, None), (1, 1024, 3328, 512, None),
    (2, 0, 3840, 768, QSCALE), (2, 768, 4608, 512, None),
)


def _norm_proj_kernel(x_ref, g_ref, w_ref, pa_ref, pb_ref, pc_ref):
    x = x_ref[...]
    h = (x * lax.rsqrt(jnp.mean(x * x, axis=-1, keepdims=True) + EPS) * g_ref[...]).astype(BF16)
    outs = (pa_ref, pb_ref, pc_ref)
    for oi, oc, wc, width, scale in _PROJ_CHUNKS:
        y = jnp.dot(h, w_ref[:, wc:wc + width], preferred_element_type=F32)
        if scale is not None:
            y = y * scale
        outs[oi][:, oc:oc + width] = y.astype(BF16)


def _norm_proj(x, g, w, tm=512):
    t = x.shape[0]
    assert t % tm == 0
    return pl.pallas_call(
        _norm_proj_kernel,
        grid=(t // tm,),
        in_specs=[
            pl.BlockSpec((tm, D_MODEL), lambda i: (i, 0)),
            pl.BlockSpec((1, D_MODEL), lambda i: (0, 0)),
            pl.BlockSpec((D_MODEL, IN_COLS), lambda i: (0, 0), pipeline_mode=pl.Buffered(1)),
        ],
        out_specs=[
            pl.BlockSpec((tm, PA_W), lambda i: (i, 0)),
            pl.BlockSpec((tm, PB_W), lambda i: (i, 0)),
            pl.BlockSpec((tm, PC_W), lambda i: (i, 0)),
        ],
        out_shape=[
            jax.ShapeDtypeStruct((t, PA_W), BF16),
            jax.ShapeDtypeStruct((t, PB_W), BF16),
            jax.ShapeDtypeStruct((t, PC_W), BF16),
        ],
        compiler_params=_cparams("parallel"),
        name="norm_in_proj",
    )(x, g.reshape(1, D_MODEL), w)


def _nt_dot(a, b):
    return lax.dot_general(a, b, (((1,), (1,)), ((), ())), preferred_element_type=F32)


def _masked_head_pair(qpair):
    half = lax.broadcasted_iota(jnp.int32, qpair.shape, 1) // HEAD_DIM
    zero = jnp.zeros_like(qpair)
    return jnp.concatenate([jnp.where(half == e, qpair, zero) for e in range(2)], axis=0)


def _dilated_attn_kernel(q_ref, kp_ref, kc_ref, kn_ref, vp_ref, vc_ref, vn_ref, bias_lo_ref, bias_hi_ref,
                         o_ref, lse_ref):
    kw = jnp.concatenate([kp_ref[0], kc_ref[0], kn_ref[0]], axis=0)
    vt = jnp.concatenate([vp_ref[0], vc_ref[0], vn_ref[0]], axis=0).T
    for sub, bias_ref in enumerate((bias_lo_ref, bias_hi_ref)):
        rows = slice(sub * A_TQ, (sub + 1) * A_TQ)
        keys = slice(sub * A_TQ, sub * A_TQ + A_TK)
        lses = []
        for j in range(A_HEADS // 2):
            lanes = slice(j * LANES, (j + 1) * LANES)
            s = _nt_dot(kw[keys, lanes], _masked_head_pair(q_ref[0, rows, lanes])) + bias_ref[j]
            m = jnp.max(s, axis=0, keepdims=True)
            p = jnp.exp2(s - m)
            l = jnp.sum(p, axis=0, keepdims=True)
            pb = p.astype(BF16)
            o_t = []
            for e in range(2):
                h = 2 * j + e
                cols = slice(e * A_TQ, (e + 1) * A_TQ)
                o = jnp.dot(vt[h * HEAD_DIM:(h + 1) * HEAD_DIM, keys], pb[:, cols], preferred_element_type=F32)
                o_t.append(o / l[:, cols])
                lses.append(m[:, cols] + jnp.log2(l[:, cols]))
            o_ref[0, rows, lanes] = jnp.concatenate(o_t, axis=0).T.astype(BF16)
        lse_t = jnp.concatenate(lses + [jnp.zeros((LANES - A_HEADS, A_TQ), F32)], axis=0)
        lse_ref[0, rows, :] = lse_t.T


def _dilated_branch(pa, bias, dil):
    b, s, _ = pa.shape
    l = s // dil
    step = A_SUB * A_TQ
    assert A_SUB == 2 and s % dil == 0 and l % step == 0
    nt = l // step
    halo = A_RADIUS
    hb = step // halo
    pav = pa.reshape(b, l, dil * PA_W)
    nblk = PA_W // A_W

    def cur(c):
        return lambda bi, r, i: (bi, i, r * nblk + c)

    def prev(c):
        return lambda bi, r, i: (bi, jnp.maximum(i * hb - 1, 0), r * nblk + c)

    def nxt(c):
        return lambda bi, r, i: (bi, jnp.minimum((i + 1) * hb, nt * hb - 1), r * nblk + c)

    tile = lambda c: pl.BlockSpec((1, step, A_W), cur(c))
    bias_spec = lambda variant: pl.BlockSpec((None, A_HEADS // 2, A_TK, 2 * A_TQ),
                                             lambda bi, r, i: (variant(i), 0, 0, 0))
    o, lse = pl.pallas_call(
        _dilated_attn_kernel,
        grid=(b, dil, nt),
        in_specs=[
            tile(0),
            pl.BlockSpec((1, halo, A_W), prev(1)), tile(1), pl.BlockSpec((1, halo, A_W), nxt(1)),
            pl.BlockSpec((1, halo, A_W), prev(2)), tile(2), pl.BlockSpec((1, halo, A_W), nxt(2)),
            bias_spec(lambda i: (i == 0).astype(jnp.int32)),
            bias_spec(lambda i: 2 * (i == nt - 1).astype(jnp.int32)),
        ],
        out_specs=[
            pl.BlockSpec((1, step, A_W), lambda bi, r, i: (bi, i, r)),
            pl.BlockSpec((1, step, LANES), lambda bi, r, i: (bi, i, r)),
        ],
        out_shape=[
            jax.ShapeDtypeStruct((b, l, dil * A_W), BF16),
            jax.ShapeDtypeStruct((b, l, dil * LANES), F32),
        ],
        compiler_params=_cparams("parallel", "parallel", "arbitrary"),
        name=f"dilated_attn_d{dil}",
    )(pav, pav, pav, pav, pav, pav, pav, bias, bias)
    return o.reshape(b * s, A_W), lse.reshape(b * s, LANES)


def _window_attn_kernel(sink_ref, q_ref, kp_ref, kc_ref, kn_ref, vp_ref, vc_ref, vn_ref, bias_ref, o_ref):
    kw = jnp.concatenate([kp_ref[0], kc_ref[0], kn_ref[0]], axis=0)
    vt = jnp.concatenate([vp_ref[0], vc_ref[0], vn_ref[0]], axis=0).T
    for pair in range(C_HEADS // 2):
        lanes = slice(pair * LANES, (pair + 1) * LANES)
        kv_pair = pair // C_GROUP
        s = _nt_dot(kw[:, kv_pair * LANES:(kv_pair + 1) * LANES], _masked_head_pair(q_ref[0, :, lanes]))
        s = s + bias_ref[pair]
        sk = jnp.concatenate([jnp.full((1, C_TQ), sink_ref[2 * pair + e], F32) for e in range(2)], axis=1)
        m = jnp.maximum(jnp.max(s, axis=0, keepdims=True), sk)
        p = jnp.exp2(s - m)
        den = jnp.sum(p, axis=0, keepdims=True) + jnp.exp2(sk - m)
        pb = p.astype(BF16)
        o_t = []
        for e in range(2):
            kv = 2 * kv_pair + e
            cols = slice(e * C_TQ, (e + 1) * C_TQ)
            o = jnp.dot(vt[kv * HEAD_DIM:(kv + 1) * HEAD_DIM, :], pb[:, cols], preferred_element_type=F32)
            o_t.append(o / den[:, cols])
        o_ref[0, :, lanes] = jnp.concatenate(o_t, axis=0).T.astype(BF16)


def _window_attn(pc, bias, sink):
    b, s, _ = pc.shape
    assert s % C_TQ == 0
    nt = s // C_TQ
    kblk = C_W // C_KV_W

    def cur(c):
        return lambda bi, i: (bi, i, c)

    def prev(c):
        return lambda bi, i: (bi, jnp.maximum(i - 1, 0), c)

    def nxt(c):
        return lambda bi, i: (bi, jnp.minimum(i + 1, nt - 1), c)

    kv_spec = lambda m: pl.BlockSpec((1, C_TQ, C_KV_W), m)
    o = pl.pallas_call(
        _window_attn_kernel,
        grid=(b, nt),
        in_specs=[
            pl.BlockSpec(memory_space=pltpu.SMEM),
            pl.BlockSpec((1, C_TQ, C_W), cur(0)),
            kv_spec(prev(kblk)), kv_spec(cur(kblk)), kv_spec(nxt(kblk)),
            kv_spec(prev(kblk + 1)), kv_spec(cur(kblk + 1)), kv_spec(nxt(kblk + 1)),
            pl.BlockSpec((None, C_HEADS // 2, C_TK, 2 * C_TQ), lambda bi, i: (_edge_variant(i, nt), 0, 0, 0)),
        ],
        out_specs=pl.BlockSpec((1, C_TQ, C_W), lambda bi, i: (bi, i, 0)),
        out_shape=jax.ShapeDtypeStruct((b, s, C_W), BF16),
        compiler_params=_cparams("parallel", "arbitrary"),
        name="window_gqa",
    )(sink.astype(F32), pc, pc, pc, pc, pc, pc, pc, bias)
    return o.reshape(b * s, C_W)


def _rope(x, cos, sin, lane_lo):
    parts = []
    for a in range(0, R_QK_W, LANES):
        xh = x[:, a:a + LANES]
        half = R_QK_DIM // 2
        parts.append(jnp.where(lane_lo, pltpu.roll(xh, LANES - half, 1), pltpu.roll(xh, half, 1)))
    return x * cos + jnp.concatenate(parts, axis=1) * sin


def _retention_chunk(q, k, v, cos, sin, decay, qdec, kdec, rdec, mask, state_ref):
    c = R_CHUNK
    lane = lax.broadcasted_iota(jnp.int32, (c, LANES), 1)
    lane_lo = (lane % R_QK_DIM) < (R_QK_DIM // 2)
    q = _rope(q.astype(F32), cos, sin, lane_lo)
    k = _rope(k.astype(F32), cos, sin, lane_lo) * (R_QK_DIM ** -0.5)
    state = state_ref[...]
    cross = jnp.dot((q * qdec).astype(BF16), state.astype(BF16), preferred_element_type=F32)

    qb = q.astype(BF16)
    kb = k.astype(BF16)
    head = lax.broadcasted_iota(jnp.int32, (c, R_QK_W), 1) // R_QK_DIM
    qs = jnp.concatenate([jnp.where(head == h, qb, jnp.zeros_like(qb)) for h in range(R_HEADS)], axis=0)
    s = lax.dot_general(qs, kb, (((1,), (1,)), ((), ())), preferred_element_type=F32)
    p = (s * decay).astype(BF16)

    vlane = lax.broadcasted_iota(jnp.int32, (c, R_W), 1) % LANES
    v_lo = jnp.where(vlane < R_V_DIM, v, jnp.zeros_like(v))
    v_hi = jnp.where(vlane >= R_V_DIM, v, jnp.zeros_like(v))
    pieces = []
    for j in range(R_HEADS // 2):
        cs = slice(j * LANES, (j + 1) * LANES)
        a = jnp.dot(p[(2 * j) * c:(2 * j + 1) * c], v_lo[:, cs], preferred_element_type=F32)
        a = a + jnp.dot(p[(2 * j + 1) * c:(2 * j + 2) * c], v_hi[:, cs], preferred_element_type=F32)
        pieces.append(a)
    intra = jnp.concatenate(pieces, axis=1)

    kv = lax.dot_general((k * kdec).astype(BF16), v, (((0,), (0,)), ((), ())), preferred_element_type=F32)
    state_ref[...] = state * rdec + kv * mask
    return cross + intra


def _retention_fwd_kernel(q_ref, k_ref, v_ref, cos_ref, sin_ref, decay_ref, qdec_ref, kdec_ref, rdec_ref,
                          mask_ref, o_ref, state_ref, *, n_chunks):
    @pl.when(pl.program_id(1) == 0)
    def _():
        state_ref[...] = jnp.zeros_like(state_ref)

    for ci in range(n_chunks):
        rows = slice(ci * R_CHUNK, (ci + 1) * R_CHUNK)
        o_ref[0, rows, :] = _retention_chunk(
            q_ref[0, rows, :], k_ref[0, rows, :], v_ref[0, rows, :], cos_ref[rows, :], sin_ref[rows, :],
            decay_ref[...], qdec_ref[...], kdec_ref[...], rdec_ref[...], mask_ref[...], state_ref)


def _split_dot(x, w):
    hi = x.astype(BF16)
    lo = (x - hi.astype(F32)).astype(BF16)
    return jnp.dot(hi, w, preferred_element_type=F32) + jnp.dot(lo, w, preferred_element_type=F32)


def _retention_bwd_kernel(q_ref, k_ref, v_ref, cos_ref, sin_ref, decay_ref, qdec_ref, kdec_ref, rdec_ref,
                          mask_ref, fwd_ref, gate_ref, avg_ref, o_ref, state_ref, *, n_chunks):
    @pl.when(pl.program_id(1) == 0)
    def _():
        state_ref[...] = jnp.zeros_like(state_ref)

    for ci in reversed(range(n_chunks)):
        rows = slice(ci * R_CHUNK, (ci + 1) * R_CHUNK)
        o = fwd_ref[0, rows, :] + _retention_chunk(
            q_ref[0, rows, :], k_ref[0, rows, :], v_ref[0, rows, :], cos_ref[rows, :], sin_ref[rows, :],
            decay_ref[...], qdec_ref[...], kdec_ref[...], rdec_ref[...], mask_ref[...], state_ref)
        avg = avg_ref[...]
        d = o - _split_dot(o, avg)
        y = d * lax.rsqrt(_split_dot(d * d, avg) + GN_EPS)
        g = gate_ref[0, rows, :].astype(F32)
        o_ref[0, rows, :] = (y * (g / (1.0 + jnp.exp(-g)))).astype(BF16)


def _retention_tables(lg, reverse):
    c = R_CHUNK
    t = np.arange(c)
    diff = (t[None, :] - t[:, None]) if reverse else (t[:, None] - t[None, :])
    keep = (diff > 0) if reverse else (diff >= 0)
    decay = jnp.where(jnp.asarray(keep)[None], jnp.exp(lg[:, None, None] * np.maximum(diff, 0).astype(np.float32)), 0.0)
    tf = t.astype(np.float32)
    qpow = (c - tf) if reverse else (tf + 1.0)
    kpow = tf if reverse else (c - 1.0 - tf)
    qdec = jnp.repeat(jnp.exp(lg[None, :] * qpow[:, None]), R_QK_DIM, axis=1)
    kdec = jnp.repeat(jnp.exp(lg[None, :] * kpow[:, None]), R_QK_DIM, axis=1)
    rdec = jnp.broadcast_to(jnp.repeat(jnp.exp(lg * c), R_QK_DIM)[:, None], (R_QK_W, R_W))
    return decay.reshape(R_HEADS * c, c), qdec, kdec, rdec


def _retention(pb, cos, sin, lg_f, lg_b):
    b, s, _ = pb.shape
    rows = R_CHUNK * R_STEP_CHUNKS
    assert s % rows == 0
    nt = s // rows
    hd = np.arange(R_QK_W)[:, None] // R_QK_DIM == np.arange(R_W)[None, :] // R_V_DIM
    mask = jnp.asarray(hd.astype(np.float32))
    gh = np.arange(R_W)[:, None] // R_V_DIM == np.arange(R_W)[None, :] // R_V_DIM
    avg = jnp.asarray(gh.astype(np.float32) / R_V_DIM, dtype=BF16)
    const = lambda shape: pl.BlockSpec(shape, lambda bi, i: (0,) * len(shape))

    def specs(order):
        seq = lambda w, cb: pl.BlockSpec((1, rows, w), lambda bi, i: (bi, order(i), cb))
        tab = pl.BlockSpec((rows, R_QK_W), lambda bi, i: (order(i), 0))
        return seq, [seq(R_QK_W, 0), seq(R_QK_W, 1), seq(R_W, 1), tab, tab,
                     const((R_HEADS * R_CHUNK, R_CHUNK)), const((R_CHUNK, R_QK_W)), const((R_CHUNK, R_QK_W)),
                     const((R_QK_W, R_W)), const((R_QK_W, R_W))]

    seq, in_specs = specs(lambda i: i)
    o_fwd = pl.pallas_call(
        functools.partial(_retention_fwd_kernel, n_chunks=R_STEP_CHUNKS),
        grid=(b, nt),
        in_specs=in_specs,
        out_specs=seq(R_W, 0),
        out_shape=jax.ShapeDtypeStruct((b, s, R_W), F32),
        scratch_shapes=[pltpu.VMEM((R_QK_W, R_W), F32)],
        compiler_params=_cparams("parallel", "arbitrary"),
        name="retention_fwd",
    )(pb, pb, pb, cos, sin, *_retention_tables(lg_f, False), mask)

    seq, in_specs = specs(lambda i: nt - 1 - i)
    out = pl.pallas_call(
        functools.partial(_retention_bwd_kernel, n_chunks=R_STEP_CHUNKS),
        grid=(b, nt),
        in_specs=in_specs + [seq(R_W, 0), seq(R_W, 2), const((R_W, R_W))],
        out_specs=seq(R_W, 0),
        out_shape=jax.ShapeDtypeStruct((b, s, R_W), BF16),
        scratch_shapes=[pltpu.VMEM((R_QK_W, R_W), F32)],
        compiler_params=_cparams("parallel", "arbitrary"),
        name="retention_bwd",
    )(pb, pb, pb, cos, sin, *_retention_tables(lg_b, True), mask, o_fwd, pb, avg)
    return out.reshape(b * s, R_W)


def _rope_tables(s):
    half = R_QK_DIM // 2
    freqs = ROPE_BASE ** (-jnp.arange(half, dtype=F32) / half)
    ang = jnp.arange(s, dtype=F32)[:, None] * freqs[None]
    cos, sin = jnp.cos(ang), jnp.sin(ang)
    return (jnp.tile(jnp.concatenate([cos, cos], axis=1), (1, R_HEADS)),
            jnp.tile(jnp.concatenate([-sin, sin], axis=1), (1, R_HEADS)))


def _merge_out_kernel(o1_ref, o2_ref, o3_ref, l1_ref, l2_ref, l3_ref, r_ref, c_ref, x_ref, w_ref, g_ref,
                      expand_ref, x1_ref, h2_ref):
    lses = (l1_ref[...], l2_ref[...], l3_ref[...])
    top = jnp.maximum(jnp.maximum(lses[0], lses[1]), lses[2])
    ws = [jnp.exp2(l - top) for l in lses]
    den = ws[0] + ws[1] + ws[2]
    expand = expand_ref[...]
    a = None
    for wgt, o_ref in zip(ws, (o1_ref, o2_ref, o3_ref)):
        term = _split_dot(wgt / den, expand) * o_ref[...].astype(F32)
        a = term if a is None else a + term
    a = a.astype(BF16)
    r = r_ref[...]
    c = c_ref[...]
    nc = 512
    for n0 in range(0, D_MODEL, nc):
        cols = slice(n0, n0 + nc)
        y = jnp.dot(a, w_ref[0:A_W, cols], preferred_element_type=F32)
        y = y + jnp.dot(r, w_ref[A_W:A_W + R_W, cols], preferred_element_type=F32)
        y = y + jnp.dot(c, w_ref[A_W + R_W:MIX_W, cols], preferred_element_type=F32)
        x1_ref[:, cols] = x_ref[:, cols] + y
    x1 = x1_ref[...]
    h2 = x1 * lax.rsqrt(jnp.mean(x1 * x1, axis=-1, keepdims=True) + EPS) * g_ref[...]
    h2_ref[...] = h2.astype(BF16)


def _merge_out(branches, r, c, x, w, g, tm=512):
    t = x.shape[0]
    assert t % tm == 0
    expand = jnp.asarray((np.arange(LANES)[:, None] == np.arange(A_W)[None, :] // HEAD_DIM).astype(np.float32), dtype=BF16)
    row = lambda w_: pl.BlockSpec((tm, w_), lambda i: (i, 0))
    (o1, l1), (o2, l2), (o3, l3) = branches
    return pl.pallas_call(
        _merge_out_kernel,
        grid=(t // tm,),
        in_specs=[row(A_W), row(A_W), row(A_W), row(LANES), row(LANES), row(LANES), row(R_W), row(C_W),
                  row(D_MODEL),
                  pl.BlockSpec((MIX_W, D_MODEL), lambda i: (0, 0), pipeline_mode=pl.Buffered(1)),
                  pl.BlockSpec((1, D_MODEL), lambda i: (0, 0)),
                  pl.BlockSpec((LANES, A_W), lambda i: (0, 0))],
        out_specs=[row(D_MODEL), row(D_MODEL)],
        out_shape=[jax.ShapeDtypeStruct((t, D_MODEL), F32), jax.ShapeDtypeStruct((t, D_MODEL), BF16)],
        compiler_params=_cparams("parallel"),
        name="merge_out_proj",
    )(o1, o2, o3, l1, l2, l3, r, c, x, w, g.reshape(1, D_MODEL), expand)


def _ffn_accumulate(h_ref, x_ref, wg_ref, wu_ref, wd_ref, o_ref):
    k = pl.program_id(1)
    h = h_ref[...]
    g = jnp.dot(h, wg_ref[...], preferred_element_type=F32)
    u = jnp.dot(h, wu_ref[...], preferred_element_type=F32)
    a = (g / (1.0 + jnp.exp(-g)) * u).astype(BF16)
    y = jnp.dot(a, wd_ref[...], preferred_element_type=F32)

    @pl.when(k == 0)
    def _():
        o_ref[...] = x_ref[...] + y

    @pl.when(k > 0)
    def _():
        o_ref[...] += y


def _ffn_kernel(h_ref, x_ref, wg_ref, wu_ref, wd_ref, o_ref):
    _ffn_accumulate(h_ref, x_ref, wg_ref, wu_ref, wd_ref, o_ref)


def _ffn_norm_kernel(h_ref, x_ref, wg_ref, wu_ref, wd_ref, g_ref, o_ref):
    _ffn_accumulate(h_ref, x_ref, wg_ref, wu_ref, wd_ref, o_ref)

    @pl.when(pl.program_id(1) == pl.num_programs(1) - 1)
    def _():
        x = o_ref[...]
        o_ref[...] = x * lax.rsqrt(jnp.mean(x * x, axis=-1, keepdims=True) + EPS) * g_ref[...]


def _ffn(h, x, wg, wu, wd, out_norm_g=None, tm=512, tf=512):
    t = x.shape[0]
    assert t % tm == 0 and D_FF % tf == 0
    in_specs = [
        pl.BlockSpec((tm, D_MODEL), lambda i, k: (i, 0)),
        pl.BlockSpec((tm, D_MODEL), lambda i, k: (i, 0)),
        pl.BlockSpec((D_MODEL, tf), lambda i, k: (0, k)),
        pl.BlockSpec((D_MODEL, tf), lambda i, k: (0, k)),
        pl.BlockSpec((tf, D_MODEL), lambda i, k: (k, 0)),
    ]
    args = [h, x, wg, wu, wd]
    body = _ffn_kernel
    if out_norm_g is not None:
        in_specs.append(pl.BlockSpec((1, D_MODEL), lambda i, k: (0, 0)))
        args.append(out_norm_g.reshape(1, D_MODEL))
        body = _ffn_norm_kernel
    return pl.pallas_call(
        body,
        grid=(t // tm, D_FF // tf),
        in_specs=in_specs,
        out_specs=pl.BlockSpec((tm, D_MODEL), lambda i, k: (i, 0)),
        out_shape=jax.ShapeDtypeStruct((t, D_MODEL), F32),
        compiler_params=_cparams("parallel", "arbitrary"),
        name="swiglu_ffn",
    )(*args)


def _layer(x, b, s, p, out_norm_g):
    pa, pb, pc = _norm_proj(x, p["g1"], p["w_in"])
    pa = pa.reshape(b, s, PA_W)
    branches = [_dilated_branch(pa, bias, dil) for bias, (_, dil) in zip(p["bias_a"], A_BRANCHES)]
    r = _retention(pb.reshape(b, s, PB_W), p["cos"], p["sin"], p["lg_f"], p["lg_b"])
    c = _window_attn(pc.reshape(b, s, PC_W), p["bias_c"], p["sink"])
    x1, h2 = _merge_out(branches, r, c, x, p["w_out"], p["g2"])
    return _ffn(h2, x1, p["w_gate"], p["w_up"], p["w_down"], out_norm_g)


def kernel(x_prompt, x_sample, rel_bias, norm1_g, w_in, ret_decay_fwd, ret_decay_bwd, attn_sink, w_out, norm2_g,
           w_gate, w_up, w_down, final_norm_g):
    bias_a = [_pair_heads(_band_bias(rel_bias[:, :A_HEADS], A_TQ, A_TK, A_RADIUS, A_RADIUS, dil))
              for _, dil in A_BRANCHES]
    q_order = np.asarray(C_Q_ORDER)
    bias_c = _pair_heads(_band_bias(rel_bias[:, A_HEADS:][:, q_order], C_TQ, C_TK, C_TQ, C_RADIUS, 1))

    def reorder_c_heads(w, axis, start):
        take = lambda a, n: lax.slice_in_dim(w, a, a + n, axis=axis)
        heads = [take(start + h * HEAD_DIM, HEAD_DIM) for h in C_Q_ORDER]
        tail = start + C_W
        return jnp.concatenate([take(0, start)] + heads + [take(tail, w.shape[axis] - tail)], axis=axis)

    layers = []
    for i in range(DEPTH):
        layers.append(dict(
            g1=norm1_g[i], g2=norm2_g[i], sink=attn_sink[i].astype(F32)[q_order] * LOG2E, bias_a=bias_a, bias_c=bias_c,
            w_in=reorder_c_heads(w_in[i].astype(BF16), 1, PA_W + PB_W),
            w_out=reorder_c_heads(w_out[i].astype(BF16), 0, A_W + R_W),
            w_gate=w_gate[i].astype(BF16), w_up=w_up[i].astype(BF16), w_down=w_down[i].astype(BF16),
            lg_f=jnp.log1p(-jnp.exp2(-ret_decay_fwd[i].astype(F32))),
            lg_b=jnp.log1p(-jnp.exp2(-ret_decay_bwd[i].astype(F32))),
        ))

    def trunk(x):
        b, s, _ = x.shape
        cos, sin = _rope_tables(s)
        x = x.reshape(b * s, D_MODEL)
        for i, p in enumerate(layers):
            x = _layer(x, b, s, dict(p, cos=cos, sin=sin), final_norm_g if i == DEPTH - 1 else None)
        return x.reshape(b, s, D_MODEL)

    return trunk(x_prompt), trunk(x_sample)
```

```python
import functools
import math

import numpy as np
import jax
import jax.numpy as jnp
from jax import lax
from jax.experimental import pallas as pl
from jax.experimental.pallas import tpu as pltpu

D_MODEL = 2048
DEPTH = 2
HEAD_DIM = 64
A_HEADS = 12
A_BRANCHES = ((128, 1), (512, 4), (2048, 16))
A_RADIUS = 64
R_HEADS = 8
R_QK_DIM = 32
R_V_DIM = 64
R_CHUNK = 128
ROPE_BASE = 10000.0
C_HEADS = 12
C_KV_HEADS = 4
C_GROUP = C_HEADS // C_KV_HEADS
C_RADIUS = 128
REL_BUCKETS = 32
REL_MAX_DIST = 1024
D_FF = 5632
EPS = 1e-6
GN_EPS = 1e-5
NEG = -1e30

A_W = A_HEADS * HEAD_DIM
R_QK_W = R_HEADS * R_QK_DIM
R_W = R_HEADS * R_V_DIM
C_W = C_HEADS * HEAD_DIM
C_KV_W = C_KV_HEADS * HEAD_DIM
PA_W = 3 * A_W
PB_W = 2 * R_QK_W + 2 * R_W
PC_W = C_W + 2 * C_KV_W
IN_COLS = PA_W + PB_W + PC_W
MIX_W = A_W + R_W + C_W

LANES = 128
V7X_VMEM_BYTES = 64 * 1024 * 1024
VMEM_LIMIT = V7X_VMEM_BYTES - 8 * 1024 * 1024

A_TQ = 128
A_TK = A_TQ + 2 * A_RADIUS
A_SUB = 2
C_TQ = 128
C_TK = 3 * C_TQ
C_Q_ORDER = tuple(kp * C_GROUP + j + e * C_GROUP
                  for kp in range(0, C_KV_HEADS, 2) for j in range(C_GROUP) for e in range(2))
R_STEP_CHUNKS = 4

LOG2E = 1.4426950408889634
QSCALE = HEAD_DIM ** -0.5 * LOG2E

BF16 = jnp.bfloat16
F32 = jnp.float32


def _cparams(*sem):
    return pltpu.CompilerParams(dimension_semantics=sem, vmem_limit_bytes=VMEM_LIMIT)


def _t5_bucket(rel):
    nb = REL_BUCKETS // 2
    max_exact = nb // 2
    ret = np.where(rel > 0, nb, 0)
    n = np.abs(rel)
    nf = np.maximum(n, 1).astype(np.float32)
    large = max_exact + (np.log(nf / max_exact) / math.log(REL_MAX_DIST / max_exact) * (nb - max_exact)).astype(np.int32)
    large = np.minimum(large, nb - 1)
    return (ret + np.where(n < max_exact, n, large)).astype(np.int32)


def _band_bias(table, tq, tk, lead, radius, dil):
    h = table.shape[1]
    n = tq + tk - 1
    offs = np.arange(n) - (tq - 1) - lead
    per_off = jnp.take(table.astype(F32), jnp.asarray(_t5_bucket(offs * dil)), axis=0).T * LOG2E
    padded = jnp.concatenate([per_off, jnp.zeros((h, 1), F32)], axis=1)
    toep = jnp.tile(padded, (1, tq))[:, :tq * n].reshape(h, tq, n)[:, :, tq - 1:tq - 1 + tk]
    off = np.arange(tk)[None, :] - lead - np.arange(tq)[:, None]
    band = np.abs(off) <= radius
    col = np.arange(tk)[None, :]
    first = band & (col >= lead)
    last = band & (col < tk - lead)
    masks = np.stack([band, first, last, first & last])
    return jnp.where(jnp.asarray(masks)[:, None], toep[None], NEG).swapaxes(-1, -2)


def _pair_heads(bias):
    v, h, tk, tq = bias.shape
    return bias.reshape(v, h // 2, 2, tk, tq).swapaxes(2, 3).reshape(v, h // 2, tk, 2 * tq)


def _edge_variant(i, n):
    return (i == 0).astype(jnp.int32) + 2 * (i == n - 1).astype(jnp.int32)


_PROJ_CHUNKS = (
    (0, 0, 0, 768, QSCALE), (0, 768, 768, 768, None), (0, 1536, 1536, 768, None),
    (1, 0, 2304, 512, None), (1, 512, 2816, 512, None), (1, 1024, 3328, 512, None),
    (2, 0, 3840, 768, QSCALE), (2, 768, 4608, 512, None),
)


A_DILATIONS = tuple(d for _, d in A_BRANCHES)


def _norm_proj_kernel(x_ref, g_ref, w_ref, *refs):
    n_a = len(A_DILATIONS)
    pa_refs, (pb_ref, pc_ref), stage_refs = refs[:n_a], refs[n_a:n_a + 2], refs[n_a + 2:]
    tm = x_ref.shape[0]
    x = x_ref[...]
    h = (x * lax.rsqrt(jnp.mean(x * x, axis=-1, keepdims=True) + EPS) * g_ref[...]).astype(BF16)
    outs = (None, pb_ref, pc_ref)
    a_chunk = 0
    for oi, oc, wc, width, scale in _PROJ_CHUNKS:
        y = jnp.dot(h, w_ref[:, wc:wc + width], preferred_element_type=F32)
        if scale is not None:
            y = y * scale
        if oi != 0:
            outs[oi][:, oc:oc + width] = y.astype(BF16)
            continue
        stage = stage_refs[a_chunk]
        a_chunk += 1
        for c in range(width // LANES):
            stage[c] = y[:, c * LANES:(c + 1) * LANES]
        for pa_ref, d in zip(pa_refs, A_DILATIONS):
            if d == 1:
                pa_ref[:, oc:oc + width] = y.astype(BF16)
                continue
            for r in range(d):
                for c in range(width // LANES):
                    col = r * PA_W + oc + c * LANES
                    pa_ref[:, col:col + LANES] = stage[c, pl.ds(r, tm // d, stride=d), :].astype(BF16)


def _norm_proj(x, g, w, tm=512):
    t = x.shape[0]
    assert t % tm == 0 and all(tm % (16 * d) == 0 for d in A_DILATIONS)
    n_a_chunks = sum(1 for c in _PROJ_CHUNKS if c[0] == 0)
    return pl.pallas_call(
        _norm_proj_kernel,
        grid=(t // tm,),
        in_specs=[
            pl.BlockSpec((tm, D_MODEL), lambda i: (i, 0)),
            pl.BlockSpec((1, D_MODEL), lambda i: (0, 0)),
            pl.BlockSpec((D_MODEL, IN_COLS), lambda i: (0, 0), pipeline_mode=pl.Buffered(1)),
        ],
        out_specs=[pl.BlockSpec((tm // d, d * PA_W), lambda i: (i, 0)) for d in A_DILATIONS] + [
            pl.BlockSpec((tm, PB_W), lambda i: (i, 0)),
            pl.BlockSpec((tm, PC_W), lambda i: (i, 0)),
        ],
        out_shape=[jax.ShapeDtypeStruct((t // d, d * PA_W), BF16) for d in A_DILATIONS] + [
            jax.ShapeDtypeStruct((t, PB_W), BF16),
            jax.ShapeDtypeStruct((t, PC_W), BF16),
        ],
        scratch_shapes=[pltpu.VMEM((A_W // LANES, tm, LANES), F32) for _ in range(n_a_chunks)],
        compiler_params=_cparams("parallel"),
        name="norm_in_proj",
    )(x, g.reshape(1, D_MODEL), w)


def _nt_dot(a, b):
    return lax.dot_general(a, b, (((1,), (1,)), ((), ())), preferred_element_type=F32)


def _masked_head_pair(qpair):
    half = lax.broadcasted_iota(jnp.int32, qpair.shape, 1) // HEAD_DIM
    zero = jnp.zeros_like(qpair)
    return jnp.concatenate([jnp.where(half == e, qpair, zero) for e in range(2)], axis=0)


def _dilated_attn_kernel(q_ref, kp_ref, kc_ref, kn_ref, vp_ref, vc_ref, vn_ref, bias_lo_ref, bias_hi_ref,
                         o_ref, lse_ref):
    kw = jnp.concatenate([kp_ref[0], kc_ref[0], kn_ref[0]], axis=0)
    vt = jnp.concatenate([vp_ref[0], vc_ref[0], vn_ref[0]], axis=0).T
    for sub, bias_ref in enumerate((bias_lo_ref, bias_hi_ref)):
        rows = slice(sub * A_TQ, (sub + 1) * A_TQ)
        keys = slice(sub * A_TQ, sub * A_TQ + A_TK)
        lses = []
        for j in range(A_HEADS // 2):
            lanes = slice(j * LANES, (j + 1) * LANES)
            s = _nt_dot(kw[keys, lanes], _masked_head_pair(q_ref[0, rows, lanes])) + bias_ref[j]
            m = jnp.max(s, axis=0, keepdims=True)
            p = jnp.exp2(s - m)
            l = jnp.sum(p, axis=0, keepdims=True)
            pb = p.astype(BF16)
            o_t = []
            for e in range(2):
                h = 2 * j + e
                cols = slice(e * A_TQ, (e + 1) * A_TQ)
                o = jnp.dot(vt[h * HEAD_DIM:(h + 1) * HEAD_DIM, keys], pb[:, cols], preferred_element_type=F32)
                o_t.append(o / l[:, cols])
                lses.append(m[:, cols] + jnp.log2(l[:, cols]))
            o_ref[0, rows, lanes] = jnp.concatenate(o_t, axis=0).T.astype(BF16)
        lse_t = jnp.concatenate(lses + [jnp.zeros((LANES - A_HEADS, A_TQ), F32)], axis=0)
        lse_ref[0, rows, :] = lse_t.T


def _dilated_branch(pav, bias, dil):
    b, l, _ = pav.shape
    step = A_SUB * A_TQ
    assert A_SUB == 2 and l % step == 0
    nt = l // step
    halo = A_RADIUS
    hb = step // halo
    nblk = PA_W // A_W

    def cur(c):
        return lambda bi, r, i: (bi, i, r * nblk + c)

    def prev(c):
        return lambda bi, r, i: (bi, jnp.maximum(i * hb - 1, 0), r * nblk + c)

    def nxt(c):
        return lambda bi, r, i: (bi, jnp.minimum((i + 1) * hb, nt * hb - 1), r * nblk + c)

    tile = lambda c: pl.BlockSpec((1, step, A_W), cur(c))
    bias_spec = lambda variant: pl.BlockSpec((None, A_HEADS // 2, A_TK, 2 * A_TQ),
                                             lambda bi, r, i: (variant(i), 0, 0, 0))
    o, lse = pl.pallas_call(
        _dilated_attn_kernel,
        grid=(b, dil, nt),
        in_specs=[
            tile(0),
            pl.BlockSpec((1, halo, A_W), prev(1)), tile(1), pl.BlockSpec((1, halo, A_W), nxt(1)),
            pl.BlockSpec((1, halo, A_W), prev(2)), tile(2), pl.BlockSpec((1, halo, A_W), nxt(2)),
            bias_spec(lambda i: (i == 0).astype(jnp.int32)),
            bias_spec(lambda i: 2 * (i == nt - 1).astype(jnp.int32)),
        ],
        out_specs=[
            pl.BlockSpec((1, step, A_W), lambda bi, r, i: (bi, i, r)),
            pl.BlockSpec((1, step, LANES), lambda bi, r, i: (bi, i, r)),
        ],
        out_shape=[
            jax.ShapeDtypeStruct((b, l, dil * A_W), BF16),
            jax.ShapeDtypeStruct((b, l, dil * LANES), F32),
        ],
        compiler_params=_cparams("parallel", "parallel", "arbitrary"),
        name=f"dilated_attn_d{dil}",
    )(pav, pav, pav, pav, pav, pav, pav, bias, bias)
    return o.reshape(b * l, dil * A_W), lse.reshape(b * l, dil * LANES)


def _window_attn_kernel(sink_ref, q_ref, kp_ref, kc_ref, kn_ref, vp_ref, vc_ref, vn_ref, bias_ref, o_ref):
    kw = jnp.concatenate([kp_ref[0], kc_ref[0], kn_ref[0]], axis=0)
    vt = jnp.concatenate([vp_ref[0], vc_ref[0], vn_ref[0]], axis=0).T
    for pair in range(C_HEADS // 2):
        lanes = slice(pair * LANES, (pair + 1) * LANES)
        kv_pair = pair // C_GROUP
        s = _nt_dot(kw[:, kv_pair * LANES:(kv_pair + 1) * LANES], _masked_head_pair(q_ref[0, :, lanes]))
        s = s + bias_ref[pair]
        sk = jnp.concatenate([jnp.full((1, C_TQ), sink_ref[2 * pair + e], F32) for e in range(2)], axis=1)
        m = jnp.maximum(jnp.max(s, axis=0, keepdims=True), sk)
        p = jnp.exp2(s - m)
        den = jnp.sum(p, axis=0, keepdims=True) + jnp.exp2(sk - m)
        pb = p.astype(BF16)
        o_t = []
        for e in range(2):
            kv = 2 * kv_pair + e
            cols = slice(e * C_TQ, (e + 1) * C_TQ)
            o = jnp.dot(vt[kv * HEAD_DIM:(kv + 1) * HEAD_DIM, :], pb[:, cols], preferred_element_type=F32)
            o_t.append(o / den[:, cols])
        o_ref[0, :, lanes] = jnp.concatenate(o_t, axis=0).T.astype(BF16)


def _window_attn(pc, bias, sink):
    b, s, _ = pc.shape
    assert s % C_TQ == 0
    nt = s // C_TQ
    kblk = C_W // C_KV_W

    def cur(c):
        return lambda bi, i: (bi, i, c)

    def prev(c):
        return lambda bi, i: (bi, jnp.maximum(i - 1, 0), c)

    def nxt(c):
        return lambda bi, i: (bi, jnp.minimum(i + 1, nt - 1), c)

    kv_spec = lambda m: pl.BlockSpec((1, C_TQ, C_KV_W), m)
    o = pl.pallas_call(
        _window_attn_kernel,
        grid=(b, nt),
        in_specs=[
            pl.BlockSpec(memory_space=pltpu.SMEM),
            pl.BlockSpec((1, C_TQ, C_W), cur(0)),
            kv_spec(prev(kblk)), kv_spec(cur(kblk)), kv_spec(nxt(kblk)),
            kv_spec(prev(kblk + 1)), kv_spec(cur(kblk + 1)), kv_spec(nxt(kblk + 1)),
            pl.BlockSpec((None, C_HEADS // 2, C_TK, 2 * C_TQ), lambda bi, i: (_edge_variant(i, nt), 0, 0, 0)),
        ],
        out_specs=pl.BlockSpec((1, C_TQ, C_W), lambda bi, i: (bi, i, 0)),
        out_shape=jax.ShapeDtypeStruct((b, s, C_W), BF16),
        compiler_params=_cparams("parallel", "arbitrary"),
        name="window_gqa",
    )(sink.astype(F32), pc, pc, pc, pc, pc, pc, pc, bias)
    return o.reshape(b * s, C_W)


def _rope(x, cos, sin, lane_lo):
    parts = []
    for a in range(0, R_QK_W, LANES):
        xh = x[:, a:a + LANES]
        half = R_QK_DIM // 2
        parts.append(jnp.where(lane_lo, pltpu.roll(xh, LANES - half, 1), pltpu.roll(xh, half, 1)))
    return x * cos + jnp.concatenate(parts, axis=1) * sin


def _retention_chunk(q, k, v, cos, sin, decay, qdec, kdec, rdec, mask, state_ref):
    c = R_CHUNK
    lane = lax.broadcasted_iota(jnp.int32, (c, LANES), 1)
    lane_lo = (lane % R_QK_DIM) < (R_QK_DIM // 2)
    q = _rope(q.astype(F32), cos, sin, lane_lo)
    k = _rope(k.astype(F32), cos, sin, lane_lo) * (R_QK_DIM ** -0.5)
    state = state_ref[...]
    cross = jnp.dot((q * qdec).astype(BF16), state.astype(BF16), preferred_element_type=F32)

    qb = q.astype(BF16)
    kb = k.astype(BF16)
    head = lax.broadcasted_iota(jnp.int32, (c, R_QK_W), 1) // R_QK_DIM
    qs = jnp.concatenate([jnp.where(head == h, qb, jnp.zeros_like(qb)) for h in range(R_HEADS)], axis=0)
    s = lax.dot_general(qs, kb, (((1,), (1,)), ((), ())), preferred_element_type=F32)
    p = (s * decay).astype(BF16)

    vlane = lax.broadcasted_iota(jnp.int32, (c, R_W), 1) % LANES
    v_lo = jnp.where(vlane < R_V_DIM, v, jnp.zeros_like(v))
    v_hi = jnp.where(vlane >= R_V_DIM, v, jnp.zeros_like(v))
    pieces = []
    for j in range(R_HEADS // 2):
        cs = slice(j * LANES, (j + 1) * LANES)
        a = jnp.dot(p[(2 * j) * c:(2 * j + 1) * c], v_lo[:, cs], preferred_element_type=F32)
        a = a + jnp.dot(p[(2 * j + 1) * c:(2 * j + 2) * c], v_hi[:, cs], preferred_element_type=F32)
        pieces.append(a)
    intra = jnp.concatenate(pieces, axis=1)

    kv = lax.dot_general((k * kdec).astype(BF16), v, (((0,), (0,)), ((), ())), preferred_element_type=F32)
    state_ref[...] = state * rdec + kv * mask
    return cross + intra


def _retention_fwd_kernel(q_ref, k_ref, v_ref, cos_ref, sin_ref, decay_ref, qdec_ref, kdec_ref, rdec_ref,
                          mask_ref, o_ref, state_ref, *, n_chunks):
    @pl.when(pl.program_id(1) == 0)
    def _():
        state_ref[...] = jnp.zeros_like(state_ref)

    for ci in range(n_chunks):
        rows = slice(ci * R_CHUNK, (ci + 1) * R_CHUNK)
        o_ref[0, rows, :] = _retention_chunk(
            q_ref[0, rows, :], k_ref[0, rows, :], v_ref[0, rows, :], cos_ref[rows, :], sin_ref[rows, :],
            decay_ref[...], qdec_ref[...], kdec_ref[...], rdec_ref[...], mask_ref[...], state_ref)


def _split_dot(x, w):
    hi = x.astype(BF16)
    lo = (x - hi.astype(F32)).astype(BF16)
    return jnp.dot(hi, w, preferred_element_type=F32) + jnp.dot(lo, w, preferred_element_type=F32)


def _retention_bwd_kernel(q_ref, k_ref, v_ref, cos_ref, sin_ref, decay_ref, qdec_ref, kdec_ref, rdec_ref,
                          mask_ref, fwd_ref, gate_ref, avg_ref, o_ref, state_ref, *, n_chunks):
    @pl.when(pl.program_id(1) == 0)
    def _():
        state_ref[...] = jnp.zeros_like(state_ref)

    for ci in reversed(range(n_chunks)):
        rows = slice(ci * R_CHUNK, (ci + 1) * R_CHUNK)
        o = fwd_ref[0, rows, :] + _retention_chunk(
            q_ref[0, rows, :], k_ref[0, rows, :], v_ref[0, rows, :], cos_ref[rows, :], sin_ref[rows, :],
            decay_ref[...], qdec_ref[...], kdec_ref[...], rdec_ref[...], mask_ref[...], state_ref)
        avg = avg_ref[...]
        d = o - _split_dot(o, avg)
        y = d * lax.rsqrt(_split_dot(d * d, avg) + GN_EPS)
        g = gate_ref[0, rows, :].astype(F32)
        o_ref[0, rows, :] = (y * (g / (1.0 + jnp.exp(-g)))).astype(BF16)


def _retention_tables(lg, reverse):
    c = R_CHUNK
    t = np.arange(c)
    diff = (t[None, :] - t[:, None]) if reverse else (t[:, None] - t[None, :])
    keep = (diff > 0) if reverse else (diff >= 0)
    decay = jnp.where(jnp.asarray(keep)[None], jnp.exp(lg[:, None, None] * np.maximum(diff, 0).astype(np.float32)), 0.0)
    tf = t.astype(np.float32)
    qpow = (c - tf) if reverse else (tf + 1.0)
    kpow = tf if reverse else (c - 1.0 - tf)
    qdec = jnp.repeat(jnp.exp(lg[None, :] * qpow[:, None]), R_QK_DIM, axis=1)
    kdec = jnp.repeat(jnp.exp(lg[None, :] * kpow[:, None]), R_QK_DIM, axis=1)
    rdec = jnp.broadcast_to(jnp.repeat(jnp.exp(lg * c), R_QK_DIM)[:, None], (R_QK_W, R_W))
    return decay.reshape(R_HEADS * c, c), qdec, kdec, rdec


def _retention(pb, cos, sin, lg_f, lg_b):
    b, s, _ = pb.shape
    rows = R_CHUNK * R_STEP_CHUNKS
    assert s % rows == 0
    nt = s // rows
    hd = np.arange(R_QK_W)[:, None] // R_QK_DIM == np.arange(R_W)[None, :] // R_V_DIM
    mask = jnp.asarray(hd.astype(np.float32))
    gh = np.arange(R_W)[:, None] // R_V_DIM == np.arange(R_W)[None, :] // R_V_DIM
    avg = jnp.asarray(gh.astype(np.float32) / R_V_DIM, dtype=BF16)
    const = lambda shape: pl.BlockSpec(shape, lambda bi, i: (0,) * len(shape))

    def specs(order):
        seq = lambda w, cb: pl.BlockSpec((1, rows, w), lambda bi, i: (bi, order(i), cb))
        tab = pl.BlockSpec((rows, R_QK_W), lambda bi, i: (order(i), 0))
        return seq, [seq(R_QK_W, 0), seq(R_QK_W, 1), seq(R_W, 1), tab, tab,
                     const((R_HEADS * R_CHUNK, R_CHUNK)), const((R_CHUNK, R_QK_W)), const((R_CHUNK, R_QK_W)),
                     const((R_QK_W, R_W)), const((R_QK_W, R_W))]

    seq, in_specs = specs(lambda i: i)
    o_fwd = pl.pallas_call(
        functools.partial(_retention_fwd_kernel, n_chunks=R_STEP_CHUNKS),
        grid=(b, nt),
        in_specs=in_specs,
        out_specs=seq(R_W, 0),
        out_shape=jax.ShapeDtypeStruct((b, s, R_W), F32),
        scratch_shapes=[pltpu.VMEM((R_QK_W, R_W), F32)],
        compiler_params=_cparams("parallel", "arbitrary"),
        name="retention_fwd",
    )(pb, pb, pb, cos, sin, *_retention_tables(lg_f, False), mask)

    seq, in_specs = specs(lambda i: nt - 1 - i)
    out = pl.pallas_call(
        functools.partial(_retention_bwd_kernel, n_chunks=R_STEP_CHUNKS),
        grid=(b, nt),
        in_specs=in_specs + [seq(R_W, 0), seq(R_W, 2), const((R_W, R_W))],
        out_specs=seq(R_W, 0),
        out_shape=jax.ShapeDtypeStruct((b, s, R_W), BF16),
        scratch_shapes=[pltpu.VMEM((R_QK_W, R_W), F32)],
        compiler_params=_cparams("parallel", "arbitrary"),
        name="retention_bwd",
    )(pb, pb, pb, cos, sin, *_retention_tables(lg_b, True), mask, o_fwd, pb, avg)
    return out.reshape(b * s, R_W)


def _rope_tables(s):
    half = R_QK_DIM // 2
    freqs = ROPE_BASE ** (-jnp.arange(half, dtype=F32) / half)
    ang = jnp.arange(s, dtype=F32)[:, None] * freqs[None]
    cos, sin = jnp.cos(ang), jnp.sin(ang)
    return (jnp.tile(jnp.concatenate([cos, cos], axis=1), (1, R_HEADS)),
            jnp.tile(jnp.concatenate([-sin, sin], axis=1), (1, R_HEADS)))


def _natural_order(o_ref, lse_ref, o_stage, lse_stage, d):
    if d == 1:
        return o_ref[...].astype(F32), lse_ref[...]
    n = o_ref.shape[0]
    for r in range(d):
        lse_stage[pl.ds(r, n, stride=d), :] = lse_ref[:, r * LANES:(r + 1) * LANES]
        for c in range(A_W // LANES):
            col = r * A_W + c * LANES
            o_stage[c, pl.ds(r, n, stride=d), :] = o_ref[:, col:col + LANES].astype(F32)
    return jnp.concatenate([o_stage[c] for c in range(A_W // LANES)], axis=1), lse_stage[...]


def _merge_out_kernel(*refs):
    n_a = len(A_DILATIONS)
    o_refs, lse_refs = refs[:n_a], refs[n_a:2 * n_a]
    r_ref, c_ref, x_ref, w_ref, g_ref, expand_ref, x1_ref, h2_ref = refs[2 * n_a:2 * n_a + 8]
    stages = refs[2 * n_a + 8:]
    outs, lses = [], []
    for b, d in enumerate(A_DILATIONS):
        o, lse = _natural_order(o_refs[b], lse_refs[b], stages[2 * b], stages[2 * b + 1], d)
        outs.append(o)
        lses.append(lse)
    top = functools.reduce(jnp.maximum, lses)
    ws = [jnp.exp2(l - top) for l in lses]
    den = functools.reduce(lambda u, v: u + v, ws)
    expand = expand_ref[...]
    a = None
    for wgt, o in zip(ws, outs):
        term = _split_dot(wgt / den, expand) * o
        a = term if a is None else a + term
    a = a.astype(BF16)
    r = r_ref[...]
    c = c_ref[...]
    nc = 512
    for n0 in range(0, D_MODEL, nc):
        cols = slice(n0, n0 + nc)
        y = jnp.dot(a, w_ref[0:A_W, cols], preferred_element_type=F32)
        y = y + jnp.dot(r, w_ref[A_W:A_W + R_W, cols], preferred_element_type=F32)
        y = y + jnp.dot(c, w_ref[A_W + R_W:MIX_W, cols], preferred_element_type=F32)
        x1_ref[:, cols] = x_ref[:, cols] + y
    x1 = x1_ref[...]
    h2 = x1 * lax.rsqrt(jnp.mean(x1 * x1, axis=-1, keepdims=True) + EPS) * g_ref[...]
    h2_ref[...] = h2.astype(BF16)


def _merge_out(branches, r, c, x, w, g, tm=512):
    t = x.shape[0]
    assert t % tm == 0 and all(tm % (16 * d) == 0 for d in A_DILATIONS)
    expand = jnp.asarray((np.arange(LANES)[:, None] == np.arange(A_W)[None, :] // HEAD_DIM).astype(np.float32), dtype=BF16)
    row = lambda w_: pl.BlockSpec((tm, w_), lambda i: (i, 0))
    dil_row = lambda w_, d: pl.BlockSpec((tm // d, d * w_), lambda i: (i, 0))
    stages = []
    for _ in A_DILATIONS:
        stages += [pltpu.VMEM((A_W // LANES, tm, LANES), F32), pltpu.VMEM((tm, LANES), F32)]
    return pl.pallas_call(
        _merge_out_kernel,
        grid=(t // tm,),
        in_specs=[dil_row(A_W, d) for d in A_DILATIONS] + [dil_row(LANES, d) for d in A_DILATIONS] + [
            row(R_W), row(C_W), row(D_MODEL),
            pl.BlockSpec((MIX_W, D_MODEL), lambda i: (0, 0), pipeline_mode=pl.Buffered(1)),
            pl.BlockSpec((1, D_MODEL), lambda i: (0, 0)),
            pl.BlockSpec((LANES, A_W), lambda i: (0, 0))],
        out_specs=[row(D_MODEL), row(D_MODEL)],
        out_shape=[jax.ShapeDtypeStruct((t, D_MODEL), F32), jax.ShapeDtypeStruct((t, D_MODEL), BF16)],
        scratch_shapes=stages,
        compiler_params=_cparams("parallel"),
        name="merge_out_proj",
    )(*[o for o, _ in branches], *[l for _, l in branches], r, c, x, w, g.reshape(1, D_MODEL), expand)


def _ffn_accumulate(h_ref, x_ref, wg_ref, wu_ref, wd_ref, o_ref):
    @pl.when(pl.program_id(1) == 0)
    def _():
        o_ref[...] = x_ref[...]

    h = h_ref[...]
    tf = wg_ref.shape[1]
    halves = [slice(c, c + tf // 2) for c in (0, tf // 2)]
    gs = [jnp.dot(h, wg_ref[:, c], preferred_element_type=F32) for c in halves]
    us = [jnp.dot(h, wu_ref[:, c], preferred_element_type=F32) for c in halves]
    y = None
    for g, u, c in zip(gs, us, halves):
        a = (g / (1.0 + jnp.exp(-g)) * u).astype(BF16)
        part = jnp.dot(a, wd_ref[c, :], preferred_element_type=F32)
        y = part if y is None else y + part
    o_ref[...] += y


def _ffn_kernel(h_ref, x_ref, wg_ref, wu_ref, wd_ref, o_ref):
    _ffn_accumulate(h_ref, x_ref, wg_ref, wu_ref, wd_ref, o_ref)


def _ffn_norm_kernel(h_ref, x_ref, wg_ref, wu_ref, wd_ref, g_ref, o_ref):
    _ffn_accumulate(h_ref, x_ref, wg_ref, wu_ref, wd_ref, o_ref)

    @pl.when(pl.program_id(1) == pl.num_programs(1) - 1)
    def _():
        x = o_ref[...]
        o_ref[...] = x * lax.rsqrt(jnp.mean(x * x, axis=-1, keepdims=True) + EPS) * g_ref[...]


def _ffn(h, x, wg, wu, wd, out_norm_g=None, tm=512, tf=512):
    t = x.shape[0]
    assert t % tm == 0 and D_FF % tf == 0
    in_specs = [
        pl.BlockSpec((tm, D_MODEL), lambda i, k: (i, 0)),
        pl.BlockSpec((tm, D_MODEL), lambda i, k: (i, 0)),
        pl.BlockSpec((D_MODEL, tf), lambda i, k: (0, k)),
        pl.BlockSpec((D_MODEL, tf), lambda i, k: (0, k)),
        pl.BlockSpec((tf, D_MODEL), lambda i, k: (k, 0)),
    ]
    args = [h, x, wg, wu, wd]
    body = _ffn_kernel
    if out_norm_g is not None:
        in_specs.append(pl.BlockSpec((1, D_MODEL), lambda i, k: (0, 0)))
        args.append(out_norm_g.reshape(1, D_MODEL))
        body = _ffn_norm_kernel
    return pl.pallas_call(
        body,
        grid=(t // tm, D_FF // tf),
        in_specs=in_specs,
        out_specs=pl.BlockSpec((tm, D_MODEL), lambda i, k: (i, 0)),
        out_shape=jax.ShapeDtypeStruct((t, D_MODEL), F32),
        compiler_params=_cparams("parallel", "arbitrary"),
        name="swiglu_ffn",
    )(*args)


def _layer(x, b, s, p, out_norm_g):
    *pas, pb, pc = _norm_proj(x, p["g1"], p["w_in"])
    branches = [_dilated_branch(pa.reshape(b, s // d, d * PA_W), bias, d)
                for pa, bias, d in zip(pas, p["bias_a"], A_DILATIONS)]
    r = _retention(pb.reshape(b, s, PB_W), p["cos"], p["sin"], p["lg_f"], p["lg_b"])
    c = _window_attn(pc.reshape(b, s, PC_W), p["bias_c"], p["sink"])
    x1, h2 = _merge_out(branches, r, c, x, p["w_out"], p["g2"])
    return _ffn(h2, x1, p["w_gate"], p["w_up"], p["w_down"], out_norm_g)


def kernel(x_prompt, x_sample, rel_bias, norm1_g, w_in, ret_decay_fwd, ret_decay_bwd, attn_sink, w_out, norm2_g,
           w_gate, w_up, w_down, final_norm_g):
    bias_a = [_pair_heads(_band_bias(rel_bias[:, :A_HEADS], A_TQ, A_TK, A_RADIUS, A_RADIUS, dil))
              for _, dil in A_BRANCHES]
    q_order = np.asarray(C_Q_ORDER)
    bias_c = _pair_heads(_band_bias(rel_bias[:, A_HEADS:][:, q_order], C_TQ, C_TK, C_TQ, C_RADIUS, 1))

    def reorder_c_heads(w, axis, start):
        take = lambda a, n: lax.slice_in_dim(w, a, a + n, axis=axis)
        heads = [take(start + h * HEAD_DIM, HEAD_DIM) for h in C_Q_ORDER]
        tail = start + C_W
        return jnp.concatenate([take(0, start)] + heads + [take(tail, w.shape[axis] - tail)], axis=axis)

    layers = []
    for i in range(DEPTH):
        layers.append(dict(
            g1=norm1_g[i], g2=norm2_g[i], sink=attn_sink[i].astype(F32)[q_order] * LOG2E, bias_a=bias_a, bias_c=bias_c,
            w_in=reorder_c_heads(w_in[i].astype(BF16), 1, PA_W + PB_W),
            w_out=reorder_c_heads(w_out[i].astype(BF16), 0, A_W + R_W),
            w_gate=w_gate[i].astype(BF16), w_up=w_up[i].astype(BF16), w_down=w_down[i].astype(BF16),
            lg_f=jnp.log1p(-jnp.exp2(-ret_decay_fwd[i].astype(F32))),
            lg_b=jnp.log1p(-jnp.exp2(-ret_decay_bwd[i].astype(F32))),
        ))

    def trunk(x):
        b, s, _ = x.shape
        cos, sin = _rope_tables(s)
        x = x.reshape(b * s, D_MODEL)
        for i, p in enumerate(layers):
            x = _layer(x, b, s, dict(p, cos=cos, sin=sin), final_norm_g if i == DEPTH - 1 else None)
        return x.reshape(b, s, D_MODEL)

    return trunk(x_prompt), trunk(x_sample)
```

```python
import functools
import math

import numpy as np
import jax
import jax.numpy as jnp
from jax import lax
from jax.experimental import pallas as pl
from jax.experimental.pallas import tpu as pltpu

D_MODEL = 2048
DEPTH = 2
HEAD_DIM = 64
A_HEADS = 12
A_BRANCHES = ((128, 1), (512, 4), (2048, 16))
A_RADIUS = 64
R_HEADS = 8
R_QK_DIM = 32
R_V_DIM = 64
R_CHUNK = 128
ROPE_BASE = 10000.0
C_HEADS = 12
C_KV_HEADS = 4
C_GROUP = C_HEADS // C_KV_HEADS
C_RADIUS = 128
REL_BUCKETS = 32
REL_MAX_DIST = 1024
D_FF = 5632
EPS = 1e-6
GN_EPS = 1e-5
NEG = -1e30

A_W = A_HEADS * HEAD_DIM
R_QK_W = R_HEADS * R_QK_DIM
R_W = R_HEADS * R_V_DIM
C_W = C_HEADS * HEAD_DIM
C_KV_W = C_KV_HEADS * HEAD_DIM
PA_W = 3 * A_W
PB_W = 2 * R_QK_W + 2 * R_W
PC_W = C_W + 2 * C_KV_W
IN_COLS = PA_W + PB_W + PC_W
MIX_W = A_W + R_W + C_W

LANES = 128
V7X_VMEM_BYTES = 64 * 1024 * 1024
VMEM_LIMIT = V7X_VMEM_BYTES - 8 * 1024 * 1024

A_TQ = 128
A_TK = A_TQ + 2 * A_RADIUS
A_SUB = 2
C_TQ = 128
C_TK = 3 * C_TQ
C_Q_ORDER = tuple(kp * C_GROUP + j + e * C_GROUP
                  for kp in range(0, C_KV_HEADS, 2) for j in range(C_GROUP) for e in range(2))
R_STEP_CHUNKS = 4

LOG2E = 1.4426950408889634
QSCALE = HEAD_DIM ** -0.5 * LOG2E

BF16 = jnp.bfloat16
F32 = jnp.float32


def _cparams(*sem):
    return pltpu.CompilerParams(dimension_semantics=sem, vmem_limit_bytes=VMEM_LIMIT)


def _t5_bucket(rel):
    nb = REL_BUCKETS // 2
    max_exact = nb // 2
    ret = np.where(rel > 0, nb, 0)
    n = np.abs(rel)
    nf = np.maximum(n, 1).astype(np.float32)
    large = max_exact + (np.log(nf / max_exact) / math.log(REL_MAX_DIST / max_exact) * (nb - max_exact)).astype(np.int32)
    large = np.minimum(large, nb - 1)
    return (ret + np.where(n < max_exact, n, large)).astype(np.int32)


def _band_bias(table, tq, tk, lead, radius, dil):
    h = table.shape[1]
    n = tq + tk - 1
    offs = np.arange(n) - (tq - 1) - lead
    per_off = jnp.take(table.astype(F32), jnp.asarray(_t5_bucket(offs * dil)), axis=0).T * LOG2E
    padded = jnp.concatenate([per_off, jnp.zeros((h, 1), F32)], axis=1)
    toep = jnp.tile(padded, (1, tq))[:, :tq * n].reshape(h, tq, n)[:, :, tq - 1:tq - 1 + tk]
    off = np.arange(tk)[None, :] - lead - np.arange(tq)[:, None]
    band = np.abs(off) <= radius
    col = np.arange(tk)[None, :]
    first = band & (col >= lead)
    last = band & (col < tk - lead)
    masks = np.stack([band, first, last, first & last])
    return jnp.where(jnp.asarray(masks)[:, None], toep[None], NEG).swapaxes(-1, -2)


def _pair_heads(bias):
    v, h, tk, tq = bias.shape
    return bias.reshape(v, h // 2, 2, tk, tq).swapaxes(2, 3).reshape(v, h // 2, tk, 2 * tq)


def _edge_variant(i, n):
    return (i == 0).astype(jnp.int32) + 2 * (i == n - 1).astype(jnp.int32)


_PROJ_CHUNKS = (
    (0, 0, 0, 768, QSCALE), (0, 768, 768, 768, None), (0, 1536, 1536, 768, None),
    (1, 0, 2304, 512, None), (1, 512, 2816, 512, None), (1, 1024, 3328, 512, None),
    (2, 0, 3840, 768, QSCALE), (2, 768, 4608, 512, None),
)


A_DILATIONS = tuple(d for _, d in A_BRANCHES)


def _norm_proj_kernel(x_ref, g_ref, w_ref, *refs):
    n_a = len(A_DILATIONS)
    pa_refs, (pb_ref, pc_ref), stage_refs = refs[:n_a], refs[n_a:n_a + 2], refs[n_a + 2:]
    tm = x_ref.shape[0]
    x = x_ref[...]
    h = (x * lax.rsqrt(jnp.mean(x * x, axis=-1, keepdims=True) + EPS) * g_ref[...]).astype(BF16)
    outs = (None, pb_ref, pc_ref)
    a_chunk = 0
    for oi, oc, wc, width, scale in _PROJ_CHUNKS:
        y = jnp.dot(h, w_ref[:, wc:wc + width], preferred_element_type=F32)
        if scale is not None:
            y = y * scale
        if oi != 0:
            outs[oi][:, oc:oc + width] = y.astype(BF16)
            continue
        stage = stage_refs[a_chunk]
        a_chunk += 1
        for c in range(width // LANES):
            stage[c] = y[:, c * LANES:(c + 1) * LANES]
        for pa_ref, d in zip(pa_refs, A_DILATIONS):
            if d == 1:
                pa_ref[:, oc:oc + width] = y.astype(BF16)
                continue
            for r in range(d):
                for c in range(width // LANES):
                    col = r * PA_W + oc + c * LANES
                    pa_ref[:, col:col + LANES] = stage[c, pl.ds(r, tm // d, stride=d), :].astype(BF16)


def _norm_proj(x, g, w, tm=512):
    t = x.shape[0]
    assert t % tm == 0 and all(tm % (16 * d) == 0 for d in A_DILATIONS)
    n_a_chunks = sum(1 for c in _PROJ_CHUNKS if c[0] == 0)
    return pl.pallas_call(
        _norm_proj_kernel,
        grid=(t // tm,),
        in_specs=[
            pl.BlockSpec((tm, D_MODEL), lambda i: (i, 0)),
            pl.BlockSpec((1, D_MODEL), lambda i: (0, 0)),
            pl.BlockSpec((D_MODEL, IN_COLS), lambda i: (0, 0), pipeline_mode=pl.Buffered(1)),
        ],
        out_specs=[pl.BlockSpec((tm // d, d * PA_W), lambda i: (i, 0)) for d in A_DILATIONS] + [
            pl.BlockSpec((tm, PB_W), lambda i: (i, 0)),
            pl.BlockSpec((tm, PC_W), lambda i: (i, 0)),
        ],
        out_shape=[jax.ShapeDtypeStruct((t // d, d * PA_W), BF16) for d in A_DILATIONS] + [
            jax.ShapeDtypeStruct((t, PB_W), BF16),
            jax.ShapeDtypeStruct((t, PC_W), BF16),
        ],
        scratch_shapes=[pltpu.VMEM((A_W // LANES, tm, LANES), F32) for _ in range(n_a_chunks)],
        compiler_params=_cparams("parallel"),
        name="norm_in_proj",
    )(x, g.reshape(1, D_MODEL), w)


def _nt_dot(a, b):
    return lax.dot_general(a, b, (((1,), (1,)), ((), ())), preferred_element_type=F32)


def _masked_heads(qg):
    head = lax.broadcasted_iota(jnp.int32, qg.shape, 1) // HEAD_DIM
    zero = jnp.zeros_like(qg)
    return jnp.concatenate([jnp.where(head == e, qg, zero) for e in range(qg.shape[1] // HEAD_DIM)], axis=0)


A_GROUP = 4


def _dilated_attn_kernel(q_ref, kp_ref, kc_ref, kn_ref, vp_ref, vc_ref, vn_ref, bias_lo_ref, bias_hi_ref,
                         o_ref, lse_ref, s_even_ref, s_odd_ref):
    t = pl.program_id(0)
    n_groups = A_HEADS // A_GROUP

    @pl.when(t == 0)
    def _():
        s_odd_ref[...] = jnp.zeros_like(s_odd_ref)

    def step(s_write_ref, s_read_ref):
        kw = jnp.concatenate([kp_ref[0], kc_ref[0], kn_ref[0]], axis=0)
        for sub in range(A_SUB):
            rows = slice(sub * A_TQ, (sub + 1) * A_TQ)
            keys = slice(sub * A_TQ, sub * A_TQ + A_TK)
            for j in range(n_groups):
                lanes = slice(j * A_GROUP * HEAD_DIM, (j + 1) * A_GROUP * HEAD_DIM)
                s_write_ref[sub * n_groups + j] = _nt_dot(kw[keys, lanes], _masked_heads(q_ref[0, rows, lanes]))

        vt = jnp.concatenate([vp_ref[0], vc_ref[0], vn_ref[0]], axis=0).T
        for sub, bias_ref in enumerate((bias_lo_ref, bias_hi_ref)):
            rows = slice(sub * A_TQ, (sub + 1) * A_TQ)
            keys = slice(sub * A_TQ, sub * A_TQ + A_TK)
            lses = []
            for j in range(n_groups):
                pairs = range(j * A_GROUP // 2, (j + 1) * A_GROUP // 2)
                s = s_read_ref[sub * n_groups + j] + jnp.concatenate([bias_ref[pr] for pr in pairs], axis=1)
                m = jnp.max(s, axis=0, keepdims=True)
                p = jnp.exp2(s - m)
                l = jnp.sum(p, axis=0, keepdims=True)
                pb = p.astype(BF16)
                o_t = []
                for e in range(A_GROUP):
                    h = A_GROUP * j + e
                    cols = slice(e * A_TQ, (e + 1) * A_TQ)
                    o = jnp.dot(vt[h * HEAD_DIM:(h + 1) * HEAD_DIM, keys], pb[:, cols], preferred_element_type=F32)
                    o_t.append(o / l[:, cols])
                    lses.append(m[:, cols] + jnp.log2(l[:, cols]))
                for i, pr in enumerate(pairs):
                    tile = jnp.concatenate(o_t[2 * i:2 * i + 2], axis=0)
                    o_ref[0, rows, pr * LANES:(pr + 1) * LANES] = tile.T.astype(BF16)
            lse_t = jnp.concatenate(lses + [jnp.zeros((LANES - A_HEADS, A_TQ), F32)], axis=0)
            lse_ref[0, rows, :] = lse_t.T

    pl.when(t % 2 == 0)(functools.partial(step, s_even_ref, s_odd_ref))
    pl.when(t % 2 == 1)(functools.partial(step, s_odd_ref, s_even_ref))


def _dilated_branch(pav, bias, dil):
    b, l, _ = pav.shape
    step = A_SUB * A_TQ
    assert A_SUB == 2 and l % step == 0
    nt = l // step
    n_tiles = b * dil * nt
    halo = A_RADIUS
    hb = step // halo
    nblk = PA_W // A_W

    def decode(tt):
        return tt // (dil * nt), (tt // nt) % dil, tt % nt

    def scores_tile(t):
        return decode(jnp.minimum(t, n_tiles - 1))

    def finish_tile(t):
        return decode(jnp.maximum(t - 1, 0))

    def cur(tile, c):
        def index(t):
            bi, r, i = tile(t)
            return bi, i, r * nblk + c
        return pl.BlockSpec((1, step, A_W), index)

    def halo_spec(tile, c, side):
        def index(t):
            bi, r, i = tile(t)
            blk = jnp.maximum(i * hb - 1, 0) if side < 0 else jnp.minimum((i + 1) * hb, nt * hb - 1)
            return bi, blk, r * nblk + c
        return pl.BlockSpec((1, halo, A_W), index)

    def bias_spec(variant):
        def index(t):
            _, _, i = finish_tile(t)
            return variant(i), 0, 0, 0
        return pl.BlockSpec((None, A_HEADS // 2, A_TK, 2 * A_TQ), index)

    def out_spec(w):
        def index(t):
            bi, r, i = finish_tile(t)
            return bi, i, r
        return pl.BlockSpec((1, step, w), index)

    n_scores = A_SUB * (A_HEADS // A_GROUP)
    o, lse = pl.pallas_call(
        _dilated_attn_kernel,
        grid=(n_tiles + 1,),
        in_specs=[
            cur(scores_tile, 0),
            halo_spec(scores_tile, 1, -1), cur(scores_tile, 1), halo_spec(scores_tile, 1, 1),
            halo_spec(finish_tile, 2, -1), cur(finish_tile, 2), halo_spec(finish_tile, 2, 1),
            bias_spec(lambda i: (i == 0).astype(jnp.int32)),
            bias_spec(lambda i: 2 * (i == nt - 1).astype(jnp.int32)),
        ],
        out_specs=[out_spec(A_W), out_spec(LANES)],
        out_shape=[
            jax.ShapeDtypeStruct((b, l, dil * A_W), BF16),
            jax.ShapeDtypeStruct((b, l, dil * LANES), F32),
        ],
        scratch_shapes=[pltpu.VMEM((n_scores, A_TK, A_GROUP * A_TQ), F32) for _ in range(2)],
        compiler_params=_cparams("arbitrary"),
        name=f"dilated_attn_d{dil}",
    )(pav, pav, pav, pav, pav, pav, pav, bias, bias)
    return o.reshape(b * l, dil * A_W), lse.reshape(b * l, dil * LANES)


def _window_attn_kernel(sink_ref, q_ref, kp_ref, kc_ref, kn_ref, vp_ref, vc_ref, vn_ref, bias_ref, o_ref,
                        s_even_ref, s_odd_ref):
    t = pl.program_id(0)

    @pl.when(t == 0)
    def _():
        s_odd_ref[...] = jnp.zeros_like(s_odd_ref)

    def step(s_write_ref, s_read_ref):
        kw = jnp.concatenate([kp_ref[0], kc_ref[0], kn_ref[0]], axis=0)
        for g in range(C_GROUP):
            qg = jnp.concatenate([q_ref[0, :, pr * LANES:(pr + 1) * LANES] for pr in (g, g + C_GROUP)], axis=1)
            s_write_ref[g] = _nt_dot(kw, _masked_heads(qg))

        vt = jnp.concatenate([vp_ref[0], vc_ref[0], vn_ref[0]], axis=0).T
        for g in range(C_GROUP):
            pairs = (g, g + C_GROUP)
            s = s_read_ref[g] + jnp.concatenate([bias_ref[pr] for pr in pairs], axis=1)
            sk = jnp.concatenate(
                [jnp.full((1, C_TQ), sink_ref[2 * pr + e], F32) for pr in pairs for e in range(2)], axis=1)
            m = jnp.maximum(jnp.max(s, axis=0, keepdims=True), sk)
            p = jnp.exp2(s - m)
            den = jnp.sum(p, axis=0, keepdims=True) + jnp.exp2(sk - m)
            pb = p.astype(BF16)
            o_t = []
            for kv in range(C_KV_HEADS):
                cols = slice(kv * C_TQ, (kv + 1) * C_TQ)
                o = jnp.dot(vt[kv * HEAD_DIM:(kv + 1) * HEAD_DIM, :], pb[:, cols], preferred_element_type=F32)
                o_t.append(o / den[:, cols])
            for i, pr in enumerate(pairs):
                o_ref[0, :, pr * LANES:(pr + 1) * LANES] = jnp.concatenate(o_t[2 * i:2 * i + 2], axis=0).T.astype(BF16)

    pl.when(t % 2 == 0)(functools.partial(step, s_even_ref, s_odd_ref))
    pl.when(t % 2 == 1)(functools.partial(step, s_odd_ref, s_even_ref))


def _window_attn(pc, bias, sink):
    b, s, _ = pc.shape
    assert s % C_TQ == 0
    nt = s // C_TQ
    n_tiles = b * nt
    kblk = C_W // C_KV_W

    def scores_tile(t):
        tt = jnp.minimum(t, n_tiles - 1)
        return tt // nt, tt % nt

    def finish_tile(t):
        tt = jnp.maximum(t - 1, 0)
        return tt // nt, tt % nt

    def spec(shape, tile, c, shift):
        def index(t):
            bi, i = tile(t)
            return bi, jnp.clip(i + shift, 0, nt - 1), c
        return pl.BlockSpec(shape, index)

    def bias_index(t):
        _, i = finish_tile(t)
        return _edge_variant(i, nt), 0, 0, 0

    kv_shape = (1, C_TQ, C_KV_W)
    o = pl.pallas_call(
        _window_attn_kernel,
        grid=(n_tiles + 1,),
        in_specs=[
            pl.BlockSpec(memory_space=pltpu.SMEM),
            spec((1, C_TQ, C_W), scores_tile, 0, 0),
            spec(kv_shape, scores_tile, kblk, -1), spec(kv_shape, scores_tile, kblk, 0),
            spec(kv_shape, scores_tile, kblk, 1),
            spec(kv_shape, finish_tile, kblk + 1, -1), spec(kv_shape, finish_tile, kblk + 1, 0),
            spec(kv_shape, finish_tile, kblk + 1, 1),
            pl.BlockSpec((None, C_HEADS // 2, C_TK, 2 * C_TQ), bias_index),
        ],
        out_specs=spec((1, C_TQ, C_W), finish_tile, 0, 0),
        out_shape=jax.ShapeDtypeStruct((b, s, C_W), BF16),
        scratch_shapes=[pltpu.VMEM((C_GROUP, C_TK, C_KV_HEADS * C_TQ), F32) for _ in range(2)],
        compiler_params=_cparams("arbitrary"),
        name="window_gqa",
    )(sink.astype(F32), pc, pc, pc, pc, pc, pc, pc, bias)
    return o.reshape(b * s, C_W)


def _rope(x, cos, sin, lane_lo):
    parts = []
    for a in range(0, R_QK_W, LANES):
        xh = x[:, a:a + LANES]
        half = R_QK_DIM // 2
        parts.append(jnp.where(lane_lo, pltpu.roll(xh, LANES - half, 1), pltpu.roll(xh, half, 1)))
    return x * cos + jnp.concatenate(parts, axis=1) * sin


def _retention_chunk(q, k, v, cos, sin, decay, qdec, kdec, rdec, mask, state_ref):
    c = R_CHUNK
    lane = lax.broadcasted_iota(jnp.int32, (c, LANES), 1)
    lane_lo = (lane % R_QK_DIM) < (R_QK_DIM // 2)
    q = _rope(q.astype(F32), cos, sin, lane_lo)
    k = _rope(k.astype(F32), cos, sin, lane_lo) * (R_QK_DIM ** -0.5)
    state = state_ref[...]
    cross = jnp.dot((q * qdec).astype(BF16), state.astype(BF16), preferred_element_type=F32)

    qb = q.astype(BF16)
    kb = k.astype(BF16)
    head = lax.broadcasted_iota(jnp.int32, (c, R_QK_W), 1) // R_QK_DIM
    qs = jnp.concatenate([jnp.where(head == h, qb, jnp.zeros_like(qb)) for h in range(R_HEADS)], axis=0)
    s = lax.dot_general(qs, kb, (((1,), (1,)), ((), ())), preferred_element_type=F32)
    p = (s * decay).astype(BF16)

    vlane = lax.broadcasted_iota(jnp.int32, (c, R_W), 1) % LANES
    v_lo = jnp.where(vlane < R_V_DIM, v, jnp.zeros_like(v))
    v_hi = jnp.where(vlane >= R_V_DIM, v, jnp.zeros_like(v))
    pieces = []
    for j in range(R_HEADS // 2):
        cs = slice(j * LANES, (j + 1) * LANES)
        a = jnp.dot(p[(2 * j) * c:(2 * j + 1) * c], v_lo[:, cs], preferred_element_type=F32)
        a = a + jnp.dot(p[(2 * j + 1) * c:(2 * j + 2) * c], v_hi[:, cs], preferred_element_type=F32)
        pieces.append(a)
    intra = jnp.concatenate(pieces, axis=1)

    kv = lax.dot_general((k * kdec).astype(BF16), v, (((0,), (0,)), ((), ())), preferred_element_type=F32)
    state_ref[...] = state * rdec + kv * mask
    return cross + intra


def _retention_fwd_kernel(q_ref, k_ref, v_ref, cos_ref, sin_ref, decay_ref, qdec_ref, kdec_ref, rdec_ref,
                          mask_ref, o_ref, state_ref, *, n_chunks):
    @pl.when(pl.program_id(1) == 0)
    def _():
        state_ref[...] = jnp.zeros_like(state_ref)

    for ci in range(n_chunks):
        rows = slice(ci * R_CHUNK, (ci + 1) * R_CHUNK)
        o_ref[0, rows, :] = _retention_chunk(
            q_ref[0, rows, :], k_ref[0, rows, :], v_ref[0, rows, :], cos_ref[rows, :], sin_ref[rows, :],
            decay_ref[...], qdec_ref[...], kdec_ref[...], rdec_ref[...], mask_ref[...], state_ref)


def _split_dot(x, w):
    hi = x.astype(BF16)
    lo = (x - hi.astype(F32)).astype(BF16)
    return jnp.dot(hi, w, preferred_element_type=F32) + jnp.dot(lo, w, preferred_element_type=F32)


def _retention_bwd_kernel(q_ref, k_ref, v_ref, cos_ref, sin_ref, decay_ref, qdec_ref, kdec_ref, rdec_ref,
                          mask_ref, fwd_ref, gate_ref, avg_ref, o_ref, state_ref, *, n_chunks):
    @pl.when(pl.program_id(1) == 0)
    def _():
        state_ref[...] = jnp.zeros_like(state_ref)

    for ci in reversed(range(n_chunks)):
        rows = slice(ci * R_CHUNK, (ci + 1) * R_CHUNK)
        o = fwd_ref[0, rows, :] + _retention_chunk(
            q_ref[0, rows, :], k_ref[0, rows, :], v_ref[0, rows, :], cos_ref[rows, :], sin_ref[rows, :],
            decay_ref[...], qdec_ref[...], kdec_ref[...], rdec_ref[...], mask_ref[...], state_ref)
        avg = avg_ref[...]
        d = o - _split_dot(o, avg)
        y = d * lax.rsqrt(_split_dot(d * d, avg) + GN_EPS)
        g = gate_ref[0, rows, :].astype(F32)
        o_ref[0, rows, :] = (y * (g / (1.0 + jnp.exp(-g)))).astype(BF16)


def _retention_tables(lg, reverse):
    c = R_CHUNK
    t = np.arange(c)
    diff = (t[None, :] - t[:, None]) if reverse else (t[:, None] - t[None, :])
    keep = (diff > 0) if reverse else (diff >= 0)
    decay = jnp.where(jnp.asarray(keep)[None], jnp.exp(lg[:, None, None] * np.maximum(diff, 0).astype(np.float32)), 0.0)
    tf = t.astype(np.float32)
    qpow = (c - tf) if reverse else (tf + 1.0)
    kpow = tf if reverse else (c - 1.0 - tf)
    qdec = jnp.repeat(jnp.exp(lg[None, :] * qpow[:, None]), R_QK_DIM, axis=1)
    kdec = jnp.repeat(jnp.exp(lg[None, :] * kpow[:, None]), R_QK_DIM, axis=1)
    rdec = jnp.broadcast_to(jnp.repeat(jnp.exp(lg * c), R_QK_DIM)[:, None], (R_QK_W, R_W))
    return decay.reshape(R_HEADS * c, c), qdec, kdec, rdec


def _retention(pb, cos, sin, lg_f, lg_b):
    b, s, _ = pb.shape
    rows = R_CHUNK * R_STEP_CHUNKS
    assert s % rows == 0
    nt = s // rows
    hd = np.arange(R_QK_W)[:, None] // R_QK_DIM == np.arange(R_W)[None, :] // R_V_DIM
    mask = jnp.asarray(hd.astype(np.float32))
    gh = np.arange(R_W)[:, None] // R_V_DIM == np.arange(R_W)[None, :] // R_V_DIM
    avg = jnp.asarray(gh.astype(np.float32) / R_V_DIM, dtype=BF16)
    const = lambda shape: pl.BlockSpec(shape, lambda bi, i: (0,) * len(shape))

    def specs(order):
        seq = lambda w, cb: pl.BlockSpec((1, rows, w), lambda bi, i: (bi, order(i), cb))
        tab = pl.BlockSpec((rows, R_QK_W), lambda bi, i: (order(i), 0))
        return seq, [seq(R_QK_W, 0), seq(R_QK_W, 1), seq(R_W, 1), tab, tab,
                     const((R_HEADS * R_CHUNK, R_CHUNK)), const((R_CHUNK, R_QK_W)), const((R_CHUNK, R_QK_W)),
                     const((R_QK_W, R_W)), const((R_QK_W, R_W))]

    seq, in_specs = specs(lambda i: i)
    o_fwd = pl.pallas_call(
        functools.partial(_retention_fwd_kernel, n_chunks=R_STEP_CHUNKS),
        grid=(b, nt),
        in_specs=in_specs,
        out_specs=seq(R_W, 0),
        out_shape=jax.ShapeDtypeStruct((b, s, R_W), F32),
        scratch_shapes=[pltpu.VMEM((R_QK_W, R_W), F32)],
        compiler_params=_cparams("parallel", "arbitrary"),
        name="retention_fwd",
    )(pb, pb, pb, cos, sin, *_retention_tables(lg_f, False), mask)

    seq, in_specs = specs(lambda i: nt - 1 - i)
    out = pl.pallas_call(
        functools.partial(_retention_bwd_kernel, n_chunks=R_STEP_CHUNKS),
        grid=(b, nt),
        in_specs=in_specs + [seq(R_W, 0), seq(R_W, 2), const((R_W, R_W))],
        out_specs=seq(R_W, 0),
        out_shape=jax.ShapeDtypeStruct((b, s, R_W), BF16),
        scratch_shapes=[pltpu.VMEM((R_QK_W, R_W), F32)],
        compiler_params=_cparams("parallel", "arbitrary"),
        name="retention_bwd",
    )(pb, pb, pb, cos, sin, *_retention_tables(lg_b, True), mask, o_fwd, pb, avg)
    return out.reshape(b * s, R_W)


def _rope_tables(s):
    half = R_QK_DIM // 2
    freqs = ROPE_BASE ** (-jnp.arange(half, dtype=F32) / half)
    ang = jnp.arange(s, dtype=F32)[:, None] * freqs[None]
    cos, sin = jnp.cos(ang), jnp.sin(ang)
    return (jnp.tile(jnp.concatenate([cos, cos], axis=1), (1, R_HEADS)),
            jnp.tile(jnp.concatenate([-sin, sin], axis=1), (1, R_HEADS)))


def _natural_order(o_ref, lse_ref, o_stage, lse_stage, d):
    if d == 1:
        return o_ref[...].astype(F32), lse_ref[...]
    n = o_ref.shape[0]
    for r in range(d):
        lse_stage[pl.ds(r, n, stride=d), :] = lse_ref[:, r * LANES:(r + 1) * LANES]
        for c in range(A_W // LANES):
            col = r * A_W + c * LANES
            o_stage[c, pl.ds(r, n, stride=d), :] = o_ref[:, col:col + LANES].astype(F32)
    return jnp.concatenate([o_stage[c] for c in range(A_W // LANES)], axis=1), lse_stage[...]


def _merge_out_kernel(*refs):
    n_a = len(A_DILATIONS)
    o_refs, lse_refs = refs[:n_a], refs[n_a:2 * n_a]
    r_ref, c_ref, x_ref, w_ref, g_ref, expand_ref, x1_ref, h2_ref = refs[2 * n_a:2 * n_a + 8]
    stages = refs[2 * n_a + 8:]
    outs, lses = [], []
    for b, d in enumerate(A_DILATIONS):
        o, lse = _natural_order(o_refs[b], lse_refs[b], stages[2 * b], stages[2 * b + 1], d)
        outs.append(o)
        lses.append(lse)
    top = functools.reduce(jnp.maximum, lses)
    ws = [jnp.exp2(l - top) for l in lses]
    den = functools.reduce(lambda u, v: u + v, ws)
    expand = expand_ref[...]
    a = None
    for wgt, o in zip(ws, outs):
        term = _split_dot(wgt / den, expand) * o
        a = term if a is None else a + term
    a = a.astype(BF16)
    r = r_ref[...]
    c = c_ref[...]
    nc = 512
    for n0 in range(0, D_MODEL, nc):
        cols = slice(n0, n0 + nc)
        y = jnp.dot(a, w_ref[0:A_W, cols], preferred_element_type=F32)
        y = y + jnp.dot(r, w_ref[A_W:A_W + R_W, cols], preferred_element_type=F32)
        y = y + jnp.dot(c, w_ref[A_W + R_W:MIX_W, cols], preferred_element_type=F32)
        x1_ref[:, cols] = x_ref[:, cols] + y
    x1 = x1_ref[...]
    h2 = x1 * lax.rsqrt(jnp.mean(x1 * x1, axis=-1, keepdims=True) + EPS) * g_ref[...]
    h2_ref[...] = h2.astype(BF16)


def _merge_out(branches, r, c, x, w, g, tm=512):
    t = x.shape[0]
    assert t % tm == 0 and all(tm % (16 * d) == 0 for d in A_DILATIONS)
    expand = jnp.asarray((np.arange(LANES)[:, None] == np.arange(A_W)[None, :] // HEAD_DIM).astype(np.float32), dtype=BF16)
    row = lambda w_: pl.BlockSpec((tm, w_), lambda i: (i, 0))
    dil_row = lambda w_, d: pl.BlockSpec((tm // d, d * w_), lambda i: (i, 0))
    stages = []
    for _ in A_DILATIONS:
        stages += [pltpu.VMEM((A_W // LANES, tm, LANES), F32), pltpu.VMEM((tm, LANES), F32)]
    return pl.pallas_call(
        _merge_out_kernel,
        grid=(t // tm,),
        in_specs=[dil_row(A_W, d) for d in A_DILATIONS] + [dil_row(LANES, d) for d in A_DILATIONS] + [
            row(R_W), row(C_W), row(D_MODEL),
            pl.BlockSpec((MIX_W, D_MODEL), lambda i: (0, 0), pipeline_mode=pl.Buffered(1)),
            pl.BlockSpec((1, D_MODEL), lambda i: (0, 0)),
            pl.BlockSpec((LANES, A_W), lambda i: (0, 0))],
        out_specs=[row(D_MODEL), row(D_MODEL)],
        out_shape=[jax.ShapeDtypeStruct((t, D_MODEL), F32), jax.ShapeDtypeStruct((t, D_MODEL), BF16)],
        scratch_shapes=stages,
        compiler_params=_cparams("parallel"),
        name="merge_out_proj",
    )(*[o for o, _ in branches], *[l for _, l in branches], r, c, x, w, g.reshape(1, D_MODEL), expand)


def _ffn_accumulate(h_ref, x_ref, wg_ref, wu_ref, wd_ref, o_ref):
    @pl.when(pl.program_id(1) == 0)
    def _():
        o_ref[...] = x_ref[...]

    h = h_ref[...]
    tf = wg_ref.shape[1]
    halves = [slice(c, c + tf // 2) for c in (0, tf // 2)]
    gs = [jnp.dot(h, wg_ref[:, c], preferred_element_type=F32) for c in halves]
    us = [jnp.dot(h, wu_ref[:, c], preferred_element_type=F32) for c in halves]
    y = None
    for g, u, c in zip(gs, us, halves):
        a = (g / (1.0 + jnp.exp(-g)) * u).astype(BF16)
        part = jnp.dot(a, wd_ref[c, :], preferred_element_type=F32)
        y = part if y is None else y + part
    o_ref[...] += y


def _ffn_kernel(h_ref, x_ref, wg_ref, wu_ref, wd_ref, o_ref):
    _ffn_accumulate(h_ref, x_ref, wg_ref, wu_ref, wd_ref, o_ref)


def _ffn_norm_kernel(h_ref, x_ref, wg_ref, wu_ref, wd_ref, g_ref, o_ref):
    _ffn_accumulate(h_ref, x_ref, wg_ref, wu_ref, wd_ref, o_ref)

    @pl.when(pl.program_id(1) == pl.num_programs(1) - 1)
    def _():
        x = o_ref[...]
        o_ref[...] = x * lax.rsqrt(jnp.mean(x * x, axis=-1, keepdims=True) + EPS) * g_ref[...]


def _ffn(h, x, wg, wu, wd, out_norm_g=None, tm=512, tf=512):
    t = x.shape[0]
    assert t % tm == 0 and D_FF % tf == 0
    in_specs = [
        pl.BlockSpec((tm, D_MODEL), lambda i, k: (i, 0)),
        pl.BlockSpec((tm, D_MODEL), lambda i, k: (i, 0)),
        pl.BlockSpec((D_MODEL, tf), lambda i, k: (0, k)),
        pl.BlockSpec((D_MODEL, tf), lambda i, k: (0, k)),
        pl.BlockSpec((tf, D_MODEL), lambda i, k: (k, 0)),
    ]
    args = [h, x, wg, wu, wd]
    body = _ffn_kernel
    if out_norm_g is not None:
        in_specs.append(pl.BlockSpec((1, D_MODEL), lambda i, k: (0, 0)))
        args.append(out_norm_g.reshape(1, D_MODEL))
        body = _ffn_norm_kernel
    return pl.pallas_call(
        body,
        grid=(t // tm, D_FF // tf),
        in_specs=in_specs,
        out_specs=pl.BlockSpec((tm, D_MODEL), lambda i, k: (i, 0)),
        out_shape=jax.ShapeDtypeStruct((t, D_MODEL), F32),
        compiler_params=_cparams("parallel", "arbitrary"),
        name="swiglu_ffn",
    )(*args)


def _layer(x, b, s, p, out_norm_g):
    *pas, pb, pc = _norm_proj(x, p["g1"], p["w_in"])
    branches = [_dilated_branch(pa.reshape(b, s // d, d * PA_W), bias, d)
                for pa, bias, d in zip(pas, p["bias_a"], A_DILATIONS)]
    r = _retention(pb.reshape(b, s, PB_W), p["cos"], p["sin"], p["lg_f"], p["lg_b"])
    c = _window_attn(pc.reshape(b, s, PC_W), p["bias_c"], p["sink"])
    x1, h2 = _merge_out(branches, r, c, x, p["w_out"], p["g2"])
    return _ffn(h2, x1, p["w_gate"], p["w_up"], p["w_down"], out_norm_g)


def kernel(x_prompt, x_sample, rel_bias, norm1_g, w_in, ret_decay_fwd, ret_decay_bwd, attn_sink, w_out, norm2_g,
           w_gate, w_up, w_down, final_norm_g):
    bias_a = [_pair_heads(_band_bias(rel_bias[:, :A_HEADS], A_TQ, A_TK, A_RADIUS, A_RADIUS, dil))
              for _, dil in A_BRANCHES]
    q_order = np.asarray(C_Q_ORDER)
    bias_c = _pair_heads(_band_bias(rel_bias[:, A_HEADS:][:, q_order], C_TQ, C_TK, C_TQ, C_RADIUS, 1))

    def reorder_c_heads(w, axis, start):
        take = lambda a, n: lax.slice_in_dim(w, a, a + n, axis=axis)
        heads = [take(start + h * HEAD_DIM, HEAD_DIM) for h in C_Q_ORDER]
        tail = start + C_W
        return jnp.concatenate([take(0, start)] + heads + [take(tail, w.shape[axis] - tail)], axis=axis)

    layers = []
    for i in range(DEPTH):
        layers.append(dict(
            g1=norm1_g[i], g2=norm2_g[i], sink=attn_sink[i].astype(F32)[q_order] * LOG2E, bias_a=bias_a, bias_c=bias_c,
            w_in=reorder_c_heads(w_in[i].astype(BF16), 1, PA_W + PB_W),
            w_out=reorder_c_heads(w_out[i].astype(BF16), 0, A_W + R_W),
            w_gate=w_gate[i].astype(BF16), w_up=w_up[i].astype(BF16), w_down=w_down[i].astype(BF16),
            lg_f=jnp.log1p(-jnp.exp2(-ret_decay_fwd[i].astype(F32))),
            lg_b=jnp.log1p(-jnp.exp2(-ret_decay_bwd[i].astype(F32))),
        ))

    def trunk(x):
        b, s, _ = x.shape
        cos, sin = _rope_tables(s)
        x = x.reshape(b * s, D_MODEL)
        for i, p in enumerate(layers):
            x = _layer(x, b, s, dict(p, cos=cos, sin=sin), final_norm_g if i == DEPTH - 1 else None)
        return x.reshape(b, s, D_MODEL)

    return trunk(x_prompt), trunk(x_sample)
```

```python
import functools
import math

import numpy as np
import jax
import jax.numpy as jnp
from jax import lax
from jax.experimental import pallas as pl
from jax.experimental.pallas import tpu as pltpu

D_MODEL = 2048
DEPTH = 2
HEAD_DIM = 64
A_HEADS = 12
A_BRANCHES = ((128, 1), (512, 4), (2048, 16))
A_RADIUS = 64
R_HEADS = 8
R_QK_DIM = 32
R_V_DIM = 64
R_CHUNK = 128
ROPE_BASE = 10000.0
C_HEADS = 12
C_KV_HEADS = 4
C_GROUP = C_HEADS // C_KV_HEADS
C_RADIUS = 128
REL_BUCKETS = 32
REL_MAX_DIST = 1024
D_FF = 5632
EPS = 1e-6
GN_EPS = 1e-5
NEG = -1e30

A_W = A_HEADS * HEAD_DIM
R_QK_W = R_HEADS * R_QK_DIM
R_W = R_HEADS * R_V_DIM
C_W = C_HEADS * HEAD_DIM
C_KV_W = C_KV_HEADS * HEAD_DIM
PA_W = 3 * A_W
PB_W = 2 * R_QK_W + 2 * R_W
PC_W = C_W + 2 * C_KV_W
IN_COLS = PA_W + PB_W + PC_W
MIX_W = A_W + R_W + C_W

LANES = 128
V7X_VMEM_BYTES = 64 * 1024 * 1024
VMEM_LIMIT = V7X_VMEM_BYTES - 4 * 1024 * 1024

A_TQ = 128
A_TK = A_TQ + 2 * A_RADIUS
A_SUB = 2
C_TQ = 128
C_TK = 3 * C_TQ
C_Q_ORDER = tuple(kp * C_GROUP + j + e * C_GROUP
                  for kp in range(0, C_KV_HEADS, 2) for j in range(C_GROUP) for e in range(2))
R_STEP_CHUNKS = 4

LOG2E = 1.4426950408889634
QSCALE = HEAD_DIM ** -0.5 * LOG2E

BF16 = jnp.bfloat16
F32 = jnp.float32


def _cparams(*sem):
    return pltpu.CompilerParams(dimension_semantics=sem, vmem_limit_bytes=VMEM_LIMIT)


def _t5_bucket(rel):
    nb = REL_BUCKETS // 2
    max_exact = nb // 2
    ret = np.where(rel > 0, nb, 0)
    n = np.abs(rel)
    nf = np.maximum(n, 1).astype(np.float32)
    large = max_exact + (np.log(nf / max_exact) / math.log(REL_MAX_DIST / max_exact) * (nb - max_exact)).astype(np.int32)
    large = np.minimum(large, nb - 1)
    return (ret + np.where(n < max_exact, n, large)).astype(np.int32)


def _band_bias(table, tq, tk, lead, radius, dil):
    h = table.shape[1]
    n = tq + tk - 1
    offs = np.arange(n) - (tq - 1) - lead
    per_off = jnp.take(table.astype(F32), jnp.asarray(_t5_bucket(offs * dil)), axis=0).T * LOG2E
    padded = jnp.concatenate([per_off, jnp.zeros((h, 1), F32)], axis=1)
    toep = jnp.tile(padded, (1, tq))[:, :tq * n].reshape(h, tq, n)[:, :, tq - 1:tq - 1 + tk]
    off = np.arange(tk)[None, :] - lead - np.arange(tq)[:, None]
    band = np.abs(off) <= radius
    col = np.arange(tk)[None, :]
    first = band & (col >= lead)
    last = band & (col < tk - lead)
    masks = np.stack([band, first, last, first & last])
    return jnp.where(jnp.asarray(masks)[:, None], toep[None], NEG).swapaxes(-1, -2)


def _pair_heads(bias):
    v, h, tk, tq = bias.shape
    return bias.reshape(v, h // 2, 2, tk, tq).swapaxes(2, 3).reshape(v, h // 2, tk, 2 * tq)


def _edge_variant(i, n):
    return (i == 0).astype(jnp.int32) + 2 * (i == n - 1).astype(jnp.int32)


_PROJ_CHUNKS = (
    (0, 0, 0, 768, QSCALE), (0, 768, 768, 768, None), (0, 1536, 1536, 768, None),
    (1, 0, 2304, 512, None), (1, 512, 2816, 512, None), (1, 1024, 3328, 512, None),
    (2, 0, 3840, 768, QSCALE), (2, 768, 4608, 512, None),
)


A_DILATIONS = tuple(d for _, d in A_BRANCHES)


def _norm_proj_kernel(x_ref, g_ref, w_ref, *refs):
    n_a = len(A_DILATIONS)
    pa_refs, (pb_ref, pc_ref), stage_refs = refs[:n_a], refs[n_a:n_a + 2], refs[n_a + 2:]
    tm = x_ref.shape[0]
    x = x_ref[...]
    h = (x * lax.rsqrt(jnp.mean(x * x, axis=-1, keepdims=True) + EPS) * g_ref[...]).astype(BF16)
    outs = (None, pb_ref, pc_ref)
    a_chunk = 0
    for oi, oc, wc, width, scale in _PROJ_CHUNKS:
        y = jnp.dot(h, w_ref[:, wc:wc + width], preferred_element_type=F32)
        if scale is not None:
            y = y * scale
        if oi != 0:
            outs[oi][:, oc:oc + width] = y.astype(BF16)
            continue
        stage, stage_mid = stage_refs[2 * a_chunk], stage_refs[2 * a_chunk + 1]
        a_chunk += 1
        _, d1, d2 = A_DILATIONS
        slabs = range(width // LANES)
        for c in slabs:
            stage[c] = y[:, c * LANES:(c + 1) * LANES]
        pa_refs[0][:, oc:oc + width] = y.astype(BF16)
        for r in range(d1):
            for c in slabs:
                part = stage[c, pl.ds(r, tm // d1, stride=d1), :]
                stage_mid[c, r] = part
                col = r * PA_W + oc + c * LANES
                pa_refs[1][:, col:col + LANES] = part.astype(BF16)
        for r in range(d1):
            for a in range(d2 // d1):
                for c in slabs:
                    col = (a * d1 + r) * PA_W + oc + c * LANES
                    part = stage_mid[c, r, pl.ds(a, tm // d2, stride=d2 // d1), :]
                    pa_refs[2][:, col:col + LANES] = part.astype(BF16)


def _norm_proj(x, g, w, tm=512):
    t = x.shape[0]
    assert t % tm == 0 and all(tm % (16 * d) == 0 for d in A_DILATIONS)
    d0, d1, d2 = A_DILATIONS
    assert d0 == 1 and d2 % d1 == 0
    n_a_chunks = sum(1 for c in _PROJ_CHUNKS if c[0] == 0)
    stages = []
    for _ in range(n_a_chunks):
        stages += [pltpu.VMEM((A_W // LANES, tm, LANES), F32), pltpu.VMEM((A_W // LANES, d1, tm // d1, LANES), F32)]
    return pl.pallas_call(
        _norm_proj_kernel,
        grid=(t // tm,),
        in_specs=[
            pl.BlockSpec((tm, D_MODEL), lambda i: (i, 0)),
            pl.BlockSpec((1, D_MODEL), lambda i: (0, 0)),
            pl.BlockSpec((D_MODEL, IN_COLS), lambda i: (0, 0), pipeline_mode=pl.Buffered(1)),
        ],
        out_specs=[pl.BlockSpec((tm // d, d * PA_W), lambda i: (i, 0)) for d in A_DILATIONS] + [
            pl.BlockSpec((tm, PB_W), lambda i: (i, 0)),
            pl.BlockSpec((tm, PC_W), lambda i: (i, 0)),
        ],
        out_shape=[jax.ShapeDtypeStruct((t // d, d * PA_W), BF16) for d in A_DILATIONS] + [
            jax.ShapeDtypeStruct((t, PB_W), BF16),
            jax.ShapeDtypeStruct((t, PC_W), BF16),
        ],
        scratch_shapes=stages,
        compiler_params=_cparams("parallel"),
        name="norm_in_proj",
    )(x, g.reshape(1, D_MODEL), w)


def _nt_dot(a, b):
    return lax.dot_general(a, b, (((1,), (1,)), ((), ())), preferred_element_type=F32)


def _masked_heads(qg):
    head = lax.broadcasted_iota(jnp.int32, qg.shape, 1) // HEAD_DIM
    zero = jnp.zeros_like(qg)
    return jnp.concatenate([jnp.where(head == e, qg, zero) for e in range(qg.shape[1] // HEAD_DIM)], axis=0)


A_GROUP = 4


def _dilated_attn_kernel(q_ref, kp_ref, kc_ref, kn_ref, vp_ref, vc_ref, vn_ref, bias_lo_ref, bias_hi_ref,
                         o_ref, lse_ref, s_even_ref, s_odd_ref):
    t = pl.program_id(0)
    n_groups = A_HEADS // A_GROUP

    @pl.when(t == 0)
    def _():
        s_odd_ref[...] = jnp.zeros_like(s_odd_ref)

    def step(s_write_ref, s_read_ref):
        kw = jnp.concatenate([kp_ref[0], kc_ref[0], kn_ref[0]], axis=0)
        for sub in range(A_SUB):
            rows = slice(sub * A_TQ, (sub + 1) * A_TQ)
            keys = slice(sub * A_TQ, sub * A_TQ + A_TK)
            for j in range(n_groups):
                lanes = slice(j * A_GROUP * HEAD_DIM, (j + 1) * A_GROUP * HEAD_DIM)
                s_write_ref[sub * n_groups + j] = _nt_dot(kw[keys, lanes], _masked_heads(q_ref[0, rows, lanes]))

        vt = jnp.concatenate([vp_ref[0], vc_ref[0], vn_ref[0]], axis=0).T
        for sub, bias_ref in enumerate((bias_lo_ref, bias_hi_ref)):
            rows = slice(sub * A_TQ, (sub + 1) * A_TQ)
            keys = slice(sub * A_TQ, sub * A_TQ + A_TK)
            lses = []
            for j in range(n_groups):
                pairs = range(j * A_GROUP // 2, (j + 1) * A_GROUP // 2)
                s = s_read_ref[sub * n_groups + j] + jnp.concatenate([bias_ref[pr] for pr in pairs], axis=1)
                m = jnp.max(s, axis=0, keepdims=True)
                p = jnp.exp2(s - m)
                l = jnp.sum(p, axis=0, keepdims=True)
                pb = p.astype(BF16)
                o_t = []
                for e in range(A_GROUP):
                    h = A_GROUP * j + e
                    cols = slice(e * A_TQ, (e + 1) * A_TQ)
                    o = jnp.dot(vt[h * HEAD_DIM:(h + 1) * HEAD_DIM, keys], pb[:, cols], preferred_element_type=F32)
                    o_t.append(o / l[:, cols])
                    lses.append(m[:, cols] + jnp.log2(l[:, cols]))
                for i, pr in enumerate(pairs):
                    tile = jnp.concatenate(o_t[2 * i:2 * i + 2], axis=0)
                    o_ref[0, rows, pr * LANES:(pr + 1) * LANES] = tile.T.astype(BF16)
            lse_t = jnp.concatenate(lses + [jnp.zeros((LANES - A_HEADS, A_TQ), F32)], axis=0)
            lse_ref[0, rows, :] = lse_t.T

    pl.when(t % 2 == 0)(functools.partial(step, s_even_ref, s_odd_ref))
    pl.when(t % 2 == 1)(functools.partial(step, s_odd_ref, s_even_ref))


def _dilated_branch(pav, bias, dil):
    b, l, _ = pav.shape
    step = A_SUB * A_TQ
    assert A_SUB == 2 and l % step == 0
    nt = l // step
    n_tiles = b * dil * nt
    halo = A_RADIUS
    hb = step // halo
    nblk = PA_W // A_W

    def decode(tt):
        return tt // (dil * nt), (tt // nt) % dil, tt % nt

    def scores_tile(t):
        return decode(jnp.minimum(t, n_tiles - 1))

    def finish_tile(t):
        return decode(jnp.maximum(t - 1, 0))

    def cur(tile, c):
        def index(t):
            bi, r, i = tile(t)
            return bi, i, r * nblk + c
        return pl.BlockSpec((1, step, A_W), index)

    def halo_spec(tile, c, side):
        def index(t):
            bi, r, i = tile(t)
            blk = jnp.maximum(i * hb - 1, 0) if side < 0 else jnp.minimum((i + 1) * hb, nt * hb - 1)
            return bi, blk, r * nblk + c
        return pl.BlockSpec((1, halo, A_W), index)

    def bias_spec(variant):
        def index(t):
            _, _, i = finish_tile(t)
            return variant(i), 0, 0, 0
        return pl.BlockSpec((None, A_HEADS // 2, A_TK, 2 * A_TQ), index)

    def out_spec(w):
        def index(t):
            bi, r, i = finish_tile(t)
            return bi, i, r
        return pl.BlockSpec((1, step, w), index)

    n_scores = A_SUB * (A_HEADS // A_GROUP)
    o, lse = pl.pallas_call(
        _dilated_attn_kernel,
        grid=(n_tiles + 1,),
        in_specs=[
            cur(scores_tile, 0),
            halo_spec(scores_tile, 1, -1), cur(scores_tile, 1), halo_spec(scores_tile, 1, 1),
            halo_spec(finish_tile, 2, -1), cur(finish_tile, 2), halo_spec(finish_tile, 2, 1),
            bias_spec(lambda i: (i == 0).astype(jnp.int32)),
            bias_spec(lambda i: 2 * (i == nt - 1).astype(jnp.int32)),
        ],
        out_specs=[out_spec(A_W), out_spec(LANES)],
        out_shape=[
            jax.ShapeDtypeStruct((b, l, dil * A_W), BF16),
            jax.ShapeDtypeStruct((b, l, dil * LANES), F32),
        ],
        scratch_shapes=[pltpu.VMEM((n_scores, A_TK, A_GROUP * A_TQ), F32) for _ in range(2)],
        compiler_params=_cparams("arbitrary"),
        name=f"dilated_attn_d{dil}",
    )(pav, pav, pav, pav, pav, pav, pav, bias, bias)
    return o.reshape(b * l, dil * A_W), lse.reshape(b * l, dil * LANES)


def _window_attn_kernel(sink_ref, q_ref, kp_ref, kc_ref, kn_ref, vp_ref, vc_ref, vn_ref, bias_ref, o_ref,
                        s_even_ref, s_odd_ref):
    t = pl.program_id(0)

    @pl.when(t == 0)
    def _():
        s_odd_ref[...] = jnp.zeros_like(s_odd_ref)

    def step(s_write_ref, s_read_ref):
        kw = jnp.concatenate([kp_ref[0], kc_ref[0], kn_ref[0]], axis=0)
        for g in range(C_GROUP):
            qg = jnp.concatenate([q_ref[0, :, pr * LANES:(pr + 1) * LANES] for pr in (g, g + C_GROUP)], axis=1)
            s_write_ref[g] = _nt_dot(kw, _masked_heads(qg))

        vt = jnp.concatenate([vp_ref[0], vc_ref[0], vn_ref[0]], axis=0).T
        for g in range(C_GROUP):
            pairs = (g, g + C_GROUP)
            s = s_read_ref[g] + jnp.concatenate([bias_ref[pr] for pr in pairs], axis=1)
            sk = jnp.concatenate(
                [jnp.full((1, C_TQ), sink_ref[2 * pr + e], F32) for pr in pairs for e in range(2)], axis=1)
            m = jnp.maximum(jnp.max(s, axis=0, keepdims=True), sk)
            p = jnp.exp2(s - m)
            den = jnp.sum(p, axis=0, keepdims=True) + jnp.exp2(sk - m)
            pb = p.astype(BF16)
            o_t = []
            for kv in range(C_KV_HEADS):
                cols = slice(kv * C_TQ, (kv + 1) * C_TQ)
                o = jnp.dot(vt[kv * HEAD_DIM:(kv + 1) * HEAD_DIM, :], pb[:, cols], preferred_element_type=F32)
                o_t.append(o / den[:, cols])
            for i, pr in enumerate(pairs):
                o_ref[0, :, pr * LANES:(pr + 1) * LANES] = jnp.concatenate(o_t[2 * i:2 * i + 2], axis=0).T.astype(BF16)

    pl.when(t % 2 == 0)(functools.partial(step, s_even_ref, s_odd_ref))
    pl.when(t % 2 == 1)(functools.partial(step, s_odd_ref, s_even_ref))


def _window_attn(pc, bias, sink):
    b, s, _ = pc.shape
    assert s % C_TQ == 0
    nt = s // C_TQ
    n_tiles = b * nt
    kblk = C_W // C_KV_W

    def scores_tile(t):
        tt = jnp.minimum(t, n_tiles - 1)
        return tt // nt, tt % nt

    def finish_tile(t):
        tt = jnp.maximum(t - 1, 0)
        return tt // nt, tt % nt

    def spec(shape, tile, c, shift):
        def index(t):
            bi, i = tile(t)
            return bi, jnp.clip(i + shift, 0, nt - 1), c
        return pl.BlockSpec(shape, index)

    def bias_index(t):
        _, i = finish_tile(t)
        return _edge_variant(i, nt), 0, 0, 0

    kv_shape = (1, C_TQ, C_KV_W)
    o = pl.pallas_call(
        _window_attn_kernel,
        grid=(n_tiles + 1,),
        in_specs=[
            pl.BlockSpec(memory_space=pltpu.SMEM),
            spec((1, C_TQ, C_W), scores_tile, 0, 0),
            spec(kv_shape, scores_tile, kblk, -1), spec(kv_shape, scores_tile, kblk, 0),
            spec(kv_shape, scores_tile, kblk, 1),
            spec(kv_shape, finish_tile, kblk + 1, -1), spec(kv_shape, finish_tile, kblk + 1, 0),
            spec(kv_shape, finish_tile, kblk + 1, 1),
            pl.BlockSpec((None, C_HEADS // 2, C_TK, 2 * C_TQ), bias_index),
        ],
        out_specs=spec((1, C_TQ, C_W), finish_tile, 0, 0),
        out_shape=jax.ShapeDtypeStruct((b, s, C_W), BF16),
        scratch_shapes=[pltpu.VMEM((C_GROUP, C_TK, C_KV_HEADS * C_TQ), F32) for _ in range(2)],
        compiler_params=_cparams("arbitrary"),
        name="window_gqa",
    )(sink.astype(F32), pc, pc, pc, pc, pc, pc, pc, bias)
    return o.reshape(b * s, C_W)


def _rope(x, cos, sin, lane_lo):
    parts = []
    for a in range(0, R_QK_W, LANES):
        xh = x[:, a:a + LANES]
        half = R_QK_DIM // 2
        parts.append(jnp.where(lane_lo, pltpu.roll(xh, LANES - half, 1), pltpu.roll(xh, half, 1)))
    return x * cos + jnp.concatenate(parts, axis=1) * sin


def _retention_chunk(q, k, v, cos, sin, decay, qdec, kdec, rdec, mask, state_ref):
    c = R_CHUNK
    lane = lax.broadcasted_iota(jnp.int32, (c, LANES), 1)
    lane_lo = (lane % R_QK_DIM) < (R_QK_DIM // 2)
    q = _rope(q.astype(F32), cos, sin, lane_lo)
    k = _rope(k.astype(F32), cos, sin, lane_lo) * (R_QK_DIM ** -0.5)
    state = state_ref[...]
    cross = jnp.dot((q * qdec).astype(BF16), state.astype(BF16), preferred_element_type=F32)

    qb = q.astype(BF16)
    kb = k.astype(BF16)
    head = lax.broadcasted_iota(jnp.int32, (c, R_QK_W), 1) // R_QK_DIM
    qs = jnp.concatenate([jnp.where(head == h, qb, jnp.zeros_like(qb)) for h in range(R_HEADS)], axis=0)
    s = lax.dot_general(qs, kb, (((1,), (1,)), ((), ())), preferred_element_type=F32)
    p = (s * decay).astype(BF16)

    vlane = lax.broadcasted_iota(jnp.int32, (c, R_W), 1) % LANES
    v_lo = jnp.where(vlane < R_V_DIM, v, jnp.zeros_like(v))
    v_hi = jnp.where(vlane >= R_V_DIM, v, jnp.zeros_like(v))
    pieces = []
    for j in range(R_HEADS // 2):
        cs = slice(j * LANES, (j + 1) * LANES)
        a = jnp.dot(p[(2 * j) * c:(2 * j + 1) * c], v_lo[:, cs], preferred_element_type=F32)
        a = a + jnp.dot(p[(2 * j + 1) * c:(2 * j + 2) * c], v_hi[:, cs], preferred_element_type=F32)
        pieces.append(a)
    intra = jnp.concatenate(pieces, axis=1)

    kv = lax.dot_general((k * kdec).astype(BF16), v, (((0,), (0,)), ((), ())), preferred_element_type=F32)
    state_ref[...] = state * rdec + kv * mask
    return cross + intra


def _retention_fwd_kernel(q_ref, k_ref, v_ref, cos_ref, sin_ref, decay_ref, qdec_ref, kdec_ref, rdec_ref,
                          mask_ref, o_ref, state_ref, *, n_chunks):
    @pl.when(pl.program_id(1) == 0)
    def _():
        state_ref[...] = jnp.zeros_like(state_ref)

    for ci in range(n_chunks):
        rows = slice(ci * R_CHUNK, (ci + 1) * R_CHUNK)
        o_ref[0, rows, :] = _retention_chunk(
            q_ref[0, rows, :], k_ref[0, rows, :], v_ref[0, rows, :], cos_ref[rows, :], sin_ref[rows, :],
            decay_ref[...], qdec_ref[...], kdec_ref[...], rdec_ref[...], mask_ref[...], state_ref)


def _split_dot(x, w):
    hi = x.astype(BF16)
    lo = (x - hi.astype(F32)).astype(BF16)
    return jnp.dot(hi, w, preferred_element_type=F32) + jnp.dot(lo, w, preferred_element_type=F32)


def _retention_bwd_kernel(q_ref, k_ref, v_ref, cos_ref, sin_ref, decay_ref, qdec_ref, kdec_ref, rdec_ref,
                          mask_ref, fwd_ref, gate_ref, avg_ref, o_ref, state_ref, *, n_chunks):
    @pl.when(pl.program_id(1) == 0)
    def _():
        state_ref[...] = jnp.zeros_like(state_ref)

    for ci in reversed(range(n_chunks)):
        rows = slice(ci * R_CHUNK, (ci + 1) * R_CHUNK)
        o = fwd_ref[0, rows, :] + _retention_chunk(
            q_ref[0, rows, :], k_ref[0, rows, :], v_ref[0, rows, :], cos_ref[rows, :], sin_ref[rows, :],
            decay_ref[...], qdec_ref[...], kdec_ref[...], rdec_ref[...], mask_ref[...], state_ref)
        avg = avg_ref[...]
        d = o - jnp.dot(o.astype(BF16), avg, preferred_element_type=F32)
        y = d * lax.rsqrt(jnp.dot((d * d).astype(BF16), avg, preferred_element_type=F32) + GN_EPS)
        g = gate_ref[0, rows, :].astype(F32)
        o_ref[0, rows, :] = (y * (g / (1.0 + jnp.exp(-g)))).astype(BF16)


def _retention_tables(lg, reverse):
    c = R_CHUNK
    t = np.arange(c)
    diff = (t[None, :] - t[:, None]) if reverse else (t[:, None] - t[None, :])
    keep = (diff > 0) if reverse else (diff >= 0)
    decay = jnp.where(jnp.asarray(keep)[None], jnp.exp(lg[:, None, None] * np.maximum(diff, 0).astype(np.float32)), 0.0)
    tf = t.astype(np.float32)
    qpow = (c - tf) if reverse else (tf + 1.0)
    kpow = tf if reverse else (c - 1.0 - tf)
    qdec = jnp.repeat(jnp.exp(lg[None, :] * qpow[:, None]), R_QK_DIM, axis=1)
    kdec = jnp.repeat(jnp.exp(lg[None, :] * kpow[:, None]), R_QK_DIM, axis=1)
    rdec = jnp.broadcast_to(jnp.repeat(jnp.exp(lg * c), R_QK_DIM)[:, None], (R_QK_W, R_W))
    return decay.reshape(R_HEADS * c, c), qdec, kdec, rdec


def _retention(pb, cos, sin, lg_f, lg_b):
    b, s, _ = pb.shape
    rows = R_CHUNK * R_STEP_CHUNKS
    assert s % rows == 0
    nt = s // rows
    hd = np.arange(R_QK_W)[:, None] // R_QK_DIM == np.arange(R_W)[None, :] // R_V_DIM
    mask = jnp.asarray(hd.astype(np.float32))
    gh = np.arange(R_W)[:, None] // R_V_DIM == np.arange(R_W)[None, :] // R_V_DIM
    avg = jnp.asarray(gh.astype(np.float32) / R_V_DIM, dtype=BF16)
    const = lambda shape: pl.BlockSpec(shape, lambda bi, i: (0,) * len(shape))

    def specs(order):
        seq = lambda w, cb: pl.BlockSpec((1, rows, w), lambda bi, i: (bi, order(i), cb))
        tab = pl.BlockSpec((rows, R_QK_W), lambda bi, i: (order(i), 0))
        return seq, [seq(R_QK_W, 0), seq(R_QK_W, 1), seq(R_W, 1), tab, tab,
                     const((R_HEADS * R_CHUNK, R_CHUNK)), const((R_CHUNK, R_QK_W)), const((R_CHUNK, R_QK_W)),
                     const((R_QK_W, R_W)), const((R_QK_W, R_W))]

    seq, in_specs = specs(lambda i: i)
    o_fwd = pl.pallas_call(
        functools.partial(_retention_fwd_kernel, n_chunks=R_STEP_CHUNKS),
        grid=(b, nt),
        in_specs=in_specs,
        out_specs=seq(R_W, 0),
        out_shape=jax.ShapeDtypeStruct((b, s, R_W), F32),
        scratch_shapes=[pltpu.VMEM((R_QK_W, R_W), F32)],
        compiler_params=_cparams("parallel", "arbitrary"),
        name="retention_fwd",
    )(pb, pb, pb, cos, sin, *_retention_tables(lg_f, False), mask)

    seq, in_specs = specs(lambda i: nt - 1 - i)
    out = pl.pallas_call(
        functools.partial(_retention_bwd_kernel, n_chunks=R_STEP_CHUNKS),
        grid=(b, nt),
        in_specs=in_specs + [seq(R_W, 0), seq(R_W, 2), const((R_W, R_W))],
        out_specs=seq(R_W, 0),
        out_shape=jax.ShapeDtypeStruct((b, s, R_W), BF16),
        scratch_shapes=[pltpu.VMEM((R_QK_W, R_W), F32)],
        compiler_params=_cparams("parallel", "arbitrary"),
        name="retention_bwd",
    )(pb, pb, pb, cos, sin, *_retention_tables(lg_b, True), mask, o_fwd, pb, avg)
    return out.reshape(b * s, R_W)


def _rope_tables(s):
    half = R_QK_DIM // 2
    freqs = ROPE_BASE ** (-jnp.arange(half, dtype=F32) / half)
    ang = jnp.arange(s, dtype=F32)[:, None] * freqs[None]
    cos, sin = jnp.cos(ang), jnp.sin(ang)
    return (jnp.tile(jnp.concatenate([cos, cos], axis=1), (1, R_HEADS)),
            jnp.tile(jnp.concatenate([-sin, sin], axis=1), (1, R_HEADS)))


def _natural_order(o_ref, lse_ref, o_stage, lse_stage, d):
    if d == 1:
        return o_ref[...].astype(F32), lse_ref[...]
    n = o_ref.shape[0]
    for r in range(d):
        lse_stage[pl.ds(r, n, stride=d), :] = lse_ref[:, r * LANES:(r + 1) * LANES]
        for c in range(A_W // LANES):
            col = r * A_W + c * LANES
            o_stage[c, pl.ds(r, n, stride=d), :] = o_ref[:, col:col + LANES].astype(F32)
    return jnp.concatenate([o_stage[c] for c in range(A_W // LANES)], axis=1), lse_stage[...]


def _merge_out_kernel(*refs):
    n_a = len(A_DILATIONS)
    o_refs, lse_refs = refs[:n_a], refs[n_a:2 * n_a]
    r_ref, c_ref, x_ref, w_ref, g_ref, expand_ref, x1_ref, h2_ref = refs[2 * n_a:2 * n_a + 8]
    stages = refs[2 * n_a + 8:]
    outs, lses = [], []
    for b, d in enumerate(A_DILATIONS):
        o, lse = _natural_order(o_refs[b], lse_refs[b], stages[2 * b], stages[2 * b + 1], d)
        outs.append(o)
        lses.append(lse)
    top = functools.reduce(jnp.maximum, lses)
    ws = [jnp.exp2(l - top) for l in lses]
    den = functools.reduce(lambda u, v: u + v, ws)
    expand = expand_ref[...]
    a = None
    for wgt, o in zip(ws, outs):
        term = _split_dot(wgt / den, expand) * o
        a = term if a is None else a + term
    a = a.astype(BF16)
    r = r_ref[...]
    c = c_ref[...]
    nc = 512
    for n0 in range(0, D_MODEL, nc):
        cols = slice(n0, n0 + nc)
        y = jnp.dot(a, w_ref[0:A_W, cols], preferred_element_type=F32)
        y = y + jnp.dot(r, w_ref[A_W:A_W + R_W, cols], preferred_element_type=F32)
        y = y + jnp.dot(c, w_ref[A_W + R_W:MIX_W, cols], preferred_element_type=F32)
        x1_ref[:, cols] = x_ref[:, cols] + y
    x1 = x1_ref[...]
    h2 = x1 * lax.rsqrt(jnp.mean(x1 * x1, axis=-1, keepdims=True) + EPS) * g_ref[...]
    h2_ref[...] = h2.astype(BF16)


def _merge_out(branches, r, c, x, w, g, tm=512):
    t = x.shape[0]
    assert t % tm == 0 and all(tm % (16 * d) == 0 for d in A_DILATIONS)
    expand = jnp.asarray((np.arange(LANES)[:, None] == np.arange(A_W)[None, :] // HEAD_DIM).astype(np.float32), dtype=BF16)
    row = lambda w_: pl.BlockSpec((tm, w_), lambda i: (i, 0))
    dil_row = lambda w_, d: pl.BlockSpec((tm // d, d * w_), lambda i: (i, 0))
    stages = []
    for _ in A_DILATIONS:
        stages += [pltpu.VMEM((A_W // LANES, tm, LANES), F32), pltpu.VMEM((tm, LANES), F32)]
    return pl.pallas_call(
        _merge_out_kernel,
        grid=(t // tm,),
        in_specs=[dil_row(A_W, d) for d in A_DILATIONS] + [dil_row(LANES, d) for d in A_DILATIONS] + [
            row(R_W), row(C_W), row(D_MODEL),
            pl.BlockSpec((MIX_W, D_MODEL), lambda i: (0, 0), pipeline_mode=pl.Buffered(1)),
            pl.BlockSpec((1, D_MODEL), lambda i: (0, 0)),
            pl.BlockSpec((LANES, A_W), lambda i: (0, 0))],
        out_specs=[row(D_MODEL), row(D_MODEL)],
        out_shape=[jax.ShapeDtypeStruct((t, D_MODEL), F32), jax.ShapeDtypeStruct((t, D_MODEL), BF16)],
        scratch_shapes=stages,
        compiler_params=_cparams("parallel"),
        name="merge_out_proj",
    )(*[o for o, _ in branches], *[l for _, l in branches], r, c, x, w, g.reshape(1, D_MODEL), expand)


def _ffn_accumulate(h_ref, x_ref, wg_ref, wu_ref, wd_ref, o_ref):
    @pl.when(pl.program_id(1) == 0)
    def _():
        o_ref[...] = x_ref[...]

    h = h_ref[...]
    tf = wg_ref.shape[1]
    halves = [slice(c, c + tf // 2) for c in (0, tf // 2)]
    gs = [jnp.dot(h, wg_ref[:, c], preferred_element_type=F32) for c in halves]
    us = [jnp.dot(h, wu_ref[:, c], preferred_element_type=F32) for c in halves]
    y = None
    for g, u, c in zip(gs, us, halves):
        a = (g / (1.0 + jnp.exp(-g)) * u).astype(BF16)
        part = jnp.dot(a, wd_ref[c, :], preferred_element_type=F32)
        y = part if y is None else y + part
    o_ref[...] += y


def _ffn_kernel(h_ref, x_ref, wg_ref, wu_ref, wd_ref, o_ref):
    _ffn_accumulate(h_ref, x_ref, wg_ref, wu_ref, wd_ref, o_ref)


def _ffn_norm_kernel(h_ref, x_ref, wg_ref, wu_ref, wd_ref, g_ref, o_ref):
    _ffn_accumulate(h_ref, x_ref, wg_ref, wu_ref, wd_ref, o_ref)

    @pl.when(pl.program_id(1) == pl.num_programs(1) - 1)
    def _():
        x = o_ref[...]
        o_ref[...] = x * lax.rsqrt(jnp.mean(x * x, axis=-1, keepdims=True) + EPS) * g_ref[...]


def _ffn(h, x, wg, wu, wd, out_norm_g=None, tm=1024, tf=512):
    t = x.shape[0]
    assert t % tm == 0 and D_FF % tf == 0
    in_specs = [
        pl.BlockSpec((tm, D_MODEL), lambda i, k: (i, 0)),
        pl.BlockSpec((tm, D_MODEL), lambda i, k: (i, 0)),
        pl.BlockSpec((D_MODEL, tf), lambda i, k: (0, k)),
        pl.BlockSpec((D_MODEL, tf), lambda i, k: (0, k)),
        pl.BlockSpec((tf, D_MODEL), lambda i, k: (k, 0)),
    ]
    args = [h, x, wg, wu, wd]
    body = _ffn_kernel
    if out_norm_g is not None:
        in_specs.append(pl.BlockSpec((1, D_MODEL), lambda i, k: (0, 0)))
        args.append(out_norm_g.reshape(1, D_MODEL))
        body = _ffn_norm_kernel
    return pl.pallas_call(
        body,
        grid=(t // tm, D_FF // tf),
        in_specs=in_specs,
        out_specs=pl.BlockSpec((tm, D_MODEL), lambda i, k: (i, 0)),
        out_shape=jax.ShapeDtypeStruct((t, D_MODEL), F32),
        compiler_params=_cparams("parallel", "arbitrary"),
        name="swiglu_ffn",
    )(*args)


def _layer(x, b, s, p, out_norm_g):
    *pas, pb, pc = _norm_proj(x, p["g1"], p["w_in"])
    branches = [_dilated_branch(pa.reshape(b, s // d, d * PA_W), bias, d)
                for pa, bias, d in zip(pas, p["bias_a"], A_DILATIONS)]
    r = _retention(pb.reshape(b, s, PB_W), p["cos"], p["sin"], p["lg_f"], p["lg_b"])
    c = _window_attn(pc.reshape(b, s, PC_W), p["bias_c"], p["sink"])
    x1, h2 = _merge_out(branches, r, c, x, p["w_out"], p["g2"])
    return _ffn(h2, x1, p["w_gate"], p["w_up"], p["w_down"], out_norm_g)


def kernel(x_prompt, x_sample, rel_bias, norm1_g, w_in, ret_decay_fwd, ret_decay_bwd, attn_sink, w_out, norm2_g,
           w_gate, w_up, w_down, final_norm_g):
    bias_a = [_pair_heads(_band_bias(rel_bias[:, :A_HEADS], A_TQ, A_TK, A_RADIUS, A_RADIUS, dil))
              for _, dil in A_BRANCHES]
    q_order = np.asarray(C_Q_ORDER)
    bias_c = _pair_heads(_band_bias(rel_bias[:, A_HEADS:][:, q_order], C_TQ, C_TK, C_TQ, C_RADIUS, 1))

    def reorder_c_heads(w, axis, start):
        take = lambda a, n: lax.slice_in_dim(w, a, a + n, axis=axis)
        heads = [take(start + h * HEAD_DIM, HEAD_DIM) for h in C_Q_ORDER]
        tail = start + C_W
        return jnp.concatenate([take(0, start)] + heads + [take(tail, w.shape[axis] - tail)], axis=axis)

    layers = []
    for i in range(DEPTH):
        layers.append(dict(
            g1=norm1_g[i], g2=norm2_g[i], sink=attn_sink[i].astype(F32)[q_order] * LOG2E, bias_a=bias_a, bias_c=bias_c,
            w_in=reorder_c_heads(w_in[i].astype(BF16), 1, PA_W + PB_W),
            w_out=reorder_c_heads(w_out[i].astype(BF16), 0, A_W + R_W),
            w_gate=w_gate[i].astype(BF16), w_up=w_up[i].astype(BF16), w_down=w_down[i].astype(BF16),
            lg_f=jnp.log1p(-jnp.exp2(-ret_decay_fwd[i].astype(F32))),
            lg_b=jnp.log1p(-jnp.exp2(-ret_decay_bwd[i].astype(F32))),
        ))

    def trunk(x):
        b, s, _ = x.shape
        cos, sin = _rope_tables(s)
        x = x.reshape(b * s, D_MODEL)
        for i, p in enumerate(layers):
            x = _layer(x, b, s, dict(p, cos=cos, sin=sin), final_norm_g if i == DEPTH - 1 else None)
        return x.reshape(b, s, D_MODEL)

    return trunk(x_prompt), trunk(x_sample)
```

```python
import functools
import math

import numpy as np
import jax
import jax.numpy as jnp
from jax import lax
from jax.experimental import pallas as pl
from jax.experimental.pallas import tpu as pltpu

D_MODEL = 2048
DEPTH = 2
HEAD_DIM = 64
A_HEADS = 12
A_BRANCHES = ((128, 1), (512, 4), (2048, 16))
A_RADIUS = 64
R_HEADS = 8
R_QK_DIM = 32
R_V_DIM = 64
R_CHUNK = 128
ROPE_BASE = 10000.0
C_HEADS = 12
C_KV_HEADS = 4
C_GROUP = C_HEADS // C_KV_HEADS
C_RADIUS = 128
REL_BUCKETS = 32
REL_MAX_DIST = 1024
D_FF = 5632
EPS = 1e-6
GN_EPS = 1e-5
NEG = -1e30

A_W = A_HEADS * HEAD_DIM
R_QK_W = R_HEADS * R_QK_DIM
R_W = R_HEADS * R_V_DIM
C_W = C_HEADS * HEAD_DIM
C_KV_W = C_KV_HEADS * HEAD_DIM
PA_W = 3 * A_W
PB_W = 2 * R_QK_W + 2 * R_W
PC_W = C_W + 2 * C_KV_W
IN_COLS = PA_W + PB_W + PC_W
MIX_W = A_W + R_W + C_W

LANES = 128
V7X_VMEM_BYTES = 64 * 1024 * 1024
VMEM_LIMIT = V7X_VMEM_BYTES - 4 * 1024 * 1024

A_TQ = 128
A_TK = A_TQ + 2 * A_RADIUS
A_SUB = 4
C_TQ = 128
C_TK = 3 * C_TQ
C_SUB = 4
C_Q_ORDER = tuple(kp * C_GROUP + j + e * C_GROUP
                  for kp in range(0, C_KV_HEADS, 2) for j in range(C_GROUP) for e in range(2))
R_STEP_CHUNKS = 4

LOG2E = 1.4426950408889634
QSCALE = HEAD_DIM ** -0.5 * LOG2E

BF16 = jnp.bfloat16
F32 = jnp.float32


def _cparams(*sem):
    return pltpu.CompilerParams(dimension_semantics=sem, vmem_limit_bytes=VMEM_LIMIT)


def _t5_bucket(rel):
    nb = REL_BUCKETS // 2
    max_exact = nb // 2
    ret = np.where(rel > 0, nb, 0)
    n = np.abs(rel)
    nf = np.maximum(n, 1).astype(np.float32)
    large = max_exact + (np.log(nf / max_exact) / math.log(REL_MAX_DIST / max_exact) * (nb - max_exact)).astype(np.int32)
    large = np.minimum(large, nb - 1)
    return (ret + np.where(n < max_exact, n, large)).astype(np.int32)


def _band_bias(table, tq, tk, lead, radius, dil):
    h = table.shape[1]
    n = tq + tk - 1
    offs = np.arange(n) - (tq - 1) - lead
    per_off = jnp.take(table.astype(F32), jnp.asarray(_t5_bucket(offs * dil)), axis=0).T * LOG2E
    padded = jnp.concatenate([per_off, jnp.zeros((h, 1), F32)], axis=1)
    toep = jnp.tile(padded, (1, tq))[:, :tq * n].reshape(h, tq, n)[:, :, tq - 1:tq - 1 + tk]
    off = np.arange(tk)[None, :] - lead - np.arange(tq)[:, None]
    band = np.abs(off) <= radius
    col = np.arange(tk)[None, :]
    first = band & (col >= lead)
    last = band & (col < tk - lead)
    masks = np.stack([band, first, last, first & last])
    return jnp.where(jnp.asarray(masks)[:, None], toep[None], NEG).swapaxes(-1, -2)


def _pair_heads(bias):
    v, h, tk, tq = bias.shape
    return bias.reshape(v, h // 2, 2, tk, tq).swapaxes(2, 3).reshape(v, h // 2, tk, 2 * tq)


_PROJ_CHUNKS = (
    (0, 0, 0, 768, QSCALE), (0, 768, 768, 768, None), (0, 1536, 1536, 768, None),
    (1, 0, 2304, 512, None), (1, 512, 2816, 512, None), (1, 1024, 3328, 512, None),
    (2, 0, 3840, 768, QSCALE), (2, 768, 4608, 512, None),
)


A_DILATIONS = tuple(d for _, d in A_BRANCHES)


def _norm_proj_kernel(x_ref, g_ref, w_ref, *refs):
    n_a = len(A_DILATIONS)
    pa_refs, (pb_ref, pc_ref), stage_refs = refs[:n_a], refs[n_a:n_a + 2], refs[n_a + 2:]
    tm = x_ref.shape[0]
    x = x_ref[...]
    h = (x * lax.rsqrt(jnp.mean(x * x, axis=-1, keepdims=True) + EPS) * g_ref[...]).astype(BF16)
    outs = (None, pb_ref, pc_ref)
    a_chunk = 0
    for oi, oc, wc, width, scale in _PROJ_CHUNKS:
        y = jnp.dot(h, w_ref[:, wc:wc + width], preferred_element_type=F32)
        if scale is not None:
            y = y * scale
        if oi != 0:
            outs[oi][:, oc:oc + width] = y.astype(BF16)
            continue
        stage, stage_mid = stage_refs[2 * a_chunk], stage_refs[2 * a_chunk + 1]
        a_chunk += 1
        _, d1, d2 = A_DILATIONS
        slabs = range(width // LANES)
        for c in slabs:
            stage[c] = y[:, c * LANES:(c + 1) * LANES]
        pa_refs[0][:, oc:oc + width] = y.astype(BF16)
        for r in range(d1):
            for c in slabs:
                part = stage[c, pl.ds(r, tm // d1, stride=d1), :]
                stage_mid[c, r] = part
                col = r * PA_W + oc + c * LANES
                pa_refs[1][:, col:col + LANES] = part.astype(BF16)
        for r in range(d1):
            for a in range(d2 // d1):
                for c in slabs:
                    col = (a * d1 + r) * PA_W + oc + c * LANES
                    part = stage_mid[c, r, pl.ds(a, tm // d2, stride=d2 // d1), :]
                    pa_refs[2][:, col:col + LANES] = part.astype(BF16)


def _norm_proj(x, g, w, tm=512):
    t = x.shape[0]
    assert t % tm == 0 and all(tm % (16 * d) == 0 for d in A_DILATIONS)
    d0, d1, d2 = A_DILATIONS
    assert d0 == 1 and d2 % d1 == 0
    n_a_chunks = sum(1 for c in _PROJ_CHUNKS if c[0] == 0)
    stages = []
    for _ in range(n_a_chunks):
        stages += [pltpu.VMEM((A_W // LANES, tm, LANES), F32), pltpu.VMEM((A_W // LANES, d1, tm // d1, LANES), F32)]
    return pl.pallas_call(
        _norm_proj_kernel,
        grid=(t // tm,),
        in_specs=[
            pl.BlockSpec((tm, D_MODEL), lambda i: (i, 0)),
            pl.BlockSpec((1, D_MODEL), lambda i: (0, 0)),
            pl.BlockSpec((D_MODEL, IN_COLS), lambda i: (0, 0), pipeline_mode=pl.Buffered(1)),
        ],
        out_specs=[pl.BlockSpec((tm // d, d * PA_W), lambda i: (i, 0)) for d in A_DILATIONS] + [
            pl.BlockSpec((tm, PB_W), lambda i: (i, 0)),
            pl.BlockSpec((tm, PC_W), lambda i: (i, 0)),
        ],
        out_shape=[jax.ShapeDtypeStruct((t // d, d * PA_W), BF16) for d in A_DILATIONS] + [
            jax.ShapeDtypeStruct((t, PB_W), BF16),
            jax.ShapeDtypeStruct((t, PC_W), BF16),
        ],
        scratch_shapes=stages,
        compiler_params=_cparams("parallel"),
        name="norm_in_proj",
    )(x, g.reshape(1, D_MODEL), w)


def _nt_dot(a, b):
    return lax.dot_general(a, b, (((1,), (1,)), ((), ())), preferred_element_type=F32)


def _masked_heads(qg):
    head = lax.broadcasted_iota(jnp.int32, qg.shape, 1) // HEAD_DIM
    zero = jnp.zeros_like(qg)
    return jnp.concatenate([jnp.where(head == e, qg, zero) for e in range(qg.shape[1] // HEAD_DIM)], axis=0)


A_GROUP = 4


def _dilated_attn_kernel(q_ref, kp_ref, kc_ref, kn_ref, vp_ref, vc_ref, vn_ref, *refs):
    bias_refs = refs[:A_SUB]
    o_ref, lse_ref, s_even_ref, s_odd_ref = refs[A_SUB:]
    t = pl.program_id(0)
    n_groups = A_HEADS // A_GROUP

    @pl.when(t == 0)
    def _():
        s_odd_ref[...] = jnp.zeros_like(s_odd_ref)

    def step(s_write_ref, s_read_ref):
        kw = jnp.concatenate([kp_ref[0], kc_ref[0], kn_ref[0]], axis=0)
        for sub in range(A_SUB):
            rows = slice(sub * A_TQ, (sub + 1) * A_TQ)
            keys = slice(sub * A_TQ, sub * A_TQ + A_TK)
            for j in range(n_groups):
                lanes = slice(j * A_GROUP * HEAD_DIM, (j + 1) * A_GROUP * HEAD_DIM)
                s_write_ref[sub * n_groups + j] = _nt_dot(kw[keys, lanes], _masked_heads(q_ref[0, rows, lanes]))

        vt = jnp.concatenate([vp_ref[0], vc_ref[0], vn_ref[0]], axis=0).T
        ones = jnp.ones((16, A_TK), BF16)
        for sub, bias_ref in enumerate(bias_refs):
            rows = slice(sub * A_TQ, (sub + 1) * A_TQ)
            keys = slice(sub * A_TQ, sub * A_TQ + A_TK)
            lses = []
            for j in range(n_groups):
                pairs = range(j * A_GROUP // 2, (j + 1) * A_GROUP // 2)
                s = s_read_ref[sub * n_groups + j] + jnp.concatenate([bias_ref[pr] for pr in pairs], axis=1)
                m = jnp.max(s, axis=0, keepdims=True)
                pb = jnp.exp2(s - m).astype(BF16)
                o_t = []
                for e in range(A_GROUP):
                    h = A_GROUP * j + e
                    cols = slice(e * A_TQ, (e + 1) * A_TQ)
                    va = jnp.concatenate([vt[h * HEAD_DIM:(h + 1) * HEAD_DIM, keys], ones], axis=0)
                    o = jnp.dot(va, pb[:, cols], preferred_element_type=F32)
                    l = o[HEAD_DIM:HEAD_DIM + 1]
                    o_t.append(o[:HEAD_DIM] / l)
                    lses.append(m[:, cols] + jnp.log2(l))
                for i, pr in enumerate(pairs):
                    tile = jnp.concatenate(o_t[2 * i:2 * i + 2], axis=0)
                    o_ref[0, rows, pr * LANES:(pr + 1) * LANES] = tile.T.astype(BF16)
            lse_t = jnp.concatenate(lses + [jnp.zeros((LANES - A_HEADS, A_TQ), F32)], axis=0)
            lse_ref[0, rows, :] = lse_t.T

    pl.when(t % 2 == 0)(functools.partial(step, s_even_ref, s_odd_ref))
    pl.when(t % 2 == 1)(functools.partial(step, s_odd_ref, s_even_ref))


def _dilated_branch(pav, bias, dil):
    b, l, _ = pav.shape
    step = A_SUB * A_TQ
    assert A_SUB >= 2 and l % step == 0
    nt = l // step
    n_tiles = b * dil * nt
    halo = A_RADIUS
    hb = step // halo
    nblk = PA_W // A_W

    def decode(tt):
        return tt // (dil * nt), (tt // nt) % dil, tt % nt

    def scores_tile(t):
        return decode(jnp.minimum(t, n_tiles - 1))

    def finish_tile(t):
        return decode(jnp.maximum(t - 1, 0))

    def cur(tile, c):
        def index(t):
            bi, r, i = tile(t)
            return bi, i, r * nblk + c
        return pl.BlockSpec((1, step, A_W), index)

    def halo_spec(tile, c, side):
        def index(t):
            bi, r, i = tile(t)
            blk = jnp.maximum(i * hb - 1, 0) if side < 0 else jnp.minimum((i + 1) * hb, nt * hb - 1)
            return bi, blk, r * nblk + c
        return pl.BlockSpec((1, halo, A_W), index)

    def bias_spec(variant):
        def index(t):
            _, _, i = finish_tile(t)
            return variant(i), 0, 0, 0
        return pl.BlockSpec((None, A_HEADS // 2, A_TK, 2 * A_TQ), index)

    def sub_variant(sub):
        def variant(i):
            first = (i == 0).astype(jnp.int32) if sub == 0 else 0
            last = 2 * (i == nt - 1).astype(jnp.int32) if sub == A_SUB - 1 else 0
            return first + last
        return variant

    def out_spec(w):
        def index(t):
            bi, r, i = finish_tile(t)
            return bi, i, r
        return pl.BlockSpec((1, step, w), index)

    n_scores = A_SUB * (A_HEADS // A_GROUP)
    o, lse = pl.pallas_call(
        _dilated_attn_kernel,
        grid=(n_tiles + 1,),
        in_specs=[
            cur(scores_tile, 0),
            halo_spec(scores_tile, 1, -1), cur(scores_tile, 1), halo_spec(scores_tile, 1, 1),
            halo_spec(finish_tile, 2, -1), cur(finish_tile, 2), halo_spec(finish_tile, 2, 1),
        ] + [bias_spec(sub_variant(sub)) for sub in range(A_SUB)],
        out_specs=[out_spec(A_W), out_spec(LANES)],
        out_shape=[
            jax.ShapeDtypeStruct((b, l, dil * A_W), BF16),
            jax.ShapeDtypeStruct((b, l, dil * LANES), F32),
        ],
        scratch_shapes=[pltpu.VMEM((n_scores, A_TK, A_GROUP * A_TQ), F32) for _ in range(2)],
        compiler_params=_cparams("arbitrary"),
        name=f"dilated_attn_d{dil}",
    )(*([pav] * 7 + [bias] * A_SUB))
    return o.reshape(b * l, dil * A_W), lse.reshape(b * l, dil * LANES)


def _window_attn_kernel(sink_ref, q_ref, kp_ref, kc_ref, kn_ref, vp_ref, vc_ref, vn_ref, *refs):
    bias_refs = refs[:C_SUB]
    o_ref, s_even_ref, s_odd_ref = refs[C_SUB:]
    t = pl.program_id(0)

    @pl.when(t == 0)
    def _():
        s_odd_ref[...] = jnp.zeros_like(s_odd_ref)

    def step(s_write_ref, s_read_ref):
        kw = jnp.concatenate([kp_ref[0], kc_ref[0], kn_ref[0]], axis=0)
        for sub in range(C_SUB):
            rows = slice(sub * C_TQ, (sub + 1) * C_TQ)
            keys = slice(sub * C_TQ, sub * C_TQ + C_TK)
            for g in range(C_GROUP):
                qg = jnp.concatenate(
                    [q_ref[0, rows, pr * LANES:(pr + 1) * LANES] for pr in (g, g + C_GROUP)], axis=1)
                s_write_ref[sub * C_GROUP + g] = _nt_dot(kw[keys], _masked_heads(qg))

        vt = jnp.concatenate([vp_ref[0], vc_ref[0], vn_ref[0]], axis=0).T
        ones = jnp.ones((16, C_TK), BF16)
        for sub, bias_ref in enumerate(bias_refs):
            rows = slice(sub * C_TQ, (sub + 1) * C_TQ)
            keys = slice(sub * C_TQ, sub * C_TQ + C_TK)
            for g in range(C_GROUP):
                pairs = (g, g + C_GROUP)
                s = s_read_ref[sub * C_GROUP + g] + jnp.concatenate([bias_ref[pr] for pr in pairs], axis=1)
                sk = jnp.concatenate(
                    [jnp.full((1, C_TQ), sink_ref[2 * pr + e], F32) for pr in pairs for e in range(2)], axis=1)
                m = jnp.maximum(jnp.max(s, axis=0, keepdims=True), sk)
                pb = jnp.exp2(s - m).astype(BF16)
                sink_p = jnp.exp2(sk - m)
                o_t = []
                for kv in range(C_KV_HEADS):
                    cols = slice(kv * C_TQ, (kv + 1) * C_TQ)
                    va = jnp.concatenate([vt[kv * HEAD_DIM:(kv + 1) * HEAD_DIM, keys], ones], axis=0)
                    o = jnp.dot(va, pb[:, cols], preferred_element_type=F32)
                    den = o[HEAD_DIM:HEAD_DIM + 1] + sink_p[:, cols]
                    o_t.append(o[:HEAD_DIM] / den)
                for i, pr in enumerate(pairs):
                    tile = jnp.concatenate(o_t[2 * i:2 * i + 2], axis=0)
                    o_ref[0, rows, pr * LANES:(pr + 1) * LANES] = tile.T.astype(BF16)

    pl.when(t % 2 == 0)(functools.partial(step, s_even_ref, s_odd_ref))
    pl.when(t % 2 == 1)(functools.partial(step, s_odd_ref, s_even_ref))


def _window_attn(pc, bias, sink):
    b, s, _ = pc.shape
    step = C_SUB * C_TQ
    assert C_SUB >= 2 and s % step == 0
    nt = s // step
    n_tiles = b * nt
    kblk = C_W // C_KV_W

    def scores_tile(t):
        tt = jnp.minimum(t, n_tiles - 1)
        return tt // nt, tt % nt

    def finish_tile(t):
        tt = jnp.maximum(t - 1, 0)
        return tt // nt, tt % nt

    def cur(tile, w, c):
        def index(t):
            bi, i = tile(t)
            return bi, i, c
        return pl.BlockSpec((1, step, w), index)

    def halo_spec(tile, c, side):
        def index(t):
            bi, i = tile(t)
            blk = jnp.maximum(i * C_SUB - 1, 0) if side < 0 else jnp.minimum((i + 1) * C_SUB, nt * C_SUB - 1)
            return bi, blk, c
        return pl.BlockSpec((1, C_TQ, C_KV_W), index)

    def bias_spec(sub):
        def index(t):
            _, i = finish_tile(t)
            first = (i == 0).astype(jnp.int32) if sub == 0 else 0
            last = 2 * (i == nt - 1).astype(jnp.int32) if sub == C_SUB - 1 else 0
            return first + last, 0, 0, 0
        return pl.BlockSpec((None, C_HEADS // 2, C_TK, 2 * C_TQ), index)

    o = pl.pallas_call(
        _window_attn_kernel,
        grid=(n_tiles + 1,),
        in_specs=[
            pl.BlockSpec(memory_space=pltpu.SMEM),
            cur(scores_tile, C_W, 0),
            halo_spec(scores_tile, kblk, -1), cur(scores_tile, C_KV_W, kblk), halo_spec(scores_tile, kblk, 1),
            halo_spec(finish_tile, kblk + 1, -1), cur(finish_tile, C_KV_W, kblk + 1),
            halo_spec(finish_tile, kblk + 1, 1),
        ] + [bias_spec(sub) for sub in range(C_SUB)],
        out_specs=cur(finish_tile, C_W, 0),
        out_shape=jax.ShapeDtypeStruct((b, s, C_W), BF16),
        scratch_shapes=[pltpu.VMEM((C_SUB * C_GROUP, C_TK, C_KV_HEADS * C_TQ), F32) for _ in range(2)],
        compiler_params=_cparams("arbitrary"),
        name="window_gqa",
    )(sink.astype(F32), *([pc] * 7 + [bias] * C_SUB))
    return o.reshape(b * s, C_W)


def _rope(x, cos, sin, lane_lo):
    parts = []
    for a in range(0, R_QK_W, LANES):
        xh = x[:, a:a + LANES]
        half = R_QK_DIM // 2
        parts.append(jnp.where(lane_lo, pltpu.roll(xh, LANES - half, 1), pltpu.roll(xh, half, 1)))
    return x * cos + jnp.concatenate(parts, axis=1) * sin


def _retention_chunk(q, k, v, cos, sin, decay, qdec, kdec, rdec, mask, state_ref):
    c = R_CHUNK
    lane = lax.broadcasted_iota(jnp.int32, (c, LANES), 1)
    lane_lo = (lane % R_QK_DIM) < (R_QK_DIM // 2)
    q = _rope(q.astype(F32), cos, sin, lane_lo)
    k = _rope(k.astype(F32), cos, sin, lane_lo) * (R_QK_DIM ** -0.5)
    state = state_ref[...]
    cross = jnp.dot((q * qdec).astype(BF16), state.astype(BF16), preferred_element_type=F32)

    qb = q.astype(BF16)
    kb = k.astype(BF16)
    head = lax.broadcasted_iota(jnp.int32, (c, R_QK_W), 1) // R_QK_DIM
    qs = jnp.concatenate([jnp.where(head == h, qb, jnp.zeros_like(qb)) for h in range(R_HEADS)], axis=0)
    s = lax.dot_general(qs, kb, (((1,), (1,)), ((), ())), preferred_element_type=F32)
    p = (s * decay).astype(BF16)

    vlane = lax.broadcasted_iota(jnp.int32, (c, R_W), 1) % LANES
    v_lo = jnp.where(vlane < R_V_DIM, v, jnp.zeros_like(v))
    v_hi = jnp.where(vlane >= R_V_DIM, v, jnp.zeros_like(v))
    pieces = []
    for j in range(R_HEADS // 2):
        cs = slice(j * LANES, (j + 1) * LANES)
        a = jnp.dot(p[(2 * j) * c:(2 * j + 1) * c], v_lo[:, cs], preferred_element_type=F32)
        a = a + jnp.dot(p[(2 * j + 1) * c:(2 * j + 2) * c], v_hi[:, cs], preferred_element_type=F32)
        pieces.append(a)
    intra = jnp.concatenate(pieces, axis=1)

    kv = lax.dot_general((k * kdec).astype(BF16), v, (((0,), (0,)), ((), ())), preferred_element_type=F32)
    state_ref[...] = state * rdec + kv * mask
    return cross + intra


def _retention_fwd_kernel(q_ref, k_ref, v_ref, cos_ref, sin_ref, decay_ref, qdec_ref, kdec_ref, rdec_ref,
                          mask_ref, o_ref, state_ref, *, n_chunks):
    @pl.when(pl.program_id(1) == 0)
    def _():
        state_ref[...] = jnp.zeros_like(state_ref)

    for ci in range(n_chunks):
        rows = slice(ci * R_CHUNK, (ci + 1) * R_CHUNK)
        o_ref[0, rows, :] = _retention_chunk(
            q_ref[0, rows, :], k_ref[0, rows, :], v_ref[0, rows, :], cos_ref[rows, :], sin_ref[rows, :],
            decay_ref[...], qdec_ref[...], kdec_ref[...], rdec_ref[...], mask_ref[...], state_ref)


def _split_dot(x, w):
    hi = x.astype(BF16)
    lo = (x - hi.astype(F32)).astype(BF16)
    return jnp.dot(hi, w, preferred_element_type=F32) + jnp.dot(lo, w, preferred_element_type=F32)


def _retention_bwd_kernel(q_ref, k_ref, v_ref, cos_ref, sin_ref, decay_ref, qdec_ref, kdec_ref, rdec_ref,
                          mask_ref, fwd_ref, gate_ref, avg_ref, o_ref, state_ref, *, n_chunks):
    @pl.when(pl.program_id(1) == 0)
    def _():
        state_ref[...] = jnp.zeros_like(state_ref)

    for ci in reversed(range(n_chunks)):
        rows = slice(ci * R_CHUNK, (ci + 1) * R_CHUNK)
        o = fwd_ref[0, rows, :] + _retention_chunk(
            q_ref[0, rows, :], k_ref[0, rows, :], v_ref[0, rows, :], cos_ref[rows, :], sin_ref[rows, :],
            decay_ref[...], qdec_ref[...], kdec_ref[...], rdec_ref[...], mask_ref[...], state_ref)
        avg = avg_ref[...]
        d = o - jnp.dot(o.astype(BF16), avg, preferred_element_type=F32)
        y = d * lax.rsqrt(jnp.dot((d * d).astype(BF16), avg, preferred_element_type=F32) + GN_EPS)
        g = gate_ref[0, rows, :].astype(F32)
        o_ref[0, rows, :] = (y * (g / (1.0 + jnp.exp(-g)))).astype(BF16)


def _retention_tables(lg, reverse):
    c = R_CHUNK
    t = np.arange(c)
    diff = (t[None, :] - t[:, None]) if reverse else (t[:, None] - t[None, :])
    keep = (diff > 0) if reverse else (diff >= 0)
    decay = jnp.where(jnp.asarray(keep)[None], jnp.exp(lg[:, None, None] * np.maximum(diff, 0).astype(np.float32)), 0.0)
    tf = t.astype(np.float32)
    qpow = (c - tf) if reverse else (tf + 1.0)
    kpow = tf if reverse else (c - 1.0 - tf)
    qdec = jnp.repeat(jnp.exp(lg[None, :] * qpow[:, None]), R_QK_DIM, axis=1)
    kdec = jnp.repeat(jnp.exp(lg[None, :] * kpow[:, None]), R_QK_DIM, axis=1)
    rdec = jnp.broadcast_to(jnp.repeat(jnp.exp(lg * c), R_QK_DIM)[:, None], (R_QK_W, R_W))
    return decay.reshape(R_HEADS * c, c), qdec, kdec, rdec


def _retention(pb, cos, sin, lg_f, lg_b):
    b, s, _ = pb.shape
    rows = R_CHUNK * R_STEP_CHUNKS
    assert s % rows == 0
    nt = s // rows
    hd = np.arange(R_QK_W)[:, None] // R_QK_DIM == np.arange(R_W)[None, :] // R_V_DIM
    mask = jnp.asarray(hd.astype(np.float32))
    gh = np.arange(R_W)[:, None] // R_V_DIM == np.arange(R_W)[None, :] // R_V_DIM
    avg = jnp.asarray(gh.astype(np.float32) / R_V_DIM, dtype=BF16)
    const = lambda shape: pl.BlockSpec(shape, lambda bi, i: (0,) * len(shape))

    def specs(order):
        seq = lambda w, cb: pl.BlockSpec((1, rows, w), lambda bi, i: (bi, order(i), cb))
        tab = pl.BlockSpec((rows, R_QK_W), lambda bi, i: (order(i), 0))
        return seq, [seq(R_QK_W, 0), seq(R_QK_W, 1), seq(R_W, 1), tab, tab,
                     const((R_HEADS * R_CHUNK, R_CHUNK)), const((R_CHUNK, R_QK_W)), const((R_CHUNK, R_QK_W)),
                     const((R_QK_W, R_W)), const((R_QK_W, R_W))]

    seq, in_specs = specs(lambda i: i)
    o_fwd = pl.pallas_call(
        functools.partial(_retention_fwd_kernel, n_chunks=R_STEP_CHUNKS),
        grid=(b, nt),
        in_specs=in_specs,
        out_specs=seq(R_W, 0),
        out_shape=jax.ShapeDtypeStruct((b, s, R_W), F32),
        scratch_shapes=[pltpu.VMEM((R_QK_W, R_W), F32)],
        compiler_params=_cparams("parallel", "arbitrary"),
        name="retention_fwd",
    )(pb, pb, pb, cos, sin, *_retention_tables(lg_f, False), mask)

    seq, in_specs = specs(lambda i: nt - 1 - i)
    out = pl.pallas_call(
        functools.partial(_retention_bwd_kernel, n_chunks=R_STEP_CHUNKS),
        grid=(b, nt),
        in_specs=in_specs + [seq(R_W, 0), seq(R_W, 2), const((R_W, R_W))],
        out_specs=seq(R_W, 0),
        out_shape=jax.ShapeDtypeStruct((b, s, R_W), BF16),
        scratch_shapes=[pltpu.VMEM((R_QK_W, R_W), F32)],
        compiler_params=_cparams("parallel", "arbitrary"),
        name="retention_bwd",
    )(pb, pb, pb, cos, sin, *_retention_tables(lg_b, True), mask, o_fwd, pb, avg)
    return out.reshape(b * s, R_W)


def _rope_tables(s):
    half = R_QK_DIM // 2
    freqs = ROPE_BASE ** (-jnp.arange(half, dtype=F32) / half)
    ang = jnp.arange(s, dtype=F32)[:, None] * freqs[None]
    cos, sin = jnp.cos(ang), jnp.sin(ang)
    return (jnp.tile(jnp.concatenate([cos, cos], axis=1), (1, R_HEADS)),
            jnp.tile(jnp.concatenate([-sin, sin], axis=1), (1, R_HEADS)))


def _natural_order(o_ref, lse_ref, o_stage, lse_stage, d):
    if d == 1:
        return o_ref[...].astype(F32), lse_ref[...]
    n = o_ref.shape[0]
    for r in range(d):
        lse_stage[pl.ds(r, n, stride=d), :] = lse_ref[:, r * LANES:(r + 1) * LANES]
        for c in range(A_W // LANES):
            col = r * A_W + c * LANES
            o_stage[c, pl.ds(r, n, stride=d), :] = o_ref[:, col:col + LANES].astype(F32)
    return jnp.concatenate([o_stage[c] for c in range(A_W // LANES)], axis=1), lse_stage[...]


def _merge_out_kernel(*refs):
    n_a = len(A_DILATIONS)
    o_refs, lse_refs = refs[:n_a], refs[n_a:2 * n_a]
    r_ref, c_ref, x_ref, w_ref, g_ref, expand_ref, x1_ref, h2_ref = refs[2 * n_a:2 * n_a + 8]
    stages = refs[2 * n_a + 8:]
    outs, lses = [], []
    for b, d in enumerate(A_DILATIONS):
        o, lse = _natural_order(o_refs[b], lse_refs[b], stages[2 * b], stages[2 * b + 1], d)
        outs.append(o)
        lses.append(lse)
    top = functools.reduce(jnp.maximum, lses)
    ws = [jnp.exp2(l - top) for l in lses]
    den = functools.reduce(lambda u, v: u + v, ws)
    expand = expand_ref[...]
    a = None
    for wgt, o in zip(ws, outs):
        term = _split_dot(wgt / den, expand) * o
        a = term if a is None else a + term
    a = a.astype(BF16)
    r = r_ref[...]
    c = c_ref[...]
    nc = 512
    for n0 in range(0, D_MODEL, nc):
        cols = slice(n0, n0 + nc)
        y = jnp.dot(a, w_ref[0:A_W, cols], preferred_element_type=F32)
        y = y + jnp.dot(r, w_ref[A_W:A_W + R_W, cols], preferred_element_type=F32)
        y = y + jnp.dot(c, w_ref[A_W + R_W:MIX_W, cols], preferred_element_type=F32)
        x1_ref[:, cols] = x_ref[:, cols] + y
    x1 = x1_ref[...]
    h2 = x1 * lax.rsqrt(jnp.mean(x1 * x1, axis=-1, keepdims=True) + EPS) * g_ref[...]
    h2_ref[...] = h2.astype(BF16)


def _merge_out(branches, r, c, x, w, g, tm=512):
    t = x.shape[0]
    assert t % tm == 0 and all(tm % (16 * d) == 0 for d in A_DILATIONS)
    expand = jnp.asarray((np.arange(LANES)[:, None] == np.arange(A_W)[None, :] // HEAD_DIM).astype(np.float32), dtype=BF16)
    row = lambda w_: pl.BlockSpec((tm, w_), lambda i: (i, 0))
    dil_row = lambda w_, d: pl.BlockSpec((tm // d, d * w_), lambda i: (i, 0))
    stages = []
    for _ in A_DILATIONS:
        stages += [pltpu.VMEM((A_W // LANES, tm, LANES), F32), pltpu.VMEM((tm, LANES), F32)]
    return pl.pallas_call(
        _merge_out_kernel,
        grid=(t // tm,),
        in_specs=[dil_row(A_W, d) for d in A_DILATIONS] + [dil_row(LANES, d) for d in A_DILATIONS] + [
            row(R_W), row(C_W), row(D_MODEL),
            pl.BlockSpec((MIX_W, D_MODEL), lambda i: (0, 0), pipeline_mode=pl.Buffered(1)),
            pl.BlockSpec((1, D_MODEL), lambda i: (0, 0)),
            pl.BlockSpec((LANES, A_W), lambda i: (0, 0))],
        out_specs=[row(D_MODEL), row(D_MODEL)],
        out_shape=[jax.ShapeDtypeStruct((t, D_MODEL), F32), jax.ShapeDtypeStruct((t, D_MODEL), BF16)],
        scratch_shapes=stages,
        compiler_params=_cparams("parallel"),
        name="merge_out_proj",
    )(*[o for o, _ in branches], *[l for _, l in branches], r, c, x, w, g.reshape(1, D_MODEL), expand)


def _ffn_accumulate(h_ref, x_ref, wg_ref, wu_ref, wd_ref, o_ref):
    @pl.when(pl.program_id(1) == 0)
    def _():
        o_ref[...] = x_ref[...]

    h = h_ref[...]
    tf = wg_ref.shape[1]
    halves = [slice(c, c + tf // 2) for c in (0, tf // 2)]
    gs = [jnp.dot(h, wg_ref[:, c], preferred_element_type=F32) for c in halves]
    us = [jnp.dot(h, wu_ref[:, c], preferred_element_type=F32) for c in halves]
    y = None
    for g, u, c in zip(gs, us, halves):
        a = (g / (1.0 + jnp.exp(-g)) * u).astype(BF16)
        part = jnp.dot(a, wd_ref[c, :], preferred_element_type=F32)
        y = part if y is None else y + part
    o_ref[...] += y


def _ffn_kernel(h_ref, x_ref, wg_ref, wu_ref, wd_ref, o_ref):
    _ffn_accumulate(h_ref, x_ref, wg_ref, wu_ref, wd_ref, o_ref)


def _ffn_norm_kernel(h_ref, x_ref, wg_ref, wu_ref, wd_ref, g_ref, o_ref):
    _ffn_accumulate(h_ref, x_ref, wg_ref, wu_ref, wd_ref, o_ref)

    @pl.when(pl.program_id(1) == pl.num_programs(1) - 1)
    def _():
        x = o_ref[...]
        o_ref[...] = x * lax.rsqrt(jnp.mean(x * x, axis=-1, keepdims=True) + EPS) * g_ref[...]


def _ffn(h, x, wg, wu, wd, out_norm_g=None, tm=1024, tf=512):
    t = x.shape[0]
    assert t % tm == 0 and D_FF % tf == 0
    in_specs = [
        pl.BlockSpec((tm, D_MODEL), lambda i, k: (i, 0)),
        pl.BlockSpec((tm, D_MODEL), lambda i, k: (i, 0)),
        pl.BlockSpec((D_MODEL, tf), lambda i, k: (0, k)),
        pl.BlockSpec((D_MODEL, tf), lambda i, k: (0, k)),
        pl.BlockSpec((tf, D_MODEL), lambda i, k: (k, 0)),
    ]
    args = [h, x, wg, wu, wd]
    body = _ffn_kernel
    if out_norm_g is not None:
        in_specs.append(pl.BlockSpec((1, D_MODEL), lambda i, k: (0, 0)))
        args.append(out_norm_g.reshape(1, D_MODEL))
        body = _ffn_norm_kernel
    return pl.pallas_call(
        body,
        grid=(t // tm, D_FF // tf),
        in_specs=in_specs,
        out_specs=pl.BlockSpec((tm, D_MODEL), lambda i, k: (i, 0)),
        out_shape=jax.ShapeDtypeStruct((t, D_MODEL), F32),
        compiler_params=_cparams("parallel", "arbitrary"),
        name="swiglu_ffn",
    )(*args)


def _layer(x, b, s, p, out_norm_g):
    *pas, pb, pc = _norm_proj(x, p["g1"], p["w_in"])
    branches = [_dilated_branch(pa.reshape(b, s // d, d * PA_W), bias, d)
                for pa, bias, d in zip(pas, p["bias_a"], A_DILATIONS)]
    r = _retention(pb.reshape(b, s, PB_W), p["cos"], p["sin"], p["lg_f"], p["lg_b"])
    c = _window_attn(pc.reshape(b, s, PC_W), p["bias_c"], p["sink"])
    x1, h2 = _merge_out(branches, r, c, x, p["w_out"], p["g2"])
    return _ffn(h2, x1, p["w_gate"], p["w_up"], p["w_down"], out_norm_g)


def kernel(x_prompt, x_sample, rel_bias, norm1_g, w_in, ret_decay_fwd, ret_decay_bwd, attn_sink, w_out, norm2_g,
           w_gate, w_up, w_down, final_norm_g):
    bias_a = [_pair_heads(_band_bias(rel_bias[:, :A_HEADS], A_TQ, A_TK, A_RADIUS, A_RADIUS, dil))
              for _, dil in A_BRANCHES]
    q_order = np.asarray(C_Q_ORDER)
    bias_c = _pair_heads(_band_bias(rel_bias[:, A_HEADS:][:, q_order], C_TQ, C_TK, C_TQ, C_RADIUS, 1))

    def reorder_c_heads(w, axis, start):
        take = lambda a, n: lax.slice_in_dim(w, a, a + n, axis=axis)
        heads = [take(start + h * HEAD_DIM, HEAD_DIM) for h in C_Q_ORDER]
        tail = start + C_W
        return jnp.concatenate([take(0, start)] + heads + [take(tail, w.shape[axis] - tail)], axis=axis)

    layers = []
    for i in range(DEPTH):
        layers.append(dict(
            g1=norm1_g[i], g2=norm2_g[i], sink=attn_sink[i].astype(F32)[q_order] * LOG2E, bias_a=bias_a, bias_c=bias_c,
            w_in=reorder_c_heads(w_in[i].astype(BF16), 1, PA_W + PB_W),
            w_out=reorder_c_heads(w_out[i].astype(BF16), 0, A_W + R_W),
            w_gate=w_gate[i].astype(BF16), w_up=w_up[i].astype(BF16), w_down=w_down[i].astype(BF16),
            lg_f=jnp.log1p(-jnp.exp2(-ret_decay_fwd[i].astype(F32))),
            lg_b=jnp.log1p(-jnp.exp2(-ret_decay_bwd[i].astype(F32))),
        ))

    def trunk(x):
        b, s, _ = x.shape
        cos, sin = _rope_tables(s)
        x = x.reshape(b * s, D_MODEL)
        for i, p in enumerate(layers):
            x = _layer(x, b, s, dict(p, cos=cos, sin=sin), final_norm_g if i == DEPTH - 1 else None)
        return x.reshape(b, s, D_MODEL)

    return trunk(x_prompt), trunk(x_sample)
```

```python
import functools
import math

import numpy as np
import jax
import jax.numpy as jnp
from jax import lax
from jax.experimental import pallas as pl
from jax.experimental.pallas import tpu as pltpu

D_MODEL = 2048
DEPTH = 2
HEAD_DIM = 64
A_HEADS = 12
A_BRANCHES = ((128, 1), (512, 4), (2048, 16))
A_RADIUS = 64
R_HEADS = 8
R_QK_DIM = 32
R_V_DIM = 64
R_CHUNK = 128
ROPE_BASE = 10000.0
C_HEADS = 12
C_KV_HEADS = 4
C_GROUP = C_HEADS // C_KV_HEADS
C_RADIUS = 128
REL_BUCKETS = 32
REL_MAX_DIST = 1024
D_FF = 5632
EPS = 1e-6
GN_EPS = 1e-5
NEG = -1e30

A_W = A_HEADS * HEAD_DIM
R_QK_W = R_HEADS * R_QK_DIM
R_W = R_HEADS * R_V_DIM
C_W = C_HEADS * HEAD_DIM
C_KV_W = C_KV_HEADS * HEAD_DIM
PA_W = 3 * A_W
PB_W = 2 * R_QK_W + 2 * R_W
PC_W = C_W + 2 * C_KV_W
IN_COLS = PA_W + PB_W + PC_W
MIX_W = A_W + R_W + C_W

LANES = 128
V7X_VMEM_BYTES = 64 * 1024 * 1024
VMEM_LIMIT = V7X_VMEM_BYTES - 4 * 1024 * 1024

A_TQ = 128
A_TK = A_TQ + 2 * A_RADIUS
A_SUB = 4
C_TQ = 128
C_TK = 3 * C_TQ
C_SUB = 4
C_Q_ORDER = tuple(kp * C_GROUP + j + e * C_GROUP
                  for kp in range(0, C_KV_HEADS, 2) for j in range(C_GROUP) for e in range(2))
R_STEP_CHUNKS = 8

LOG2E = 1.4426950408889634
QSCALE = HEAD_DIM ** -0.5 * LOG2E

BF16 = jnp.bfloat16
F32 = jnp.float32


def _cparams(*sem):
    return pltpu.CompilerParams(dimension_semantics=sem, vmem_limit_bytes=VMEM_LIMIT)


def _t5_bucket(rel):
    nb = REL_BUCKETS // 2
    max_exact = nb // 2
    ret = np.where(rel > 0, nb, 0)
    n = np.abs(rel)
    nf = np.maximum(n, 1).astype(np.float32)
    large = max_exact + (np.log(nf / max_exact) / math.log(REL_MAX_DIST / max_exact) * (nb - max_exact)).astype(np.int32)
    large = np.minimum(large, nb - 1)
    return (ret + np.where(n < max_exact, n, large)).astype(np.int32)


def _band_bias(table, tq, tk, lead, radius, dil):
    h = table.shape[1]
    n = tq + tk - 1
    offs = np.arange(n) - (tq - 1) - lead
    per_off = jnp.take(table.astype(F32), jnp.asarray(_t5_bucket(offs * dil)), axis=0).T * LOG2E
    padded = jnp.concatenate([per_off, jnp.zeros((h, 1), F32)], axis=1)
    toep = jnp.tile(padded, (1, tq))[:, :tq * n].reshape(h, tq, n)[:, :, tq - 1:tq - 1 + tk]
    off = np.arange(tk)[None, :] - lead - np.arange(tq)[:, None]
    band = np.abs(off) <= radius
    col = np.arange(tk)[None, :]
    first = band & (col >= lead)
    last = band & (col < tk - lead)
    masks = np.stack([band, first, last, first & last])
    return jnp.where(jnp.asarray(masks)[:, None], toep[None], NEG).swapaxes(-1, -2)


def _pair_heads(bias):
    v, h, tk, tq = bias.shape
    return bias.reshape(v, h // 2, 2, tk, tq).swapaxes(2, 3).reshape(v, h // 2, tk, 2 * tq)


_PROJ_CHUNKS = (
    (0, 0, 0, 768, QSCALE), (0, 768, 768, 768, None), (0, 1536, 1536, 768, None),
    (1, 0, 2304, 512, None), (1, 512, 2816, 512, None), (1, 1024, 3328, 512, None),
    (2, 0, 3840, 768, QSCALE), (2, 768, 4608, 512, None),
)


A_DILATIONS = tuple(d for _, d in A_BRANCHES)


def _norm_proj_kernel(x_ref, g_ref, w_ref, *refs):
    n_a = len(A_DILATIONS)
    pa_refs, (pb_ref, pc_ref), stage_refs = refs[:n_a], refs[n_a:n_a + 2], refs[n_a + 2:]
    tm = x_ref.shape[0]
    x = x_ref[...]
    h = (x * lax.rsqrt(jnp.mean(x * x, axis=-1, keepdims=True) + EPS) * g_ref[...]).astype(BF16)
    outs = (None, pb_ref, pc_ref)
    a_chunk = 0
    for oi, oc, wc, width, scale in _PROJ_CHUNKS:
        y = jnp.dot(h, w_ref[:, wc:wc + width], preferred_element_type=F32)
        if scale is not None:
            y = y * scale
        if oi != 0:
            outs[oi][:, oc:oc + width] = y.astype(BF16)
            continue
        stage, stage_mid = stage_refs[2 * a_chunk], stage_refs[2 * a_chunk + 1]
        a_chunk += 1
        _, d1, d2 = A_DILATIONS
        slabs = range(width // LANES)
        for c in slabs:
            stage[c] = y[:, c * LANES:(c + 1) * LANES]
        pa_refs[0][:, oc:oc + width] = y.astype(BF16)
        for r in range(d1):
            for c in slabs:
                part = stage[c, pl.ds(r, tm // d1, stride=d1), :]
                stage_mid[c, r] = part
                col = r * PA_W + oc + c * LANES
                pa_refs[1][:, col:col + LANES] = part.astype(BF16)
        for r in range(d1):
            for a in range(d2 // d1):
                for c in slabs:
                    col = (a * d1 + r) * PA_W + oc + c * LANES
                    part = stage_mid[c, r, pl.ds(a, tm // d2, stride=d2 // d1), :]
                    pa_refs[2][:, col:col + LANES] = part.astype(BF16)


def _norm_proj(x, g, w, tm=512):
    t = x.shape[0]
    assert t % tm == 0 and all(tm % (16 * d) == 0 for d in A_DILATIONS)
    d0, d1, d2 = A_DILATIONS
    assert d0 == 1 and d2 % d1 == 0
    n_a_chunks = sum(1 for c in _PROJ_CHUNKS if c[0] == 0)
    stages = []
    for _ in range(n_a_chunks):
        stages += [pltpu.VMEM((A_W // LANES, tm, LANES), F32), pltpu.VMEM((A_W // LANES, d1, tm // d1, LANES), F32)]
    return pl.pallas_call(
        _norm_proj_kernel,
        grid=(t // tm,),
        in_specs=[
            pl.BlockSpec((tm, D_MODEL), lambda i: (i, 0)),
            pl.BlockSpec((1, D_MODEL), lambda i: (0, 0)),
            pl.BlockSpec((D_MODEL, IN_COLS), lambda i: (0, 0), pipeline_mode=pl.Buffered(1)),
        ],
        out_specs=[pl.BlockSpec((tm // d, d * PA_W), lambda i: (i, 0)) for d in A_DILATIONS] + [
            pl.BlockSpec((tm, PB_W), lambda i: (i, 0)),
            pl.BlockSpec((tm, PC_W), lambda i: (i, 0)),
        ],
        out_shape=[jax.ShapeDtypeStruct((t // d, d * PA_W), BF16) for d in A_DILATIONS] + [
            jax.ShapeDtypeStruct((t, PB_W), BF16),
            jax.ShapeDtypeStruct((t, PC_W), BF16),
        ],
        scratch_shapes=stages,
        compiler_params=_cparams("parallel"),
        name="norm_in_proj",
    )(x, g.reshape(1, D_MODEL), w)


def _nt_dot(a, b):
    return lax.dot_general(a, b, (((1,), (1,)), ((), ())), preferred_element_type=F32)


def _masked_heads(qg):
    head = lax.broadcasted_iota(jnp.int32, qg.shape, 1) // HEAD_DIM
    zero = jnp.zeros_like(qg)
    return jnp.concatenate([jnp.where(head == e, qg, zero) for e in range(qg.shape[1] // HEAD_DIM)], axis=0)


A_GROUP = 4


def _dilated_attn_kernel(q_ref, kp_ref, kc_ref, kn_ref, vp_ref, vc_ref, vn_ref, *refs):
    bias_refs = refs[:A_SUB]
    o_ref, lse_ref, s_even_ref, s_odd_ref = refs[A_SUB:]
    t = pl.program_id(0)
    n_groups = A_HEADS // A_GROUP

    @pl.when(t == 0)
    def _():
        s_odd_ref[...] = jnp.zeros_like(s_odd_ref)

    def step(s_write_ref, s_read_ref):
        kw = jnp.concatenate([kp_ref[0], kc_ref[0], kn_ref[0]], axis=0)
        for sub in range(A_SUB):
            rows = slice(sub * A_TQ, (sub + 1) * A_TQ)
            keys = slice(sub * A_TQ, sub * A_TQ + A_TK)
            for j in range(n_groups):
                lanes = slice(j * A_GROUP * HEAD_DIM, (j + 1) * A_GROUP * HEAD_DIM)
                s_write_ref[sub * n_groups + j] = _nt_dot(kw[keys, lanes], _masked_heads(q_ref[0, rows, lanes]))

        vt = jnp.concatenate([vp_ref[0], vc_ref[0], vn_ref[0]], axis=0).T
        ones = jnp.ones((16, A_TK), BF16)
        for sub, bias_ref in enumerate(bias_refs):
            rows = slice(sub * A_TQ, (sub + 1) * A_TQ)
            keys = slice(sub * A_TQ, sub * A_TQ + A_TK)
            lses = []
            for j in range(n_groups):
                pairs = range(j * A_GROUP // 2, (j + 1) * A_GROUP // 2)
                s = s_read_ref[sub * n_groups + j] + jnp.concatenate([bias_ref[pr] for pr in pairs], axis=1)
                m = jnp.max(s, axis=0, keepdims=True)
                pb = jnp.exp2(s - m).astype(BF16)
                o_t = []
                for e in range(A_GROUP):
                    h = A_GROUP * j + e
                    cols = slice(e * A_TQ, (e + 1) * A_TQ)
                    va = jnp.concatenate([vt[h * HEAD_DIM:(h + 1) * HEAD_DIM, keys], ones], axis=0)
                    o = jnp.dot(va, pb[:, cols], preferred_element_type=F32)
                    l = o[HEAD_DIM:HEAD_DIM + 1]
                    o_t.append(o[:HEAD_DIM] / l)
                    lses.append(m[:, cols] + jnp.log2(l))
                for i, pr in enumerate(pairs):
                    tile = jnp.concatenate(o_t[2 * i:2 * i + 2], axis=0)
                    o_ref[0, rows, pr * LANES:(pr + 1) * LANES] = tile.T.astype(BF16)
            lse_t = jnp.concatenate(lses + [jnp.zeros((LANES - A_HEADS, A_TQ), F32)], axis=0)
            lse_ref[0, rows, :] = lse_t.T

    pl.when(t % 2 == 0)(functools.partial(step, s_even_ref, s_odd_ref))
    pl.when(t % 2 == 1)(functools.partial(step, s_odd_ref, s_even_ref))


def _dilated_branch(pav, bias, dil):
    b, l, _ = pav.shape
    step = A_SUB * A_TQ
    assert A_SUB >= 2 and l % step == 0
    nt = l // step
    n_tiles = b * dil * nt
    halo = A_RADIUS
    hb = step // halo
    nblk = PA_W // A_W

    def decode(tt):
        return tt // (dil * nt), (tt // nt) % dil, tt % nt

    def scores_tile(t):
        return decode(jnp.minimum(t, n_tiles - 1))

    def finish_tile(t):
        return decode(jnp.maximum(t - 1, 0))

    def cur(tile, c):
        def index(t):
            bi, r, i = tile(t)
            return bi, i, r * nblk + c
        return pl.BlockSpec((1, step, A_W), index)

    def halo_spec(tile, c, side):
        def index(t):
            bi, r, i = tile(t)
            blk = jnp.maximum(i * hb - 1, 0) if side < 0 else jnp.minimum((i + 1) * hb, nt * hb - 1)
            return bi, blk, r * nblk + c
        return pl.BlockSpec((1, halo, A_W), index)

    def bias_spec(variant):
        def index(t):
            _, _, i = finish_tile(t)
            return variant(i), 0, 0, 0
        return pl.BlockSpec((None, A_HEADS // 2, A_TK, 2 * A_TQ), index)

    def sub_variant(sub):
        def variant(i):
            first = (i == 0).astype(jnp.int32) if sub == 0 else 0
            last = 2 * (i == nt - 1).astype(jnp.int32) if sub == A_SUB - 1 else 0
            return first + last
        return variant

    def out_spec(w):
        def index(t):
            bi, r, i = finish_tile(t)
            return bi, i, r
        return pl.BlockSpec((1, step, w), index)

    n_scores = A_SUB * (A_HEADS // A_GROUP)
    o, lse = pl.pallas_call(
        _dilated_attn_kernel,
        grid=(n_tiles + 1,),
        in_specs=[
            cur(scores_tile, 0),
            halo_spec(scores_tile, 1, -1), cur(scores_tile, 1), halo_spec(scores_tile, 1, 1),
            halo_spec(finish_tile, 2, -1), cur(finish_tile, 2), halo_spec(finish_tile, 2, 1),
        ] + [bias_spec(sub_variant(sub)) for sub in range(A_SUB)],
        out_specs=[out_spec(A_W), out_spec(LANES)],
        out_shape=[
            jax.ShapeDtypeStruct((b, l, dil * A_W), BF16),
            jax.ShapeDtypeStruct((b, l, dil * LANES), F32),
        ],
        scratch_shapes=[pltpu.VMEM((n_scores, A_TK, A_GROUP * A_TQ), F32) for _ in range(2)],
        compiler_params=_cparams("arbitrary"),
        name=f"dilated_attn_d{dil}",
    )(*([pav] * 7 + [bias] * A_SUB))
    return o.reshape(b * l, dil * A_W), lse.reshape(b * l, dil * LANES)


def _window_attn_kernel(sink_ref, q_ref, kp_ref, kc_ref, kn_ref, vp_ref, vc_ref, vn_ref, *refs):
    bias_refs = refs[:C_SUB]
    o_ref, s_even_ref, s_odd_ref = refs[C_SUB:]
    t = pl.program_id(0)

    @pl.when(t == 0)
    def _():
        s_odd_ref[...] = jnp.zeros_like(s_odd_ref)

    def step(s_write_ref, s_read_ref):
        kw = jnp.concatenate([kp_ref[0], kc_ref[0], kn_ref[0]], axis=0)
        for sub in range(C_SUB):
            rows = slice(sub * C_TQ, (sub + 1) * C_TQ)
            keys = slice(sub * C_TQ, sub * C_TQ + C_TK)
            for g in range(C_GROUP):
                qg = jnp.concatenate(
                    [q_ref[0, rows, pr * LANES:(pr + 1) * LANES] for pr in (g, g + C_GROUP)], axis=1)
                s_write_ref[sub * C_GROUP + g] = _nt_dot(kw[keys], _masked_heads(qg))

        vt = jnp.concatenate([vp_ref[0], vc_ref[0], vn_ref[0]], axis=0).T
        ones = jnp.ones((16, C_TK), BF16)
        for sub, bias_ref in enumerate(bias_refs):
            rows = slice(sub * C_TQ, (sub + 1) * C_TQ)
            keys = slice(sub * C_TQ, sub * C_TQ + C_TK)
            for g in range(C_GROUP):
                pairs = (g, g + C_GROUP)
                s = s_read_ref[sub * C_GROUP + g] + jnp.concatenate([bias_ref[pr] for pr in pairs], axis=1)
                sk = jnp.concatenate(
                    [jnp.full((1, C_TQ), sink_ref[2 * pr + e], F32) for pr in pairs for e in range(2)], axis=1)
                m = jnp.maximum(jnp.max(s, axis=0, keepdims=True), sk)
                pb = jnp.exp2(s - m).astype(BF16)
                sink_p = jnp.exp2(sk - m)
                o_t = []
                for kv in range(C_KV_HEADS):
                    cols = slice(kv * C_TQ, (kv + 1) * C_TQ)
                    va = jnp.concatenate([vt[kv * HEAD_DIM:(kv + 1) * HEAD_DIM, keys], ones], axis=0)
                    o = jnp.dot(va, pb[:, cols], preferred_element_type=F32)
                    den = o[HEAD_DIM:HEAD_DIM + 1] + sink_p[:, cols]
                    o_t.append(o[:HEAD_DIM] / den)
                for i, pr in enumerate(pairs):
                    tile = jnp.concatenate(o_t[2 * i:2 * i + 2], axis=0)
                    o_ref[0, rows, pr * LANES:(pr + 1) * LANES] = tile.T.astype(BF16)

    pl.when(t % 2 == 0)(functools.partial(step, s_even_ref, s_odd_ref))
    pl.when(t % 2 == 1)(functools.partial(step, s_odd_ref, s_even_ref))


def _window_attn(pc, bias, sink):
    b, s, _ = pc.shape
    step = C_SUB * C_TQ
    assert C_SUB >= 2 and s % step == 0
    nt = s // step
    n_tiles = b * nt
    kblk = C_W // C_KV_W

    def scores_tile(t):
        tt = jnp.minimum(t, n_tiles - 1)
        return tt // nt, tt % nt

    def finish_tile(t):
        tt = jnp.maximum(t - 1, 0)
        return tt // nt, tt % nt

    def cur(tile, w, c):
        def index(t):
            bi, i = tile(t)
            return bi, i, c
        return pl.BlockSpec((1, step, w), index)

    def halo_spec(tile, c, side):
        def index(t):
            bi, i = tile(t)
            blk = jnp.maximum(i * C_SUB - 1, 0) if side < 0 else jnp.minimum((i + 1) * C_SUB, nt * C_SUB - 1)
            return bi, blk, c
        return pl.BlockSpec((1, C_TQ, C_KV_W), index)

    def bias_spec(sub):
        def index(t):
            _, i = finish_tile(t)
            first = (i == 0).astype(jnp.int32) if sub == 0 else 0
            last = 2 * (i == nt - 1).astype(jnp.int32) if sub == C_SUB - 1 else 0
            return first + last, 0, 0, 0
        return pl.BlockSpec((None, C_HEADS // 2, C_TK, 2 * C_TQ), index)

    o = pl.pallas_call(
        _window_attn_kernel,
        grid=(n_tiles + 1,),
        in_specs=[
            pl.BlockSpec(memory_space=pltpu.SMEM),
            cur(scores_tile, C_W, 0),
            halo_spec(scores_tile, kblk, -1), cur(scores_tile, C_KV_W, kblk), halo_spec(scores_tile, kblk, 1),
            halo_spec(finish_tile, kblk + 1, -1), cur(finish_tile, C_KV_W, kblk + 1),
            halo_spec(finish_tile, kblk + 1, 1),
        ] + [bias_spec(sub) for sub in range(C_SUB)],
        out_specs=cur(finish_tile, C_W, 0),
        out_shape=jax.ShapeDtypeStruct((b, s, C_W), BF16),
        scratch_shapes=[pltpu.VMEM((C_SUB * C_GROUP, C_TK, C_KV_HEADS * C_TQ), F32) for _ in range(2)],
        compiler_params=_cparams("arbitrary"),
        name="window_gqa",
    )(sink.astype(F32), *([pc] * 7 + [bias] * C_SUB))
    return o.reshape(b * s, C_W)


def _rope(x, cos, sin, lane_lo):
    parts = []
    for a in range(0, R_QK_W, LANES):
        xh = x[:, a:a + LANES]
        half = R_QK_DIM // 2
        parts.append(jnp.where(lane_lo, pltpu.roll(xh, LANES - half, 1), pltpu.roll(xh, half, 1)))
    return x * cos + jnp.concatenate(parts, axis=1) * sin


def _retention_scores(q, k, cos, sin, decay, qdec, kdec):
    c = R_CHUNK
    lane = lax.broadcasted_iota(jnp.int32, (c, LANES), 1)
    lane_lo = (lane % R_QK_DIM) < (R_QK_DIM // 2)
    q = _rope(q.astype(F32), cos, sin, lane_lo)
    k = _rope(k.astype(F32), cos, sin, lane_lo) * (R_QK_DIM ** -0.5)
    qb = q.astype(BF16)
    head = lax.broadcasted_iota(jnp.int32, (c, R_QK_W), 1) // R_QK_DIM
    qs = jnp.concatenate([jnp.where(head == h, qb, jnp.zeros_like(qb)) for h in range(R_HEADS)], axis=0)
    s = _nt_dot(qs, k.astype(BF16))
    return (s * decay).astype(BF16), (q * qdec).astype(BF16), (k * kdec).astype(BF16)


def _retention_values(p, qd, kd, v, state, rdec, mask):
    c = R_CHUNK
    cross = jnp.dot(qd, state.astype(BF16), preferred_element_type=F32)
    vlane = lax.broadcasted_iota(jnp.int32, (c, R_W), 1) % LANES
    v_lo = jnp.where(vlane < R_V_DIM, v, jnp.zeros_like(v))
    v_hi = jnp.where(vlane >= R_V_DIM, v, jnp.zeros_like(v))
    pieces = []
    for j in range(R_HEADS // 2):
        cs = slice(j * LANES, (j + 1) * LANES)
        a = jnp.dot(p[(2 * j) * c:(2 * j + 1) * c], v_lo[:, cs], preferred_element_type=F32)
        a = a + jnp.dot(p[(2 * j + 1) * c:(2 * j + 2) * c], v_hi[:, cs], preferred_element_type=F32)
        pieces.append(a)
    intra = jnp.concatenate(pieces, axis=1)
    kv = lax.dot_general(kd, v, (((0,), (0,)), ((), ())), preferred_element_type=F32)
    return cross + intra, state * rdec + kv * mask


def _split_dot(x, w):
    hi = x.astype(BF16)
    lo = (x - hi.astype(F32)).astype(BF16)
    return jnp.dot(hi, w, preferred_element_type=F32) + jnp.dot(lo, w, preferred_element_type=F32)


def _retention_kernel(q_ref, k_ref, cos_ref, sin_ref, decay_ref, qdec_ref, kdec_ref, v_ref, rdec_ref, mask_ref,
                      *refs, n_chunks, steps_per_row, reverse, final):
    if final:
        fwd_ref, gate_ref, avg_ref = refs[:3]
        refs = refs[3:]
    o_ref, state_ref, p_even, p_odd, qd_even, qd_odd, kd_even, kd_odd = refs
    t = pl.program_id(0)

    @pl.when(t == 0)
    def _():
        for ref in (state_ref, p_odd, qd_odd, kd_odd):
            ref[...] = jnp.zeros_like(ref)

    restart = (jnp.maximum(t - 1, 0) % steps_per_row) == 0

    def step(write_refs, read_refs):
        p_w, qd_w, kd_w = write_refs
        p_r, qd_r, kd_r = read_refs
        for ci in range(n_chunks):
            rows = slice(ci * R_CHUNK, (ci + 1) * R_CHUNK)
            p_w[ci], qd_w[ci], kd_w[ci] = _retention_scores(
                q_ref[0, rows, :], k_ref[0, rows, :], cos_ref[rows, :], sin_ref[rows, :],
                decay_ref[...], qdec_ref[...], kdec_ref[...])

        state = jnp.where(restart, jnp.zeros_like(state_ref), state_ref[...])
        for ci in (reversed(range(n_chunks)) if reverse else range(n_chunks)):
            rows = slice(ci * R_CHUNK, (ci + 1) * R_CHUNK)
            o, state = _retention_values(p_r[ci], qd_r[ci], kd_r[ci], v_ref[0, rows, :], state,
                                         rdec_ref[...], mask_ref[...])
            if not final:
                o_ref[0, rows, :] = o
                continue
            o = o + fwd_ref[0, rows, :]
            avg = avg_ref[...]
            d = o - jnp.dot(o.astype(BF16), avg, preferred_element_type=F32)
            y = d * lax.rsqrt(jnp.dot((d * d).astype(BF16), avg, preferred_element_type=F32) + GN_EPS)
            g = gate_ref[0, rows, :].astype(F32)
            o_ref[0, rows, :] = (y * (g / (1.0 + jnp.exp(-g)))).astype(BF16)
        state_ref[...] = state

    even, odd = (p_even, qd_even, kd_even), (p_odd, qd_odd, kd_odd)
    pl.when(t % 2 == 0)(functools.partial(step, even, odd))
    pl.when(t % 2 == 1)(functools.partial(step, odd, even))


def _retention_tables(lg, reverse):
    c = R_CHUNK
    t = np.arange(c)
    diff = (t[None, :] - t[:, None]) if reverse else (t[:, None] - t[None, :])
    keep = (diff > 0) if reverse else (diff >= 0)
    decay = jnp.where(jnp.asarray(keep)[None], jnp.exp(lg[:, None, None] * np.maximum(diff, 0).astype(np.float32)), 0.0)
    tf = t.astype(np.float32)
    qpow = (c - tf) if reverse else (tf + 1.0)
    kpow = tf if reverse else (c - 1.0 - tf)
    qdec = jnp.repeat(jnp.exp(lg[None, :] * qpow[:, None]), R_QK_DIM, axis=1)
    kdec = jnp.repeat(jnp.exp(lg[None, :] * kpow[:, None]), R_QK_DIM, axis=1)
    rdec = jnp.broadcast_to(jnp.repeat(jnp.exp(lg * c), R_QK_DIM)[:, None], (R_QK_W, R_W))
    return decay.reshape(R_HEADS * c, c), qdec, kdec, rdec


def _retention_direction(pb, cos, sin, lg, reverse, final_inputs=None):
    b, s, _ = pb.shape
    rows = R_CHUNK * R_STEP_CHUNKS
    assert s % rows == 0
    nt = s // rows
    n_tiles = b * nt
    final = final_inputs is not None
    hd = np.arange(R_QK_W)[:, None] // R_QK_DIM == np.arange(R_W)[None, :] // R_V_DIM
    mask = jnp.asarray(hd.astype(np.float32))
    decay, qdec, kdec, rdec = _retention_tables(lg, reverse)

    def tile(t, finish):
        tt = jnp.maximum(t - 1, 0) if finish else jnp.minimum(t, n_tiles - 1)
        i = tt % nt
        return tt // nt, (nt - 1 - i) if reverse else i

    def seq(w, cb, finish):
        return pl.BlockSpec((1, rows, w), lambda t: (*tile(t, finish), cb))

    const = lambda shape: pl.BlockSpec(shape, lambda t: (0,) * len(shape))
    tab = pl.BlockSpec((rows, R_QK_W), lambda t: (tile(t, False)[1], 0))
    in_specs = [seq(R_QK_W, 0, False), seq(R_QK_W, 1, False), tab, tab,
                const((R_HEADS * R_CHUNK, R_CHUNK)), const((R_CHUNK, R_QK_W)), const((R_CHUNK, R_QK_W)),
                seq(R_W, 1, True), const((R_QK_W, R_W)), const((R_QK_W, R_W))]
    args = [pb, pb, cos, sin, decay, qdec, kdec, pb, rdec, mask]
    if final:
        o_fwd, avg = final_inputs
        in_specs += [seq(R_W, 0, True), seq(R_W, 2, True), const((R_W, R_W))]
        args += [o_fwd, pb, avg]
    stage = lambda shape: [pltpu.VMEM((R_STEP_CHUNKS,) + shape, BF16) for _ in range(2)]
    return pl.pallas_call(
        functools.partial(_retention_kernel, n_chunks=R_STEP_CHUNKS, steps_per_row=nt, reverse=reverse, final=final),
        grid=(n_tiles + 1,),
        in_specs=in_specs,
        out_specs=seq(R_W, 0, True),
        out_shape=jax.ShapeDtypeStruct((b, s, R_W), BF16 if final else F32),
        scratch_shapes=([pltpu.VMEM((R_QK_W, R_W), F32)] + stage((R_HEADS * R_CHUNK, R_CHUNK))
                        + stage((R_CHUNK, R_QK_W)) + stage((R_CHUNK, R_QK_W))),
        compiler_params=_cparams("arbitrary"),
        name="retention_bwd" if reverse else "retention_fwd",
    )(*args)


def _retention(pb, cos, sin, lg_f, lg_b):
    b, s, _ = pb.shape
    gh = np.arange(R_W)[:, None] // R_V_DIM == np.arange(R_W)[None, :] // R_V_DIM
    avg = jnp.asarray(gh.astype(np.float32) / R_V_DIM, dtype=BF16)
    o_fwd = _retention_direction(pb, cos, sin, lg_f, False)
    out = _retention_direction(pb, cos, sin, lg_b, True, (o_fwd, avg))
    return out.reshape(b * s, R_W)


def _rope_tables(s):
    half = R_QK_DIM // 2
    freqs = ROPE_BASE ** (-jnp.arange(half, dtype=F32) / half)
    ang = jnp.arange(s, dtype=F32)[:, None] * freqs[None]
    cos, sin = jnp.cos(ang), jnp.sin(ang)
    return (jnp.tile(jnp.concatenate([cos, cos], axis=1), (1, R_HEADS)),
            jnp.tile(jnp.concatenate([-sin, sin], axis=1), (1, R_HEADS)))


def _natural_order(o_ref, lse_ref, o_stage, lse_stage, d):
    if d == 1:
        return o_ref[...].astype(F32), lse_ref[...]
    n = o_ref.shape[0]
    for r in range(d):
        lse_stage[pl.ds(r, n, stride=d), :] = lse_ref[:, r * LANES:(r + 1) * LANES]
        for c in range(A_W // LANES):
            col = r * A_W + c * LANES
            o_stage[c, pl.ds(r, n, stride=d), :] = o_ref[:, col:col + LANES].astype(F32)
    return jnp.concatenate([o_stage[c] for c in range(A_W // LANES)], axis=1), lse_stage[...]


def _merge_out_kernel(*refs):
    n_a = len(A_DILATIONS)
    o_refs, lse_refs = refs[:n_a], refs[n_a:2 * n_a]
    r_ref, c_ref, x_ref, w_ref, g_ref, expand_ref, x1_ref, h2_ref = refs[2 * n_a:2 * n_a + 8]
    stages = refs[2 * n_a + 8:]
    outs, lses = [], []
    for b, d in enumerate(A_DILATIONS):
        o, lse = _natural_order(o_refs[b], lse_refs[b], stages[2 * b], stages[2 * b + 1], d)
        outs.append(o)
        lses.append(lse)
    top = functools.reduce(jnp.maximum, lses)
    ws = [jnp.exp2(l - top) for l in lses]
    den = functools.reduce(lambda u, v: u + v, ws)
    expand = expand_ref[...]
    a = None
    for wgt, o in zip(ws, outs):
        term = _split_dot(wgt / den, expand) * o
        a = term if a is None else a + term
    a = a.astype(BF16)
    r = r_ref[...]
    c = c_ref[...]
    nc = 512
    for n0 in range(0, D_MODEL, nc):
        cols = slice(n0, n0 + nc)
        y = jnp.dot(a, w_ref[0:A_W, cols], preferred_element_type=F32)
        y = y + jnp.dot(r, w_ref[A_W:A_W + R_W, cols], preferred_element_type=F32)
        y = y + jnp.dot(c, w_ref[A_W + R_W:MIX_W, cols], preferred_element_type=F32)
        x1_ref[:, cols] = x_ref[:, cols] + y
    x1 = x1_ref[...]
    h2 = x1 * lax.rsqrt(jnp.mean(x1 * x1, axis=-1, keepdims=True) + EPS) * g_ref[...]
    h2_ref[...] = h2.astype(BF16)


def _merge_out(branches, r, c, x, w, g, tm=512):
    t = x.shape[0]
    assert t % tm == 0 and all(tm % (16 * d) == 0 for d in A_DILATIONS)
    expand = jnp.asarray((np.arange(LANES)[:, None] == np.arange(A_W)[None, :] // HEAD_DIM).astype(np.float32), dtype=BF16)
    row = lambda w_: pl.BlockSpec((tm, w_), lambda i: (i, 0))
    dil_row = lambda w_, d: pl.BlockSpec((tm // d, d * w_), lambda i: (i, 0))
    stages = []
    for _ in A_DILATIONS:
        stages += [pltpu.VMEM((A_W // LANES, tm, LANES), F32), pltpu.VMEM((tm, LANES), F32)]
    return pl.pallas_call(
        _merge_out_kernel,
        grid=(t // tm,),
        in_specs=[dil_row(A_W, d) for d in A_DILATIONS] + [dil_row(LANES, d) for d in A_DILATIONS] + [
            row(R_W), row(C_W), row(D_MODEL),
            pl.BlockSpec((MIX_W, D_MODEL), lambda i: (0, 0), pipeline_mode=pl.Buffered(1)),
            pl.BlockSpec((1, D_MODEL), lambda i: (0, 0)),
            pl.BlockSpec((LANES, A_W), lambda i: (0, 0))],
        out_specs=[row(D_MODEL), row(D_MODEL)],
        out_shape=[jax.ShapeDtypeStruct((t, D_MODEL), F32), jax.ShapeDtypeStruct((t, D_MODEL), BF16)],
        scratch_shapes=stages,
        compiler_params=_cparams("parallel"),
        name="merge_out_proj",
    )(*[o for o, _ in branches], *[l for _, l in branches], r, c, x, w, g.reshape(1, D_MODEL), expand)


def _ffn_accumulate(h_ref, x_ref, wg_ref, wu_ref, wd_ref, o_ref):
    @pl.when(pl.program_id(1) == 0)
    def _():
        o_ref[...] = x_ref[...]

    h = h_ref[...]
    tf = wg_ref.shape[1]
    halves = [slice(c, c + tf // 2) for c in (0, tf // 2)]
    gs = [jnp.dot(h, wg_ref[:, c], preferred_element_type=F32) for c in halves]
    us = [jnp.dot(h, wu_ref[:, c], preferred_element_type=F32) for c in halves]
    y = None
    for g, u, c in zip(gs, us, halves):
        a = (g / (1.0 + jnp.exp(-g)) * u).astype(BF16)
        part = jnp.dot(a, wd_ref[c, :], preferred_element_type=F32)
        y = part if y is None else y + part
    o_ref[...] += y


def _ffn_kernel(h_ref, x_ref, wg_ref, wu_ref, wd_ref, o_ref):
    _ffn_accumulate(h_ref, x_ref, wg_ref, wu_ref, wd_ref, o_ref)


def _ffn_norm_kernel(h_ref, x_ref, wg_ref, wu_ref, wd_ref, g_ref, o_ref):
    _ffn_accumulate(h_ref, x_ref, wg_ref, wu_ref, wd_ref, o_ref)

    @pl.when(pl.program_id(1) == pl.num_programs(1) - 1)
    def _():
        x = o_ref[...]
        o_ref[...] = x * lax.rsqrt(jnp.mean(x * x, axis=-1, keepdims=True) + EPS) * g_ref[...]


def _ffn(h, x, wg, wu, wd, out_norm_g=None, tm=1024, tf=512):
    t = x.shape[0]
    assert t % tm == 0 and D_FF % tf == 0
    in_specs = [
        pl.BlockSpec((tm, D_MODEL), lambda i, k: (i, 0)),
        pl.BlockSpec((tm, D_MODEL), lambda i, k: (i, 0)),
        pl.BlockSpec((D_MODEL, tf), lambda i, k: (0, k)),
        pl.BlockSpec((D_MODEL, tf), lambda i, k: (0, k)),
        pl.BlockSpec((tf, D_MODEL), lambda i, k: (k, 0)),
    ]
    args = [h, x, wg, wu, wd]
    body = _ffn_kernel
    if out_norm_g is not None:
        in_specs.append(pl.BlockSpec((1, D_MODEL), lambda i, k: (0, 0)))
        args.append(out_norm_g.reshape(1, D_MODEL))
        body = _ffn_norm_kernel
    return pl.pallas_call(
        body,
        grid=(t // tm, D_FF // tf),
        in_specs=in_specs,
        out_specs=pl.BlockSpec((tm, D_MODEL), lambda i, k: (i, 0)),
        out_shape=jax.ShapeDtypeStruct((t, D_MODEL), F32),
        compiler_params=_cparams("parallel", "arbitrary"),
        name="swiglu_ffn",
    )(*args)


def _layer(x, b, s, p, out_norm_g):
    *pas, pb, pc = _norm_proj(x, p["g1"], p["w_in"])
    branches = [_dilated_branch(pa.reshape(b, s // d, d * PA_W), bias, d)
                for pa, bias, d in zip(pas, p["bias_a"], A_DILATIONS)]
    r = _retention(pb.reshape(b, s, PB_W), p["cos"], p["sin"], p["lg_f"], p["lg_b"])
    c = _window_attn(pc.reshape(b, s, PC_W), p["bias_c"], p["sink"])
    x1, h2 = _merge_out(branches, r, c, x, p["w_out"], p["g2"])
    return _ffn(h2, x1, p["w_gate"], p["w_up"], p["w_down"], out_norm_g)


def kernel(x_prompt, x_sample, rel_bias, norm1_g, w_in, ret_decay_fwd, ret_decay_bwd, attn_sink, w_out, norm2_g,
           w_gate, w_up, w_down, final_norm_g):
    bias_a = [_pair_heads(_band_bias(rel_bias[:, :A_HEADS], A_TQ, A_TK, A_RADIUS, A_RADIUS, dil))
              for _, dil in A_BRANCHES]
    q_order = np.asarray(C_Q_ORDER)
    bias_c = _pair_heads(_band_bias(rel_bias[:, A_HEADS:][:, q_order], C_TQ, C_TK, C_TQ, C_RADIUS, 1))

    def reorder_c_heads(w, axis, start):
        take = lambda a, n: lax.slice_in_dim(w, a, a + n, axis=axis)
        heads = [take(start + h * HEAD_DIM, HEAD_DIM) for h in C_Q_ORDER]
        tail = start + C_W
        return jnp.concatenate([take(0, start)] + heads + [take(tail, w.shape[axis] - tail)], axis=axis)

    layers = []
    for i in range(DEPTH):
        layers.append(dict(
            g1=norm1_g[i], g2=norm2_g[i], sink=attn_sink[i].astype(F32)[q_order] * LOG2E, bias_a=bias_a, bias_c=bias_c,
            w_in=reorder_c_heads(w_in[i].astype(BF16), 1, PA_W + PB_W),
            w_out=reorder_c_heads(w_out[i].astype(BF16), 0, A_W + R_W),
            w_gate=w_gate[i].astype(BF16), w_up=w_up[i].astype(BF16), w_down=w_down[i].astype(BF16),
            lg_f=jnp.log1p(-jnp.exp2(-ret_decay_fwd[i].astype(F32))),
            lg_b=jnp.log1p(-jnp.exp2(-ret_decay_bwd[i].astype(F32))),
        ))

    def trunk(x):
        b, s, _ = x.shape
        cos, sin = _rope_tables(s)
        x = x.reshape(b * s, D_MODEL)
        for i, p in enumerate(layers):
            x = _layer(x, b, s, dict(p, cos=cos, sin=sin), final_norm_g if i == DEPTH - 1 else None)
        return x.reshape(b, s, D_MODEL)

    return trunk(x_prompt), trunk(x_sample)
```

```python
import functools
import math

import numpy as np
import jax
import jax.numpy as jnp
from jax import lax
from jax.experimental import pallas as pl
from jax.experimental.pallas import tpu as pltpu

D_MODEL = 2048
DEPTH = 2
HEAD_DIM = 64
A_HEADS = 12
A_BRANCHES = ((128, 1), (512, 4), (2048, 16))
A_RADIUS = 64
R_HEADS = 8
R_QK_DIM = 32
R_V_DIM = 64
R_CHUNK = 128
ROPE_BASE = 10000.0
C_HEADS = 12
C_KV_HEADS = 4
C_GROUP = C_HEADS // C_KV_HEADS
C_RADIUS = 128
REL_BUCKETS = 32
REL_MAX_DIST = 1024
D_FF = 5632
EPS = 1e-6
GN_EPS = 1e-5
NEG = -1e30

A_W = A_HEADS * HEAD_DIM
R_QK_W = R_HEADS * R_QK_DIM
R_W = R_HEADS * R_V_DIM
C_W = C_HEADS * HEAD_DIM
C_KV_W = C_KV_HEADS * HEAD_DIM
PA_W = 3 * A_W
PB_W = 2 * R_QK_W + 2 * R_W
PC_W = C_W + 2 * C_KV_W
IN_COLS = PA_W + PB_W + PC_W
MIX_W = A_W + R_W + C_W

LANES = 128
V7X_VMEM_BYTES = 64 * 1024 * 1024
VMEM_LIMIT = V7X_VMEM_BYTES - 4 * 1024 * 1024

A_TQ = 128
A_TK = A_TQ + 2 * A_RADIUS
A_SUB = 4
C_TQ = 128
C_TK = 3 * C_TQ
C_SUB = 4
C_Q_ORDER = tuple(kp * C_GROUP + j + e * C_GROUP
                  for kp in range(0, C_KV_HEADS, 2) for j in range(C_GROUP) for e in range(2))
R_STEP_CHUNKS = 8

LOG2E = 1.4426950408889634
QSCALE = HEAD_DIM ** -0.5 * LOG2E

BF16 = jnp.bfloat16
F32 = jnp.float32


def _cparams(*sem):
    return pltpu.CompilerParams(dimension_semantics=sem, vmem_limit_bytes=VMEM_LIMIT)


def _t5_bucket(rel):
    nb = REL_BUCKETS // 2
    max_exact = nb // 2
    ret = np.where(rel > 0, nb, 0)
    n = np.abs(rel)
    nf = np.maximum(n, 1).astype(np.float32)
    large = max_exact + (np.log(nf / max_exact) / math.log(REL_MAX_DIST / max_exact) * (nb - max_exact)).astype(np.int32)
    large = np.minimum(large, nb - 1)
    return (ret + np.where(n < max_exact, n, large)).astype(np.int32)


def _band_bias(table, tq, tk, lead, radius, dil):
    h = table.shape[1]
    n = tq + tk - 1
    offs = np.arange(n) - (tq - 1) - lead
    per_off = jnp.take(table.astype(F32), jnp.asarray(_t5_bucket(offs * dil)), axis=0).T * LOG2E
    padded = jnp.concatenate([per_off, jnp.zeros((h, 1), F32)], axis=1)
    toep = jnp.tile(padded, (1, tq))[:, :tq * n].reshape(h, tq, n)[:, :, tq - 1:tq - 1 + tk]
    off = np.arange(tk)[None, :] - lead - np.arange(tq)[:, None]
    band = np.abs(off) <= radius
    col = np.arange(tk)[None, :]
    first = band & (col >= lead)
    last = band & (col < tk - lead)
    masks = np.stack([band, first, last, first & last])
    return jnp.where(jnp.asarray(masks)[:, None], toep[None], NEG).swapaxes(-1, -2)


def _pair_heads(bias):
    v, h, tk, tq = bias.shape
    return bias.reshape(v, h // 2, 2, tk, tq).swapaxes(2, 3).reshape(v, h // 2, tk, 2 * tq)


_PROJ_CHUNKS = (
    (0, 0, 0, 768, QSCALE), (0, 768, 768, 768, None), (0, 1536, 1536, 768, None),
    (1, 0, 2304, 512, None), (1, 512, 2816, 512, None), (1, 1024, 3328, 512, None),
    (2, 0, 3840, 768, QSCALE), (2, 768, 4608, 512, None),
)


A_DILATIONS = tuple(d for _, d in A_BRANCHES)


def _norm_proj_kernel(x_ref, g_ref, w_ref, *refs):
    n_a = len(A_DILATIONS)
    pa_refs, (pb_ref, pc_ref), stage_refs = refs[:n_a], refs[n_a:n_a + 2], refs[n_a + 2:]
    tm = x_ref.shape[0]
    x = x_ref[...]
    h = (x * lax.rsqrt(jnp.mean(x * x, axis=-1, keepdims=True) + EPS) * g_ref[...]).astype(BF16)
    outs = (None, pb_ref, pc_ref)
    a_chunk = 0
    for oi, oc, wc, width, scale in _PROJ_CHUNKS:
        y = jnp.dot(h, w_ref[:, wc:wc + width], preferred_element_type=F32)
        if scale is not None:
            y = y * scale
        if oi != 0:
            outs[oi][:, oc:oc + width] = y.astype(BF16)
            continue
        stage, stage_mid = stage_refs[2 * a_chunk], stage_refs[2 * a_chunk + 1]
        a_chunk += 1
        _, d1, d2 = A_DILATIONS
        slabs = range(width // LANES)
        for c in slabs:
            stage[c] = y[:, c * LANES:(c + 1) * LANES]
        pa_refs[0][:, oc:oc + width] = y.astype(BF16)
        for r in range(d1):
            for c in slabs:
                part = stage[c, pl.ds(r, tm // d1, stride=d1), :]
                stage_mid[c, r] = part
                col = r * PA_W + oc + c * LANES
                pa_refs[1][:, col:col + LANES] = part.astype(BF16)
        for r in range(d1):
            for a in range(d2 // d1):
                for c in slabs:
                    col = (a * d1 + r) * PA_W + oc + c * LANES
                    part = stage_mid[c, r, pl.ds(a, tm // d2, stride=d2 // d1), :]
                    pa_refs[2][:, col:col + LANES] = part.astype(BF16)


def _norm_proj(x, g, w, tm=512):
    t = x.shape[0]
    assert t % tm == 0 and all(tm % (16 * d) == 0 for d in A_DILATIONS)
    d0, d1, d2 = A_DILATIONS
    assert d0 == 1 and d2 % d1 == 0
    n_a_chunks = sum(1 for c in _PROJ_CHUNKS if c[0] == 0)
    stages = []
    for _ in range(n_a_chunks):
        stages += [pltpu.VMEM((A_W // LANES, tm, LANES), F32), pltpu.VMEM((A_W // LANES, d1, tm // d1, LANES), F32)]
    return pl.pallas_call(
        _norm_proj_kernel,
        grid=(t // tm,),
        in_specs=[
            pl.BlockSpec((tm, D_MODEL), lambda i: (i, 0)),
            pl.BlockSpec((1, D_MODEL), lambda i: (0, 0)),
            pl.BlockSpec((D_MODEL, IN_COLS), lambda i: (0, 0), pipeline_mode=pl.Buffered(1)),
        ],
        out_specs=[pl.BlockSpec((tm // d, d * PA_W), lambda i: (i, 0)) for d in A_DILATIONS] + [
            pl.BlockSpec((tm, PB_W), lambda i: (i, 0)),
            pl.BlockSpec((tm, PC_W), lambda i: (i, 0)),
        ],
        out_shape=[jax.ShapeDtypeStruct((t // d, d * PA_W), BF16) for d in A_DILATIONS] + [
            jax.ShapeDtypeStruct((t, PB_W), BF16),
            jax.ShapeDtypeStruct((t, PC_W), BF16),
        ],
        scratch_shapes=stages,
        compiler_params=_cparams("parallel"),
        name="norm_in_proj",
    )(x, g.reshape(1, D_MODEL), w)


def _nt_dot(a, b):
    return lax.dot_general(a, b, (((1,), (1,)), ((), ())), preferred_element_type=F32)


def _masked_heads(qg):
    head = lax.broadcasted_iota(jnp.int32, qg.shape, 1) // HEAD_DIM
    zero = jnp.zeros_like(qg)
    return jnp.concatenate([jnp.where(head == e, qg, zero) for e in range(qg.shape[1] // HEAD_DIM)], axis=0)


A_GROUP = 4


def _dilated_attn_kernel(q_ref, kp_ref, kc_ref, kn_ref, vp_ref, vc_ref, vn_ref, *refs):
    bias_refs = refs[:A_SUB]
    o_ref, lse_ref, s_even_ref, s_odd_ref = refs[A_SUB:]
    t = pl.program_id(0)
    n_groups = A_HEADS // A_GROUP

    @pl.when(t == 0)
    def _():
        s_odd_ref[...] = jnp.zeros_like(s_odd_ref)

    def step(s_write_ref, s_read_ref):
        kw = jnp.concatenate([kp_ref[0], kc_ref[0], kn_ref[0]], axis=0)
        for sub, bias_ref in enumerate(bias_refs):
            rows = slice(sub * A_TQ, (sub + 1) * A_TQ)
            keys = slice(sub * A_TQ, sub * A_TQ + A_TK)
            for j in range(n_groups):
                lanes = slice(j * A_GROUP * HEAD_DIM, (j + 1) * A_GROUP * HEAD_DIM)
                bias = jnp.concatenate([bias_ref[pr] for pr in range(j * A_GROUP // 2, (j + 1) * A_GROUP // 2)], axis=1)
                s_write_ref[sub * n_groups + j] = _nt_dot(kw[keys, lanes], _masked_heads(q_ref[0, rows, lanes])) + bias

        vt = jnp.concatenate([vp_ref[0], vc_ref[0], vn_ref[0]], axis=0).T
        ones = jnp.ones((16, A_TK), BF16)
        for sub in range(A_SUB):
            rows = slice(sub * A_TQ, (sub + 1) * A_TQ)
            keys = slice(sub * A_TQ, sub * A_TQ + A_TK)
            lses = []
            for j in range(n_groups):
                pairs = range(j * A_GROUP // 2, (j + 1) * A_GROUP // 2)
                s = s_read_ref[sub * n_groups + j]
                m = jnp.max(s, axis=0, keepdims=True)
                pb = jnp.exp2(s - m).astype(BF16)
                o_t = []
                for e in range(A_GROUP):
                    h = A_GROUP * j + e
                    cols = slice(e * A_TQ, (e + 1) * A_TQ)
                    va = jnp.concatenate([vt[h * HEAD_DIM:(h + 1) * HEAD_DIM, keys], ones], axis=0)
                    o = jnp.dot(va, pb[:, cols], preferred_element_type=F32)
                    l = o[HEAD_DIM:HEAD_DIM + 1]
                    o_t.append(o[:HEAD_DIM] / l)
                    lses.append(m[:, cols] + jnp.log2(l))
                for i, pr in enumerate(pairs):
                    tile = jnp.concatenate(o_t[2 * i:2 * i + 2], axis=0)
                    o_ref[0, rows, pr * LANES:(pr + 1) * LANES] = tile.T.astype(BF16)
            lse_t = jnp.concatenate(lses + [jnp.zeros((LANES - A_HEADS, A_TQ), F32)], axis=0)
            lse_ref[0, rows, :] = lse_t.T

    pl.when(t % 2 == 0)(functools.partial(step, s_even_ref, s_odd_ref))
    pl.when(t % 2 == 1)(functools.partial(step, s_odd_ref, s_even_ref))


def _dilated_branch(pav, bias, dil):
    b, l, _ = pav.shape
    step = A_SUB * A_TQ
    assert A_SUB >= 2 and l % step == 0
    nt = l // step
    n_tiles = b * dil * nt
    halo = A_RADIUS
    hb = step // halo
    nblk = PA_W // A_W

    def decode(tt):
        return tt // (dil * nt), (tt // nt) % dil, tt % nt

    def scores_tile(t):
        return decode(jnp.minimum(t, n_tiles - 1))

    def finish_tile(t):
        return decode(jnp.maximum(t - 1, 0))

    def cur(tile, c):
        def index(t):
            bi, r, i = tile(t)
            return bi, i, r * nblk + c
        return pl.BlockSpec((1, step, A_W), index)

    def halo_spec(tile, c, side):
        def index(t):
            bi, r, i = tile(t)
            blk = jnp.maximum(i * hb - 1, 0) if side < 0 else jnp.minimum((i + 1) * hb, nt * hb - 1)
            return bi, blk, r * nblk + c
        return pl.BlockSpec((1, halo, A_W), index)

    def bias_spec(variant):
        def index(t):
            _, _, i = scores_tile(t)
            return variant(i), 0, 0, 0
        return pl.BlockSpec((None, A_HEADS // 2, A_TK, 2 * A_TQ), index)

    def sub_variant(sub):
        def variant(i):
            first = (i == 0).astype(jnp.int32) if sub == 0 else 0
            last = 2 * (i == nt - 1).astype(jnp.int32) if sub == A_SUB - 1 else 0
            return first + last
        return variant

    def out_spec(w):
        def index(t):
            bi, r, i = finish_tile(t)
            return bi, i, r
        return pl.BlockSpec((1, step, w), index)

    n_scores = A_SUB * (A_HEADS // A_GROUP)
    o, lse = pl.pallas_call(
        _dilated_attn_kernel,
        grid=(n_tiles + 1,),
        in_specs=[
            cur(scores_tile, 0),
            halo_spec(scores_tile, 1, -1), cur(scores_tile, 1), halo_spec(scores_tile, 1, 1),
            halo_spec(finish_tile, 2, -1), cur(finish_tile, 2), halo_spec(finish_tile, 2, 1),
        ] + [bias_spec(sub_variant(sub)) for sub in range(A_SUB)],
        out_specs=[out_spec(A_W), out_spec(LANES)],
        out_shape=[
            jax.ShapeDtypeStruct((b, l, dil * A_W), BF16),
            jax.ShapeDtypeStruct((b, l, dil * LANES), F32),
        ],
        scratch_shapes=[pltpu.VMEM((n_scores, A_TK, A_GROUP * A_TQ), F32) for _ in range(2)],
        compiler_params=_cparams("arbitrary"),
        name=f"dilated_attn_d{dil}",
    )(*([pav] * 7 + [bias] * A_SUB))
    return o.reshape(b * l, dil * A_W), lse.reshape(b * l, dil * LANES)


def _window_attn_kernel(sink_ref, q_ref, kp_ref, kc_ref, kn_ref, vp_ref, vc_ref, vn_ref, *refs):
    bias_refs = refs[:C_SUB]
    o_ref, s_even_ref, s_odd_ref = refs[C_SUB:]
    t = pl.program_id(0)

    @pl.when(t == 0)
    def _():
        s_odd_ref[...] = jnp.zeros_like(s_odd_ref)

    def step(s_write_ref, s_read_ref):
        kw = jnp.concatenate([kp_ref[0], kc_ref[0], kn_ref[0]], axis=0)
        for sub, bias_ref in enumerate(bias_refs):
            rows = slice(sub * C_TQ, (sub + 1) * C_TQ)
            keys = slice(sub * C_TQ, sub * C_TQ + C_TK)
            for g in range(C_GROUP):
                pairs = (g, g + C_GROUP)
                qg = jnp.concatenate([q_ref[0, rows, pr * LANES:(pr + 1) * LANES] for pr in pairs], axis=1)
                bias = jnp.concatenate([bias_ref[pr] for pr in pairs], axis=1)
                s_write_ref[sub * C_GROUP + g] = _nt_dot(kw[keys], _masked_heads(qg)) + bias

        vt = jnp.concatenate([vp_ref[0], vc_ref[0], vn_ref[0]], axis=0).T
        ones = jnp.ones((16, C_TK), BF16)
        for sub in range(C_SUB):
            rows = slice(sub * C_TQ, (sub + 1) * C_TQ)
            keys = slice(sub * C_TQ, sub * C_TQ + C_TK)
            for g in range(C_GROUP):
                pairs = (g, g + C_GROUP)
                s = s_read_ref[sub * C_GROUP + g]
                sk = jnp.concatenate(
                    [jnp.full((1, C_TQ), sink_ref[2 * pr + e], F32) for pr in pairs for e in range(2)], axis=1)
                m = jnp.maximum(jnp.max(s, axis=0, keepdims=True), sk)
                pb = jnp.exp2(s - m).astype(BF16)
                sink_p = jnp.exp2(sk - m)
                o_t = []
                for kv in range(C_KV_HEADS):
                    cols = slice(kv * C_TQ, (kv + 1) * C_TQ)
                    va = jnp.concatenate([vt[kv * HEAD_DIM:(kv + 1) * HEAD_DIM, keys], ones], axis=0)
                    o = jnp.dot(va, pb[:, cols], preferred_element_type=F32)
                    den = o[HEAD_DIM:HEAD_DIM + 1] + sink_p[:, cols]
                    o_t.append(o[:HEAD_DIM] / den)
                for i, pr in enumerate(pairs):
                    tile = jnp.concatenate(o_t[2 * i:2 * i + 2], axis=0)
                    o_ref[0, rows, pr * LANES:(pr + 1) * LANES] = tile.T.astype(BF16)

    pl.when(t % 2 == 0)(functools.partial(step, s_even_ref, s_odd_ref))
    pl.when(t % 2 == 1)(functools.partial(step, s_odd_ref, s_even_ref))


def _window_attn(pc, bias, sink):
    b, s, _ = pc.shape
    step = C_SUB * C_TQ
    assert C_SUB >= 2 and s % step == 0
    nt = s // step
    n_tiles = b * nt
    kblk = C_W // C_KV_W

    def scores_tile(t):
        tt = jnp.minimum(t, n_tiles - 1)
        return tt // nt, tt % nt

    def finish_tile(t):
        tt = jnp.maximum(t - 1, 0)
        return tt // nt, tt % nt

    def cur(tile, w, c):
        def index(t):
            bi, i = tile(t)
            return bi, i, c
        return pl.BlockSpec((1, step, w), index)

    def halo_spec(tile, c, side):
        def index(t):
            bi, i = tile(t)
            blk = jnp.maximum(i * C_SUB - 1, 0) if side < 0 else jnp.minimum((i + 1) * C_SUB, nt * C_SUB - 1)
            return bi, blk, c
        return pl.BlockSpec((1, C_TQ, C_KV_W), index)

    def bias_spec(sub):
        def index(t):
            _, i = scores_tile(t)
            first = (i == 0).astype(jnp.int32) if sub == 0 else 0
            last = 2 * (i == nt - 1).astype(jnp.int32) if sub == C_SUB - 1 else 0
            return first + last, 0, 0, 0
        return pl.BlockSpec((None, C_HEADS // 2, C_TK, 2 * C_TQ), index)

    o = pl.pallas_call(
        _window_attn_kernel,
        grid=(n_tiles + 1,),
        in_specs=[
            pl.BlockSpec(memory_space=pltpu.SMEM),
            cur(scores_tile, C_W, 0),
            halo_spec(scores_tile, kblk, -1), cur(scores_tile, C_KV_W, kblk), halo_spec(scores_tile, kblk, 1),
            halo_spec(finish_tile, kblk + 1, -1), cur(finish_tile, C_KV_W, kblk + 1),
            halo_spec(finish_tile, kblk + 1, 1),
        ] + [bias_spec(sub) for sub in range(C_SUB)],
        out_specs=cur(finish_tile, C_W, 0),
        out_shape=jax.ShapeDtypeStruct((b, s, C_W), BF16),
        scratch_shapes=[pltpu.VMEM((C_SUB * C_GROUP, C_TK, C_KV_HEADS * C_TQ), F32) for _ in range(2)],
        compiler_params=_cparams("arbitrary"),
        name="window_gqa",
    )(sink.astype(F32), *([pc] * 7 + [bias] * C_SUB))
    return o.reshape(b * s, C_W)


def _rope(x, cos, sin, lane_lo):
    parts = []
    for a in range(0, R_QK_W, LANES):
        xh = x[:, a:a + LANES]
        half = R_QK_DIM // 2
        parts.append(jnp.where(lane_lo, pltpu.roll(xh, LANES - half, 1), pltpu.roll(xh, half, 1)))
    return x * cos + jnp.concatenate(parts, axis=1) * sin


def _retention_scores(q, k, cos, sin, decay, qdec, kdec):
    c = R_CHUNK
    lane = lax.broadcasted_iota(jnp.int32, (c, LANES), 1)
    lane_lo = (lane % R_QK_DIM) < (R_QK_DIM // 2)
    q = _rope(q.astype(F32), cos, sin, lane_lo)
    k = _rope(k.astype(F32), cos, sin, lane_lo) * (R_QK_DIM ** -0.5)
    qb = q.astype(BF16)
    head = lax.broadcasted_iota(jnp.int32, (c, R_QK_W), 1) // R_QK_DIM
    qs = jnp.concatenate([jnp.where(head == h, qb, jnp.zeros_like(qb)) for h in range(R_HEADS)], axis=0)
    s = _nt_dot(qs, k.astype(BF16))
    return (s * decay).astype(BF16), (q * qdec).astype(BF16), (k * kdec).astype(BF16)


def _retention_values(p, qd, kd, v, state, rdec, mask):
    c = R_CHUNK
    cross = jnp.dot(qd, state.astype(BF16), preferred_element_type=F32)
    vlane = lax.broadcasted_iota(jnp.int32, (c, R_W), 1) % LANES
    v_lo = jnp.where(vlane < R_V_DIM, v, jnp.zeros_like(v))
    v_hi = jnp.where(vlane >= R_V_DIM, v, jnp.zeros_like(v))
    pieces = []
    for j in range(R_HEADS // 2):
        cs = slice(j * LANES, (j + 1) * LANES)
        a = jnp.dot(p[(2 * j) * c:(2 * j + 1) * c], v_lo[:, cs], preferred_element_type=F32)
        a = a + jnp.dot(p[(2 * j + 1) * c:(2 * j + 2) * c], v_hi[:, cs], preferred_element_type=F32)
        pieces.append(a)
    intra = jnp.concatenate(pieces, axis=1)
    kv = lax.dot_general(kd, v, (((0,), (0,)), ((), ())), preferred_element_type=F32)
    return cross + intra, state * rdec + kv * mask


def _split_dot(x, w):
    hi = x.astype(BF16)
    lo = (x - hi.astype(F32)).astype(BF16)
    return jnp.dot(hi, w, preferred_element_type=F32) + jnp.dot(lo, w, preferred_element_type=F32)


def _retention_kernel(q_ref, k_ref, cos_ref, sin_ref, decay_ref, qdec_ref, kdec_ref, v_ref, rdec_ref, mask_ref,
                      *refs, n_chunks, steps_per_row, reverse, final):
    if final:
        fwd_ref, gate_ref, avg_ref = refs[:3]
        refs = refs[3:]
    o_ref, state_ref, p_even, p_odd, qd_even, qd_odd, kd_even, kd_odd = refs
    t = pl.program_id(0)

    @pl.when(t == 0)
    def _():
        for ref in (state_ref, p_odd, qd_odd, kd_odd):
            ref[...] = jnp.zeros_like(ref)

    restart = (jnp.maximum(t - 1, 0) % steps_per_row) == 0

    def step(write_refs, read_refs):
        p_w, qd_w, kd_w = write_refs
        p_r, qd_r, kd_r = read_refs
        for ci in range(n_chunks):
            rows = slice(ci * R_CHUNK, (ci + 1) * R_CHUNK)
            p_w[ci], qd_w[ci], kd_w[ci] = _retention_scores(
                q_ref[0, rows, :], k_ref[0, rows, :], cos_ref[rows, :], sin_ref[rows, :],
                decay_ref[...], qdec_ref[...], kdec_ref[...])

        state = jnp.where(restart, jnp.zeros_like(state_ref), state_ref[...])
        for ci in (reversed(range(n_chunks)) if reverse else range(n_chunks)):
            rows = slice(ci * R_CHUNK, (ci + 1) * R_CHUNK)
            o, state = _retention_values(p_r[ci], qd_r[ci], kd_r[ci], v_ref[0, rows, :], state,
                                         rdec_ref[...], mask_ref[...])
            if not final:
                o_ref[0, rows, :] = o
                continue
            o = o + fwd_ref[0, rows, :]
            avg = avg_ref[...]
            d = o - jnp.dot(o.astype(BF16), avg, preferred_element_type=F32)
            y = d * lax.rsqrt(jnp.dot((d * d).astype(BF16), avg, preferred_element_type=F32) + GN_EPS)
            g = gate_ref[0, rows, :].astype(F32)
            o_ref[0, rows, :] = (y * (g / (1.0 + jnp.exp(-g)))).astype(BF16)
        state_ref[...] = state

    even, odd = (p_even, qd_even, kd_even), (p_odd, qd_odd, kd_odd)
    pl.when(t % 2 == 0)(functools.partial(step, even, odd))
    pl.when(t % 2 == 1)(functools.partial(step, odd, even))


def _retention_tables(lg, reverse):
    c = R_CHUNK
    t = np.arange(c)
    diff = (t[None, :] - t[:, None]) if reverse else (t[:, None] - t[None, :])
    keep = (diff > 0) if reverse else (diff >= 0)
    decay = jnp.where(jnp.asarray(keep)[None], jnp.exp(lg[:, None, None] * np.maximum(diff, 0).astype(np.float32)), 0.0)
    tf = t.astype(np.float32)
    qpow = (c - tf) if reverse else (tf + 1.0)
    kpow = tf if reverse else (c - 1.0 - tf)
    qdec = jnp.repeat(jnp.exp(lg[None, :] * qpow[:, None]), R_QK_DIM, axis=1)
    kdec = jnp.repeat(jnp.exp(lg[None, :] * kpow[:, None]), R_QK_DIM, axis=1)
    rdec = jnp.broadcast_to(jnp.repeat(jnp.exp(lg * c), R_QK_DIM)[:, None], (R_QK_W, R_W))
    return decay.reshape(R_HEADS * c, c), qdec, kdec, rdec


def _retention_direction(pb, cos, sin, lg, reverse, final_inputs=None):
    b, s, _ = pb.shape
    rows = R_CHUNK * R_STEP_CHUNKS
    assert s % rows == 0
    nt = s // rows
    n_tiles = b * nt
    final = final_inputs is not None
    hd = np.arange(R_QK_W)[:, None] // R_QK_DIM == np.arange(R_W)[None, :] // R_V_DIM
    mask = jnp.asarray(hd.astype(np.float32))
    decay, qdec, kdec, rdec = _retention_tables(lg, reverse)

    def tile(t, finish):
        tt = jnp.maximum(t - 1, 0) if finish else jnp.minimum(t, n_tiles - 1)
        i = tt % nt
        return tt // nt, (nt - 1 - i) if reverse else i

    def seq(w, cb, finish):
        return pl.BlockSpec((1, rows, w), lambda t: (*tile(t, finish), cb))

    const = lambda shape: pl.BlockSpec(shape, lambda t: (0,) * len(shape))
    tab = pl.BlockSpec((rows, R_QK_W), lambda t: (tile(t, False)[1], 0))
    in_specs = [seq(R_QK_W, 0, False), seq(R_QK_W, 1, False), tab, tab,
                const((R_HEADS * R_CHUNK, R_CHUNK)), const((R_CHUNK, R_QK_W)), const((R_CHUNK, R_QK_W)),
                seq(R_W, 1, True), const((R_QK_W, R_W)), const((R_QK_W, R_W))]
    args = [pb, pb, cos, sin, decay, qdec, kdec, pb, rdec, mask]
    if final:
        o_fwd, avg = final_inputs
        in_specs += [seq(R_W, 0, True), seq(R_W, 2, True), const((R_W, R_W))]
        args += [o_fwd, pb, avg]
    stage = lambda shape: [pltpu.VMEM((R_STEP_CHUNKS,) + shape, BF16) for _ in range(2)]
    return pl.pallas_call(
        functools.partial(_retention_kernel, n_chunks=R_STEP_CHUNKS, steps_per_row=nt, reverse=reverse, final=final),
        grid=(n_tiles + 1,),
        in_specs=in_specs,
        out_specs=seq(R_W, 0, True),
        out_shape=jax.ShapeDtypeStruct((b, s, R_W), BF16 if final else F32),
        scratch_shapes=([pltpu.VMEM((R_QK_W, R_W), F32)] + stage((R_HEADS * R_CHUNK, R_CHUNK))
                        + stage((R_CHUNK, R_QK_W)) + stage((R_CHUNK, R_QK_W))),
        compiler_params=_cparams("arbitrary"),
        name="retention_bwd" if reverse else "retention_fwd",
    )(*args)


def _retention(pb, cos, sin, lg_f, lg_b):
    b, s, _ = pb.shape
    gh = np.arange(R_W)[:, None] // R_V_DIM == np.arange(R_W)[None, :] // R_V_DIM
    avg = jnp.asarray(gh.astype(np.float32) / R_V_DIM, dtype=BF16)
    o_fwd = _retention_direction(pb, cos, sin, lg_f, False)
    out = _retention_direction(pb, cos, sin, lg_b, True, (o_fwd, avg))
    return out.reshape(b * s, R_W)


def _rope_tables(s):
    half = R_QK_DIM // 2
    freqs = ROPE_BASE ** (-jnp.arange(half, dtype=F32) / half)
    ang = jnp.arange(s, dtype=F32)[:, None] * freqs[None]
    cos, sin = jnp.cos(ang), jnp.sin(ang)
    return (jnp.tile(jnp.concatenate([cos, cos], axis=1), (1, R_HEADS)),
            jnp.tile(jnp.concatenate([-sin, sin], axis=1), (1, R_HEADS)))


def _natural_order(o_ref, lse_ref, o_stage, lse_stage, d):
    if d == 1:
        return o_ref[...].astype(F32), lse_ref[...]
    n = o_ref.shape[0]
    for r in range(d):
        lse_stage[pl.ds(r, n, stride=d), :] = lse_ref[:, r * LANES:(r + 1) * LANES]
        for c in range(A_W // LANES):
            col = r * A_W + c * LANES
            o_stage[c, pl.ds(r, n, stride=d), :] = o_ref[:, col:col + LANES].astype(F32)
    return jnp.concatenate([o_stage[c] for c in range(A_W // LANES)], axis=1), lse_stage[...]


def _merge_out_kernel(*refs):
    n_a = len(A_DILATIONS)
    o_refs, lse_refs = refs[:n_a], refs[n_a:2 * n_a]
    r_ref, c_ref, x_ref, w_ref, g_ref, expand_ref, x1_ref, h2_ref = refs[2 * n_a:2 * n_a + 8]
    stages = refs[2 * n_a + 8:]
    outs, lses = [], []
    for b, d in enumerate(A_DILATIONS):
        o, lse = _natural_order(o_refs[b], lse_refs[b], stages[2 * b], stages[2 * b + 1], d)
        outs.append(o)
        lses.append(lse)
    top = functools.reduce(jnp.maximum, lses)
    ws = [jnp.exp2(l - top) for l in lses]
    den = functools.reduce(lambda u, v: u + v, ws)
    expand = expand_ref[...]
    a = None
    for wgt, o in zip(ws, outs):
        term = _split_dot(wgt / den, expand) * o
        a = term if a is None else a + term
    a = a.astype(BF16)
    r = r_ref[...]
    c = c_ref[...]
    nc = 512
    for n0 in range(0, D_MODEL, nc):
        cols = slice(n0, n0 + nc)
        y = jnp.dot(a, w_ref[0:A_W, cols], preferred_element_type=F32)
        y = y + jnp.dot(r, w_ref[A_W:A_W + R_W, cols], preferred_element_type=F32)
        y = y + jnp.dot(c, w_ref[A_W + R_W:MIX_W, cols], preferred_element_type=F32)
        x1_ref[:, cols] = x_ref[:, cols] + y
    x1 = x1_ref[...]
    h2 = x1 * lax.rsqrt(jnp.mean(x1 * x1, axis=-1, keepdims=True) + EPS) * g_ref[...]
    h2_ref[...] = h2.astype(BF16)


def _merge_out(branches, r, c, x, w, g, tm=512):
    t = x.shape[0]
    assert t % tm == 0 and all(tm % (16 * d) == 0 for d in A_DILATIONS)
    expand = jnp.asarray((np.arange(LANES)[:, None] == np.arange(A_W)[None, :] // HEAD_DIM).astype(np.float32), dtype=BF16)
    row = lambda w_: pl.BlockSpec((tm, w_), lambda i: (i, 0))
    dil_row = lambda w_, d: pl.BlockSpec((tm // d, d * w_), lambda i: (i, 0))
    stages = []
    for _ in A_DILATIONS:
        stages += [pltpu.VMEM((A_W // LANES, tm, LANES), F32), pltpu.VMEM((tm, LANES), F32)]
    return pl.pallas_call(
        _merge_out_kernel,
        grid=(t // tm,),
        in_specs=[dil_row(A_W, d) for d in A_DILATIONS] + [dil_row(LANES, d) for d in A_DILATIONS] + [
            row(R_W), row(C_W), row(D_MODEL),
            pl.BlockSpec((MIX_W, D_MODEL), lambda i: (0, 0), pipeline_mode=pl.Buffered(1)),
            pl.BlockSpec((1, D_MODEL), lambda i: (0, 0)),
            pl.BlockSpec((LANES, A_W), lambda i: (0, 0))],
        out_specs=[row(D_MODEL), row(D_MODEL)],
        out_shape=[jax.ShapeDtypeStruct((t, D_MODEL), F32), jax.ShapeDtypeStruct((t, D_MODEL), BF16)],
        scratch_shapes=stages,
        compiler_params=_cparams("parallel"),
        name="merge_out_proj",
    )(*[o for o, _ in branches], *[l for _, l in branches], r, c, x, w, g.reshape(1, D_MODEL), expand)


def _ffn_accumulate(h_ref, x_ref, wg_ref, wu_ref, wd_ref, o_ref):
    @pl.when(pl.program_id(1) == 0)
    def _():
        o_ref[...] = x_ref[...]

    h = h_ref[...]
    tf = wg_ref.shape[1]
    halves = [slice(c, c + tf // 2) for c in (0, tf // 2)]
    gs = [jnp.dot(h, wg_ref[:, c], preferred_element_type=F32) for c in halves]
    us = [jnp.dot(h, wu_ref[:, c], preferred_element_type=F32) for c in halves]
    y = None
    for g, u, c in zip(gs, us, halves):
        a = (g / (1.0 + jnp.exp(-g)) * u).astype(BF16)
        part = jnp.dot(a, wd_ref[c, :], preferred_element_type=F32)
        y = part if y is None else y + part
    o_ref[...] += y


def _ffn_kernel(h_ref, x_ref, wg_ref, wu_ref, wd_ref, o_ref):
    _ffn_accumulate(h_ref, x_ref, wg_ref, wu_ref, wd_ref, o_ref)


def _ffn_norm_kernel(h_ref, x_ref, wg_ref, wu_ref, wd_ref, g_ref, o_ref):
    _ffn_accumulate(h_ref, x_ref, wg_ref, wu_ref, wd_ref, o_ref)

    @pl.when(pl.program_id(1) == pl.num_programs(1) - 1)
    def _():
        x = o_ref[...]
        o_ref[...] = x * lax.rsqrt(jnp.mean(x * x, axis=-1, keepdims=True) + EPS) * g_ref[...]


def _ffn(h, x, wg, wu, wd, out_norm_g=None, tm=1024, tf=512):
    t = x.shape[0]
    assert t % tm == 0 and D_FF % tf == 0
    in_specs = [
        pl.BlockSpec((tm, D_MODEL), lambda i, k: (i, 0)),
        pl.BlockSpec((tm, D_MODEL), lambda i, k: (i, 0)),
        pl.BlockSpec((D_MODEL, tf), lambda i, k: (0, k)),
        pl.BlockSpec((D_MODEL, tf), lambda i, k: (0, k)),
        pl.BlockSpec((tf, D_MODEL), lambda i, k: (k, 0)),
    ]
    args = [h, x, wg, wu, wd]
    body = _ffn_kernel
    if out_norm_g is not None:
        in_specs.append(pl.BlockSpec((1, D_MODEL), lambda i, k: (0, 0)))
        args.append(out_norm_g.reshape(1, D_MODEL))
        body = _ffn_norm_kernel
    return pl.pallas_call(
        body,
        grid=(t // tm, D_FF // tf),
        in_specs=in_specs,
        out_specs=pl.BlockSpec((tm, D_MODEL), lambda i, k: (i, 0)),
        out_shape=jax.ShapeDtypeStruct((t, D_MODEL), F32),
        compiler_params=_cparams("parallel", "arbitrary"),
        name="swiglu_ffn",
    )(*args)


def _layer(x, b, s, p, out_norm_g):
    *pas, pb, pc = _norm_proj(x, p["g1"], p["w_in"])
    branches = [_dilated_branch(pa.reshape(b, s // d, d * PA_W), bias, d)
                for pa, bias, d in zip(pas, p["bias_a"], A_DILATIONS)]
    r = _retention(pb.reshape(b, s, PB_W), p["cos"], p["sin"], p["lg_f"], p["lg_b"])
    c = _window_attn(pc.reshape(b, s, PC_W), p["bias_c"], p["sink"])
    x1, h2 = _merge_out(branches, r, c, x, p["w_out"], p["g2"])
    return _ffn(h2, x1, p["w_gate"], p["w_up"], p["w_down"], out_norm_g)


def kernel(x_prompt, x_sample, rel_bias, norm1_g, w_in, ret_decay_fwd, ret_decay_bwd, attn_sink, w_out, norm2_g,
           w_gate, w_up, w_down, final_norm_g):
    bias_a = [_pair_heads(_band_bias(rel_bias[:, :A_HEADS], A_TQ, A_TK, A_RADIUS, A_RADIUS, dil))
              for _, dil in A_BRANCHES]
    q_order = np.asarray(C_Q_ORDER)
    bias_c = _pair_heads(_band_bias(rel_bias[:, A_HEADS:][:, q_order], C_TQ, C_TK, C_TQ, C_RADIUS, 1))

    def reorder_c_heads(w, axis, start):
        take = lambda a, n: lax.slice_in_dim(w, a, a + n, axis=axis)
        heads = [take(start + h * HEAD_DIM, HEAD_DIM) for h in C_Q_ORDER]
        tail = start + C_W
        return jnp.concatenate([take(0, start)] + heads + [take(tail, w.shape[axis] - tail)], axis=axis)

    layers = []
    for i in range(DEPTH):
        layers.append(dict(
            g1=norm1_g[i], g2=norm2_g[i], sink=attn_sink[i].astype(F32)[q_order] * LOG2E, bias_a=bias_a, bias_c=bias_c,
            w_in=reorder_c_heads(w_in[i].astype(BF16), 1, PA_W + PB_W),
            w_out=reorder_c_heads(w_out[i].astype(BF16), 0, A_W + R_W),
            w_gate=w_gate[i].astype(BF16), w_up=w_up[i].astype(BF16), w_down=w_down[i].astype(BF16),
            lg_f=jnp.log1p(-jnp.exp2(-ret_decay_fwd[i].astype(F32))),
            lg_b=jnp.log1p(-jnp.exp2(-ret_decay_bwd[i].astype(F32))),
        ))

    def trunk(x):
        b, s, _ = x.shape
        cos, sin = _rope_tables(s)
        x = x.reshape(b * s, D_MODEL)
        for i, p in enumerate(layers):
            x = _layer(x, b, s, dict(p, cos=cos, sin=sin), final_norm_g if i == DEPTH - 1 else None)
        return x.reshape(b, s, D_MODEL)

    return trunk(x_prompt), trunk(x_sample)
```

```python
import functools
import math

import numpy as np
import jax
import jax.numpy as jnp
from jax import lax
from jax.experimental import pallas as pl
from jax.experimental.pallas import tpu as pltpu

D_MODEL = 2048
DEPTH = 2
HEAD_DIM = 64
A_HEADS = 12
A_BRANCHES = ((128, 1), (512, 4), (2048, 16))
A_RADIUS = 64
R_HEADS = 8
R_QK_DIM = 32
R_V_DIM = 64
R_CHUNK = 128
ROPE_BASE = 10000.0
C_HEADS = 12
C_KV_HEADS = 4
C_GROUP = C_HEADS // C_KV_HEADS
C_RADIUS = 128
REL_BUCKETS = 32
REL_MAX_DIST = 1024
D_FF = 5632
EPS = 1e-6
GN_EPS = 1e-5
NEG = -1e30

A_W = A_HEADS * HEAD_DIM
R_QK_W = R_HEADS * R_QK_DIM
R_W = R_HEADS * R_V_DIM
C_W = C_HEADS * HEAD_DIM
C_KV_W = C_KV_HEADS * HEAD_DIM
PA_W = 3 * A_W
PB_W = 2 * R_QK_W + 2 * R_W
PC_W = C_W + 2 * C_KV_W
IN_COLS = PA_W + PB_W + PC_W
MIX_W = A_W + R_W + C_W

LANES = 128
V7X_VMEM_BYTES = 64 * 1024 * 1024
VMEM_LIMIT = V7X_VMEM_BYTES - 4 * 1024 * 1024

A_TQ = 128
A_TK = A_TQ + 2 * A_RADIUS
A_SUB = 4
C_TQ = 128
C_TK = 3 * C_TQ
C_SUB = 4
C_Q_ORDER = tuple(kp * C_GROUP + j + e * C_GROUP
                  for kp in range(0, C_KV_HEADS, 2) for j in range(C_GROUP) for e in range(2))
R_STEP_CHUNKS = 8

LOG2E = 1.4426950408889634
QSCALE = HEAD_DIM ** -0.5 * LOG2E

BF16 = jnp.bfloat16
F32 = jnp.float32


def _cparams(*sem):
    return pltpu.CompilerParams(dimension_semantics=sem, vmem_limit_bytes=VMEM_LIMIT)


def _t5_bucket(rel):
    nb = REL_BUCKETS // 2
    max_exact = nb // 2
    ret = np.where(rel > 0, nb, 0)
    n = np.abs(rel)
    nf = np.maximum(n, 1).astype(np.float32)
    large = max_exact + (np.log(nf / max_exact) / math.log(REL_MAX_DIST / max_exact) * (nb - max_exact)).astype(np.int32)
    large = np.minimum(large, nb - 1)
    return (ret + np.where(n < max_exact, n, large)).astype(np.int32)


def _band_bias(table, tq, tk, lead, radius, dil):
    h = table.shape[1]
    n = tq + tk - 1
    offs = np.arange(n) - (tq - 1) - lead
    per_off = jnp.take(table.astype(F32), jnp.asarray(_t5_bucket(offs * dil)), axis=0).T * LOG2E
    padded = jnp.concatenate([per_off, jnp.zeros((h, 1), F32)], axis=1)
    toep = jnp.tile(padded, (1, tq))[:, :tq * n].reshape(h, tq, n)[:, :, tq - 1:tq - 1 + tk]
    off = np.arange(tk)[None, :] - lead - np.arange(tq)[:, None]
    band = np.abs(off) <= radius
    col = np.arange(tk)[None, :]
    first = band & (col >= lead)
    last = band & (col < tk - lead)
    masks = np.stack([band, first, last, first & last])
    return jnp.where(jnp.asarray(masks)[:, None], toep[None], NEG).swapaxes(-1, -2)


def _pair_heads(bias):
    v, h, tk, tq = bias.shape
    return bias.reshape(v, h // 2, 2, tk, tq).swapaxes(2, 3).reshape(v, h // 2, tk, 2 * tq)


_PROJ_CHUNKS = (
    (0, 0, 0, 768, QSCALE), (0, 768, 768, 768, None), (0, 1536, 1536, 768, None),
    (1, 0, 2304, 512, None), (1, 512, 2816, 512, None), (1, 1024, 3328, 512, None),
    (2, 0, 3840, 768, QSCALE), (2, 768, 4608, 512, None),
)


A_DILATIONS = tuple(d for _, d in A_BRANCHES)


def _norm_proj_kernel(x_ref, g_ref, w_ref, *refs):
    n_a = len(A_DILATIONS)
    pa_refs, (pb_ref, pc_ref), stage_refs = refs[:n_a], refs[n_a:n_a + 2], refs[n_a + 2:]
    tm = x_ref.shape[0]
    x = x_ref[...]
    h = (x * lax.rsqrt(jnp.mean(x * x, axis=-1, keepdims=True) + EPS) * g_ref[...]).astype(BF16)
    outs = (None, pb_ref, pc_ref)
    a_chunk = 0
    for oi, oc, wc, width, scale in _PROJ_CHUNKS:
        y = jnp.dot(h, w_ref[:, wc:wc + width], preferred_element_type=F32)
        if scale is not None:
            y = y * scale
        if oi != 0:
            outs[oi][:, oc:oc + width] = y.astype(BF16)
            continue
        stage, stage_mid = stage_refs[2 * a_chunk], stage_refs[2 * a_chunk + 1]
        a_chunk += 1
        _, d1, d2 = A_DILATIONS
        slabs = range(width // LANES)
        for c in slabs:
            stage[c] = y[:, c * LANES:(c + 1) * LANES]
        pa_refs[0][:, oc:oc + width] = y.astype(BF16)
        for r in range(d1):
            for c in slabs:
                part = stage[c, pl.ds(r, tm // d1, stride=d1), :]
                stage_mid[c, r] = part
                col = r * PA_W + oc + c * LANES
                pa_refs[1][:, col:col + LANES] = part.astype(BF16)
        for r in range(d1):
            for a in range(d2 // d1):
                for c in slabs:
                    col = (a * d1 + r) * PA_W + oc + c * LANES
                    part = stage_mid[c, r, pl.ds(a, tm // d2, stride=d2 // d1), :]
                    pa_refs[2][:, col:col + LANES] = part.astype(BF16)


def _norm_proj(x, g, w, tm=512):
    t = x.shape[0]
    assert t % tm == 0 and all(tm % (16 * d) == 0 for d in A_DILATIONS)
    d0, d1, d2 = A_DILATIONS
    assert d0 == 1 and d2 % d1 == 0
    n_a_chunks = sum(1 for c in _PROJ_CHUNKS if c[0] == 0)
    stages = []
    for _ in range(n_a_chunks):
        stages += [pltpu.VMEM((A_W // LANES, tm, LANES), F32), pltpu.VMEM((A_W // LANES, d1, tm // d1, LANES), F32)]
    return pl.pallas_call(
        _norm_proj_kernel,
        grid=(t // tm,),
        in_specs=[
            pl.BlockSpec((tm, D_MODEL), lambda i: (i, 0)),
            pl.BlockSpec((1, D_MODEL), lambda i: (0, 0)),
            pl.BlockSpec((D_MODEL, IN_COLS), lambda i: (0, 0), pipeline_mode=pl.Buffered(1)),
        ],
        out_specs=[pl.BlockSpec((tm // d, d * PA_W), lambda i: (i, 0)) for d in A_DILATIONS] + [
            pl.BlockSpec((tm, PB_W), lambda i: (i, 0)),
            pl.BlockSpec((tm, PC_W), lambda i: (i, 0)),
        ],
        out_shape=[jax.ShapeDtypeStruct((t // d, d * PA_W), BF16) for d in A_DILATIONS] + [
            jax.ShapeDtypeStruct((t, PB_W), BF16),
            jax.ShapeDtypeStruct((t, PC_W), BF16),
        ],
        scratch_shapes=stages,
        compiler_params=_cparams("parallel"),
        name="norm_in_proj",
    )(x, g.reshape(1, D_MODEL), w)


def _nt_dot(a, b):
    return lax.dot_general(a, b, (((1,), (1,)), ((), ())), preferred_element_type=F32)


def _masked_heads(qg):
    head = lax.broadcasted_iota(jnp.int32, qg.shape, 1) // HEAD_DIM
    zero = jnp.zeros_like(qg)
    return jnp.concatenate([jnp.where(head == e, qg, zero) for e in range(qg.shape[1] // HEAD_DIM)], axis=0)


A_GROUP = 4


def _dilated_attn_kernel(q_ref, kp_ref, kc_ref, kn_ref, vp_ref, vc_ref, vn_ref, *refs):
    bias_refs = refs[:A_SUB]
    o_ref, lse_ref, s_even_ref, s_odd_ref, m_even_ref, m_odd_ref = refs[A_SUB:]
    t = pl.program_id(0)
    n_groups = A_HEADS // A_GROUP

    @pl.when(t == 0)
    def _():
        s_odd_ref[...] = jnp.zeros_like(s_odd_ref)
        m_odd_ref[...] = jnp.zeros_like(m_odd_ref)

    def step(s_write_ref, s_read_ref, m_write_ref, m_read_ref):
        kw = jnp.concatenate([kp_ref[0], kc_ref[0], kn_ref[0]], axis=0)
        for sub, bias_ref in enumerate(bias_refs):
            rows = slice(sub * A_TQ, (sub + 1) * A_TQ)
            keys = slice(sub * A_TQ, sub * A_TQ + A_TK)
            for j in range(n_groups):
                lanes = slice(j * A_GROUP * HEAD_DIM, (j + 1) * A_GROUP * HEAD_DIM)
                bias = jnp.concatenate([bias_ref[pr] for pr in range(j * A_GROUP // 2, (j + 1) * A_GROUP // 2)], axis=1)
                sc = _nt_dot(kw[keys, lanes], _masked_heads(q_ref[0, rows, lanes])) + bias
                s_write_ref[sub * n_groups + j] = sc
                m_write_ref[sub * n_groups + j] = jnp.broadcast_to(jnp.max(sc, axis=0, keepdims=True), (8, sc.shape[1]))

        vt = jnp.concatenate([vp_ref[0], vc_ref[0], vn_ref[0]], axis=0).T
        ones = jnp.ones((16, A_TK), BF16)
        for sub in range(A_SUB):
            rows = slice(sub * A_TQ, (sub + 1) * A_TQ)
            keys = slice(sub * A_TQ, sub * A_TQ + A_TK)
            lses = []
            for j in range(n_groups):
                pairs = range(j * A_GROUP // 2, (j + 1) * A_GROUP // 2)
                s = s_read_ref[sub * n_groups + j]
                m = m_read_ref[sub * n_groups + j][0:1]
                pb = jnp.exp2(s - m).astype(BF16)
                o_t = []
                for e in range(A_GROUP):
                    h = A_GROUP * j + e
                    cols = slice(e * A_TQ, (e + 1) * A_TQ)
                    va = jnp.concatenate([vt[h * HEAD_DIM:(h + 1) * HEAD_DIM, keys], ones], axis=0)
                    o = jnp.dot(va, pb[:, cols], preferred_element_type=F32)
                    l = o[HEAD_DIM:HEAD_DIM + 1]
                    o_t.append(o[:HEAD_DIM] / l)
                    lses.append(m[:, cols] + jnp.log2(l))
                for i, pr in enumerate(pairs):
                    tile = jnp.concatenate(o_t[2 * i:2 * i + 2], axis=0)
                    o_ref[0, rows, pr * LANES:(pr + 1) * LANES] = tile.T.astype(BF16)
            lse_t = jnp.concatenate(lses + [jnp.zeros((LANES - A_HEADS, A_TQ), F32)], axis=0)
            lse_ref[0, rows, :] = lse_t.T

    pl.when(t % 2 == 0)(functools.partial(step, s_even_ref, s_odd_ref, m_even_ref, m_odd_ref))
    pl.when(t % 2 == 1)(functools.partial(step, s_odd_ref, s_even_ref, m_odd_ref, m_even_ref))


def _dilated_branch(pav, bias, dil):
    b, l, _ = pav.shape
    step = A_SUB * A_TQ
    assert A_SUB >= 2 and l % step == 0
    nt = l // step
    n_tiles = b * dil * nt
    halo = A_RADIUS
    hb = step // halo
    nblk = PA_W // A_W

    def decode(tt):
        return tt // (dil * nt), (tt // nt) % dil, tt % nt

    def scores_tile(t):
        return decode(jnp.minimum(t, n_tiles - 1))

    def finish_tile(t):
        return decode(jnp.maximum(t - 1, 0))

    def cur(tile, c):
        def index(t):
            bi, r, i = tile(t)
            return bi, i, r * nblk + c
        return pl.BlockSpec((1, step, A_W), index)

    def halo_spec(tile, c, side):
        def index(t):
            bi, r, i = tile(t)
            blk = jnp.maximum(i * hb - 1, 0) if side < 0 else jnp.minimum((i + 1) * hb, nt * hb - 1)
            return bi, blk, r * nblk + c
        return pl.BlockSpec((1, halo, A_W), index)

    def bias_spec(variant):
        def index(t):
            _, _, i = scores_tile(t)
            return variant(i), 0, 0, 0
        return pl.BlockSpec((None, A_HEADS // 2, A_TK, 2 * A_TQ), index)

    def sub_variant(sub):
        def variant(i):
            first = (i == 0).astype(jnp.int32) if sub == 0 else 0
            last = 2 * (i == nt - 1).astype(jnp.int32) if sub == A_SUB - 1 else 0
            return first + last
        return variant

    def out_spec(w):
        def index(t):
            bi, r, i = finish_tile(t)
            return bi, i, r
        return pl.BlockSpec((1, step, w), index)

    n_scores = A_SUB * (A_HEADS // A_GROUP)
    o, lse = pl.pallas_call(
        _dilated_attn_kernel,
        grid=(n_tiles + 1,),
        in_specs=[
            cur(scores_tile, 0),
            halo_spec(scores_tile, 1, -1), cur(scores_tile, 1), halo_spec(scores_tile, 1, 1),
            halo_spec(finish_tile, 2, -1), cur(finish_tile, 2), halo_spec(finish_tile, 2, 1),
        ] + [bias_spec(sub_variant(sub)) for sub in range(A_SUB)],
        out_specs=[out_spec(A_W), out_spec(LANES)],
        out_shape=[
            jax.ShapeDtypeStruct((b, l, dil * A_W), BF16),
            jax.ShapeDtypeStruct((b, l, dil * LANES), F32),
        ],
        scratch_shapes=([pltpu.VMEM((n_scores, A_TK, A_GROUP * A_TQ), F32) for _ in range(2)]
                        + [pltpu.VMEM((n_scores, 8, A_GROUP * A_TQ), F32) for _ in range(2)]),
        compiler_params=_cparams("arbitrary"),
        name=f"dilated_attn_d{dil}",
    )(*([pav] * 7 + [bias] * A_SUB))
    return o.reshape(b * l, dil * A_W), lse.reshape(b * l, dil * LANES)


def _window_attn_kernel(sink_ref, q_ref, kp_ref, kc_ref, kn_ref, vp_ref, vc_ref, vn_ref, *refs):
    bias_refs = refs[:C_SUB]
    o_ref, s_even_ref, s_odd_ref, m_even_ref, m_odd_ref = refs[C_SUB:]
    t = pl.program_id(0)

    @pl.when(t == 0)
    def _():
        s_odd_ref[...] = jnp.zeros_like(s_odd_ref)
        m_odd_ref[...] = jnp.zeros_like(m_odd_ref)

    def step(s_write_ref, s_read_ref, m_write_ref, m_read_ref):
        kw = jnp.concatenate([kp_ref[0], kc_ref[0], kn_ref[0]], axis=0)
        for sub, bias_ref in enumerate(bias_refs):
            rows = slice(sub * C_TQ, (sub + 1) * C_TQ)
            keys = slice(sub * C_TQ, sub * C_TQ + C_TK)
            for g in range(C_GROUP):
                pairs = (g, g + C_GROUP)
                qg = jnp.concatenate([q_ref[0, rows, pr * LANES:(pr + 1) * LANES] for pr in pairs], axis=1)
                bias = jnp.concatenate([bias_ref[pr] for pr in pairs], axis=1)
                sc = _nt_dot(kw[keys], _masked_heads(qg)) + bias
                s_write_ref[sub * C_GROUP + g] = sc
                m_write_ref[sub * C_GROUP + g] = jnp.broadcast_to(jnp.max(sc, axis=0, keepdims=True), (8, C_KV_HEADS * C_TQ))

        vt = jnp.concatenate([vp_ref[0], vc_ref[0], vn_ref[0]], axis=0).T
        ones = jnp.ones((16, C_TK), BF16)
        for sub in range(C_SUB):
            rows = slice(sub * C_TQ, (sub + 1) * C_TQ)
            keys = slice(sub * C_TQ, sub * C_TQ + C_TK)
            for g in range(C_GROUP):
                pairs = (g, g + C_GROUP)
                s = s_read_ref[sub * C_GROUP + g]
                sk = jnp.concatenate(
                    [jnp.full((1, C_TQ), sink_ref[2 * pr + e], F32) for pr in pairs for e in range(2)], axis=1)
                m = jnp.maximum(m_read_ref[sub * C_GROUP + g][0:1], sk)
                pb = jnp.exp2(s - m).astype(BF16)
                sink_p = jnp.exp2(sk - m)
                o_t = []
                for kv in range(C_KV_HEADS):
                    cols = slice(kv * C_TQ, (kv + 1) * C_TQ)
                    va = jnp.concatenate([vt[kv * HEAD_DIM:(kv + 1) * HEAD_DIM, keys], ones], axis=0)
                    o = jnp.dot(va, pb[:, cols], preferred_element_type=F32)
                    den = o[HEAD_DIM:HEAD_DIM + 1] + sink_p[:, cols]
                    o_t.append(o[:HEAD_DIM] / den)
                for i, pr in enumerate(pairs):
                    tile = jnp.concatenate(o_t[2 * i:2 * i + 2], axis=0)
                    o_ref[0, rows, pr * LANES:(pr + 1) * LANES] = tile.T.astype(BF16)

    pl.when(t % 2 == 0)(functools.partial(step, s_even_ref, s_odd_ref, m_even_ref, m_odd_ref))
    pl.when(t % 2 == 1)(functools.partial(step, s_odd_ref, s_even_ref, m_odd_ref, m_even_ref))


def _window_attn(pc, bias, sink):
    b, s, _ = pc.shape
    step = C_SUB * C_TQ
    assert C_SUB >= 2 and s % step == 0
    nt = s // step
    n_tiles = b * nt
    kblk = C_W // C_KV_W

    def scores_tile(t):
        tt = jnp.minimum(t, n_tiles - 1)
        return tt // nt, tt % nt

    def finish_tile(t):
        tt = jnp.maximum(t - 1, 0)
        return tt // nt, tt % nt

    def cur(tile, w, c):
        def index(t):
            bi, i = tile(t)
            return bi, i, c
        return pl.BlockSpec((1, step, w), index)

    def halo_spec(tile, c, side):
        def index(t):
            bi, i = tile(t)
            blk = jnp.maximum(i * C_SUB - 1, 0) if side < 0 else jnp.minimum((i + 1) * C_SUB, nt * C_SUB - 1)
            return bi, blk, c
        return pl.BlockSpec((1, C_TQ, C_KV_W), index)

    def bias_spec(sub):
        def index(t):
            _, i = scores_tile(t)
            first = (i == 0).astype(jnp.int32) if sub == 0 else 0
            last = 2 * (i == nt - 1).astype(jnp.int32) if sub == C_SUB - 1 else 0
            return first + last, 0, 0, 0
        return pl.BlockSpec((None, C_HEADS // 2, C_TK, 2 * C_TQ), index)

    o = pl.pallas_call(
        _window_attn_kernel,
        grid=(n_tiles + 1,),
        in_specs=[
            pl.BlockSpec(memory_space=pltpu.SMEM),
            cur(scores_tile, C_W, 0),
            halo_spec(scores_tile, kblk, -1), cur(scores_tile, C_KV_W, kblk), halo_spec(scores_tile, kblk, 1),
            halo_spec(finish_tile, kblk + 1, -1), cur(finish_tile, C_KV_W, kblk + 1),
            halo_spec(finish_tile, kblk + 1, 1),
        ] + [bias_spec(sub) for sub in range(C_SUB)],
        out_specs=cur(finish_tile, C_W, 0),
        out_shape=jax.ShapeDtypeStruct((b, s, C_W), BF16),
        scratch_shapes=([pltpu.VMEM((C_SUB * C_GROUP, C_TK, C_KV_HEADS * C_TQ), F32) for _ in range(2)]
                        + [pltpu.VMEM((C_SUB * C_GROUP, 8, C_KV_HEADS * C_TQ), F32) for _ in range(2)]),
        compiler_params=_cparams("arbitrary"),
        name="window_gqa",
    )(sink.astype(F32), *([pc] * 7 + [bias] * C_SUB))
    return o.reshape(b * s, C_W)


def _rope(x, cos, sin, lane_lo):
    parts = []
    for a in range(0, R_QK_W, LANES):
        xh = x[:, a:a + LANES]
        half = R_QK_DIM // 2
        parts.append(jnp.where(lane_lo, pltpu.roll(xh, LANES - half, 1), pltpu.roll(xh, half, 1)))
    return x * cos + jnp.concatenate(parts, axis=1) * sin


def _retention_scores(q, k, cos, sin, decay, qdec, kdec):
    c = R_CHUNK
    lane = lax.broadcasted_iota(jnp.int32, (c, LANES), 1)
    lane_lo = (lane % R_QK_DIM) < (R_QK_DIM // 2)
    q = _rope(q.astype(F32), cos, sin, lane_lo)
    k = _rope(k.astype(F32), cos, sin, lane_lo) * (R_QK_DIM ** -0.5)
    qb = q.astype(BF16)
    head = lax.broadcasted_iota(jnp.int32, (c, R_QK_W), 1) // R_QK_DIM
    qs = jnp.concatenate([jnp.where(head == h, qb, jnp.zeros_like(qb)) for h in range(R_HEADS)], axis=0)
    s = _nt_dot(qs, k.astype(BF16))
    return (s * decay).astype(BF16), (q * qdec).astype(BF16), (k * kdec).astype(BF16)


def _retention_values(p, qd, kd, v, state, rdec, mask):
    c = R_CHUNK
    cross = jnp.dot(qd, state.astype(BF16), preferred_element_type=F32)
    vlane = lax.broadcasted_iota(jnp.int32, (c, R_W), 1) % LANES
    v_lo = jnp.where(vlane < R_V_DIM, v, jnp.zeros_like(v))
    v_hi = jnp.where(vlane >= R_V_DIM, v, jnp.zeros_like(v))
    pieces = []
    for j in range(R_HEADS // 2):
        cs = slice(j * LANES, (j + 1) * LANES)
        a = jnp.dot(p[(2 * j) * c:(2 * j + 1) * c], v_lo[:, cs], preferred_element_type=F32)
        a = a + jnp.dot(p[(2 * j + 1) * c:(2 * j + 2) * c], v_hi[:, cs], preferred_element_type=F32)
        pieces.append(a)
    intra = jnp.concatenate(pieces, axis=1)
    kv = lax.dot_general(kd, v, (((0,), (0,)), ((), ())), preferred_element_type=F32)
    return cross + intra, state * rdec + kv * mask


def _retention_kernel(q_ref, k_ref, cos_ref, sin_ref, decay_ref, qdec_ref, kdec_ref, v_ref, rdec_ref, mask_ref,
                      *refs, n_chunks, steps_per_row, reverse, final):
    if final:
        fwd_ref, gate_ref, avg_ref = refs[:3]
        refs = refs[3:]
    o_ref, state_ref, p_even, p_odd, qd_even, qd_odd, kd_even, kd_odd = refs
    t = pl.program_id(0)

    @pl.when(t == 0)
    def _():
        for ref in (state_ref, p_odd, qd_odd, kd_odd):
            ref[...] = jnp.zeros_like(ref)

    restart = (jnp.maximum(t - 1, 0) % steps_per_row) == 0

    def step(write_refs, read_refs):
        p_w, qd_w, kd_w = write_refs
        p_r, qd_r, kd_r = read_refs
        for ci in range(n_chunks):
            rows = slice(ci * R_CHUNK, (ci + 1) * R_CHUNK)
            p_w[ci], qd_w[ci], kd_w[ci] = _retention_scores(
                q_ref[0, rows, :], k_ref[0, rows, :], cos_ref[rows, :], sin_ref[rows, :],
                decay_ref[...], qdec_ref[...], kdec_ref[...])

        state = jnp.where(restart, jnp.zeros_like(state_ref), state_ref[...])
        for ci in (reversed(range(n_chunks)) if reverse else range(n_chunks)):
            rows = slice(ci * R_CHUNK, (ci + 1) * R_CHUNK)
            o, state = _retention_values(p_r[ci], qd_r[ci], kd_r[ci], v_ref[0, rows, :], state,
                                         rdec_ref[...], mask_ref[...])
            if not final:
                o_ref[0, rows, :] = o
                continue
            o = o + fwd_ref[0, rows, :]
            avg = avg_ref[...]
            d = o - jnp.dot(o.astype(BF16), avg, preferred_element_type=F32)
            y = d * lax.rsqrt(jnp.dot((d * d).astype(BF16), avg, preferred_element_type=F32) + GN_EPS)
            g = gate_ref[0, rows, :].astype(F32)
            o_ref[0, rows, :] = (y * (g / (1.0 + jnp.exp(-g)))).astype(BF16)
        state_ref[...] = state

    even, odd = (p_even, qd_even, kd_even), (p_odd, qd_odd, kd_odd)
    pl.when(t % 2 == 0)(functools.partial(step, even, odd))
    pl.when(t % 2 == 1)(functools.partial(step, odd, even))


def _retention_tables(lg, reverse):
    c = R_CHUNK
    t = np.arange(c)
    diff = (t[None, :] - t[:, None]) if reverse else (t[:, None] - t[None, :])
    keep = (diff > 0) if reverse else (diff >= 0)
    decay = jnp.where(jnp.asarray(keep)[None], jnp.exp(lg[:, None, None] * np.maximum(diff, 0).astype(np.float32)), 0.0)
    tf = t.astype(np.float32)
    qpow = (c - tf) if reverse else (tf + 1.0)
    kpow = tf if reverse else (c - 1.0 - tf)
    qdec = jnp.repeat(jnp.exp(lg[None, :] * qpow[:, None]), R_QK_DIM, axis=1)
    kdec = jnp.repeat(jnp.exp(lg[None, :] * kpow[:, None]), R_QK_DIM, axis=1)
    rdec = jnp.broadcast_to(jnp.repeat(jnp.exp(lg * c), R_QK_DIM)[:, None], (R_QK_W, R_W))
    return decay.reshape(R_HEADS * c, c), qdec, kdec, rdec


def _retention_direction(pb, cos, sin, lg, reverse, final_inputs=None):
    b, s, _ = pb.shape
    rows = R_CHUNK * R_STEP_CHUNKS
    assert s % rows == 0
    nt = s // rows
    n_tiles = b * nt
    final = final_inputs is not None
    hd = np.arange(R_QK_W)[:, None] // R_QK_DIM == np.arange(R_W)[None, :] // R_V_DIM
    mask = jnp.asarray(hd.astype(np.float32))
    decay, qdec, kdec, rdec = _retention_tables(lg, reverse)

    def tile(t, finish):
        tt = jnp.maximum(t - 1, 0) if finish else jnp.minimum(t, n_tiles - 1)
        i = tt % nt
        return tt // nt, (nt - 1 - i) if reverse else i

    def seq(w, cb, finish):
        return pl.BlockSpec((1, rows, w), lambda t: (*tile(t, finish), cb))

    const = lambda shape: pl.BlockSpec(shape, lambda t: (0,) * len(shape))
    tab = pl.BlockSpec((rows, R_QK_W), lambda t: (tile(t, False)[1], 0))
    in_specs = [seq(R_QK_W, 0, False), seq(R_QK_W, 1, False), tab, tab,
                const((R_HEADS * R_CHUNK, R_CHUNK)), const((R_CHUNK, R_QK_W)), const((R_CHUNK, R_QK_W)),
                seq(R_W, 1, True), const((R_QK_W, R_W)), const((R_QK_W, R_W))]
    args = [pb, pb, cos, sin, decay, qdec, kdec, pb, rdec, mask]
    if final:
        o_fwd, avg = final_inputs
        in_specs += [seq(R_W, 0, True), seq(R_W, 2, True), const((R_W, R_W))]
        args += [o_fwd, pb, avg]
    stage = lambda shape: [pltpu.VMEM((R_STEP_CHUNKS,) + shape, BF16) for _ in range(2)]
    return pl.pallas_call(
        functools.partial(_retention_kernel, n_chunks=R_STEP_CHUNKS, steps_per_row=nt, reverse=reverse, final=final),
        grid=(n_tiles + 1,),
        in_specs=in_specs,
        out_specs=seq(R_W, 0, True),
        out_shape=jax.ShapeDtypeStruct((b, s, R_W), BF16 if final else F32),
        scratch_shapes=([pltpu.VMEM((R_QK_W, R_W), F32)] + stage((R_HEADS * R_CHUNK, R_CHUNK))
                        + stage((R_CHUNK, R_QK_W)) + stage((R_CHUNK, R_QK_W))),
        compiler_params=_cparams("arbitrary"),
        name="retention_bwd" if reverse else "retention_fwd",
    )(*args)


def _retention(pb, cos, sin, lg_f, lg_b):
    b, s, _ = pb.shape
    gh = np.arange(R_W)[:, None] // R_V_DIM == np.arange(R_W)[None, :] // R_V_DIM
    avg = jnp.asarray(gh.astype(np.float32) / R_V_DIM, dtype=BF16)
    o_fwd = _retention_direction(pb, cos, sin, lg_f, False)
    out = _retention_direction(pb, cos, sin, lg_b, True, (o_fwd, avg))
    return out.reshape(b * s, R_W)


def _rope_tables(s):
    half = R_QK_DIM // 2
    freqs = ROPE_BASE ** (-jnp.arange(half, dtype=F32) / half)
    ang = jnp.arange(s, dtype=F32)[:, None] * freqs[None]
    cos, sin = jnp.cos(ang), jnp.sin(ang)
    return (jnp.tile(jnp.concatenate([cos, cos], axis=1), (1, R_HEADS)),
            jnp.tile(jnp.concatenate([-sin, sin], axis=1), (1, R_HEADS)))


def _natural_order(o_ref, lse_ref, o_stage, lse_stage, d):
    if d == 1:
        return o_ref[...].astype(F32), lse_ref[...]
    n = o_ref.shape[0]
    for r in range(d):
        lse_stage[pl.ds(r, n, stride=d), :] = lse_ref[:, r * LANES:(r + 1) * LANES]
        for c in range(A_W // LANES):
            col = r * A_W + c * LANES
            o_stage[c, pl.ds(r, n, stride=d), :] = o_ref[:, col:col + LANES].astype(F32)
    return jnp.concatenate([o_stage[c] for c in range(A_W // LANES)], axis=1), lse_stage[...]


def _merge_out_kernel(*refs):
    n_a = len(A_DILATIONS)
    o_refs, lse_refs = refs[:n_a], refs[n_a:2 * n_a]
    r_ref, c_ref, x_ref, w_ref, g_ref, expand_ref, x1_ref, h2_ref = refs[2 * n_a:2 * n_a + 8]
    stages = refs[2 * n_a + 8:]
    outs, lses = [], []
    for b, d in enumerate(A_DILATIONS):
        o, lse = _natural_order(o_refs[b], lse_refs[b], stages[2 * b], stages[2 * b + 1], d)
        outs.append(o)
        lses.append(lse)
    top = functools.reduce(jnp.maximum, lses)
    ws = [jnp.exp2(l - top) for l in lses]
    den = functools.reduce(lambda u, v: u + v, ws)
    expand = expand_ref[...]
    is_head = lax.broadcasted_iota(jnp.int32, den.shape, 1) < A_HEADS
    a = None
    for wgt, o in zip(ws, outs):
        wn = jnp.where(is_head, wgt / den, 0.0)
        hi = wn.astype(BF16).astype(F32)
        packed = hi + pltpu.roll(wn - hi, A_HEADS, 1)
        term = jnp.dot(packed.astype(BF16), expand, preferred_element_type=F32) * o
        a = term if a is None else a + term
    a = a.astype(BF16)
    r = r_ref[...]
    c = c_ref[...]
    nc = 512
    for n0 in range(0, D_MODEL, nc):
        cols = slice(n0, n0 + nc)
        y = jnp.dot(a, w_ref[0:A_W, cols], preferred_element_type=F32)
        y = y + jnp.dot(r, w_ref[A_W:A_W + R_W, cols], preferred_element_type=F32)
        y = y + jnp.dot(c, w_ref[A_W + R_W:MIX_W, cols], preferred_element_type=F32)
        x1_ref[:, cols] = x_ref[:, cols] + y
    x1 = x1_ref[...]
    h2 = x1 * lax.rsqrt(jnp.mean(x1 * x1, axis=-1, keepdims=True) + EPS) * g_ref[...]
    h2_ref[...] = h2.astype(BF16)


def _merge_out(branches, r, c, x, w, g, tm=512):
    t = x.shape[0]
    assert t % tm == 0 and all(tm % (16 * d) == 0 for d in A_DILATIONS)
    expand = jnp.asarray((np.arange(LANES)[:, None] % A_HEADS == np.arange(A_W)[None, :] // HEAD_DIM)
                         & (np.arange(LANES)[:, None] < 2 * A_HEADS), dtype=BF16)
    row = lambda w_: pl.BlockSpec((tm, w_), lambda i: (i, 0))
    dil_row = lambda w_, d: pl.BlockSpec((tm // d, d * w_), lambda i: (i, 0))
    stages = []
    for _ in A_DILATIONS:
        stages += [pltpu.VMEM((A_W // LANES, tm, LANES), F32), pltpu.VMEM((tm, LANES), F32)]
    return pl.pallas_call(
        _merge_out_kernel,
        grid=(t // tm,),
        in_specs=[dil_row(A_W, d) for d in A_DILATIONS] + [dil_row(LANES, d) for d in A_DILATIONS] + [
            row(R_W), row(C_W), row(D_MODEL),
            pl.BlockSpec((MIX_W, D_MODEL), lambda i: (0, 0), pipeline_mode=pl.Buffered(1)),
            pl.BlockSpec((1, D_MODEL), lambda i: (0, 0)),
            pl.BlockSpec((LANES, A_W), lambda i: (0, 0))],
        out_specs=[row(D_MODEL), row(D_MODEL)],
        out_shape=[jax.ShapeDtypeStruct((t, D_MODEL), F32), jax.ShapeDtypeStruct((t, D_MODEL), BF16)],
        scratch_shapes=stages,
        compiler_params=_cparams("parallel"),
        name="merge_out_proj",
    )(*[o for o, _ in branches], *[l for _, l in branches], r, c, x, w, g.reshape(1, D_MODEL), expand)


def _ffn_accumulate(h_ref, x_ref, wg_ref, wu_ref, wd_ref, o_ref):
    @pl.when(pl.program_id(1) == 0)
    def _():
        o_ref[...] = x_ref[...]

    h = h_ref[...]
    tf = wg_ref.shape[1]
    halves = [slice(c, c + tf // 2) for c in (0, tf // 2)]
    gs = [jnp.dot(h, wg_ref[:, c], preferred_element_type=F32) for c in halves]
    us = [jnp.dot(h, wu_ref[:, c], preferred_element_type=F32) for c in halves]
    y = None
    for g, u, c in zip(gs, us, halves):
        a = (g / (1.0 + jnp.exp(-g)) * u).astype(BF16)
        part = jnp.dot(a, wd_ref[c, :], preferred_element_type=F32)
        y = part if y is None else y + part
    o_ref[...] += y


def _ffn_kernel(h_ref, x_ref, wg_ref, wu_ref, wd_ref, o_ref):
    _ffn_accumulate(h_ref, x_ref, wg_ref, wu_ref, wd_ref, o_ref)


def _ffn_norm_kernel(h_ref, x_ref, wg_ref, wu_ref, wd_ref, g_ref, o_ref):
    _ffn_accumulate(h_ref, x_ref, wg_ref, wu_ref, wd_ref, o_ref)

    @pl.when(pl.program_id(1) == pl.num_programs(1) - 1)
    def _():
        x = o_ref[...]
        o_ref[...] = x * lax.rsqrt(jnp.mean(x * x, axis=-1, keepdims=True) + EPS) * g_ref[...]


def _ffn(h, x, wg, wu, wd, out_norm_g=None, tm=1024, tf=512):
    t = x.shape[0]
    assert t % tm == 0 and D_FF % tf == 0
    in_specs = [
        pl.BlockSpec((tm, D_MODEL), lambda i, k: (i, 0)),
        pl.BlockSpec((tm, D_MODEL), lambda i, k: (i, 0)),
        pl.BlockSpec((D_MODEL, tf), lambda i, k: (0, k)),
        pl.BlockSpec((D_MODEL, tf), lambda i, k: (0, k)),
        pl.BlockSpec((tf, D_MODEL), lambda i, k: (k, 0)),
    ]
    args = [h, x, wg, wu, wd]
    body = _ffn_kernel
    if out_norm_g is not None:
        in_specs.append(pl.BlockSpec((1, D_MODEL), lambda i, k: (0, 0)))
        args.append(out_norm_g.reshape(1, D_MODEL))
        body = _ffn_norm_kernel
    return pl.pallas_call(
        body,
        grid=(t // tm, D_FF // tf),
        in_specs=in_specs,
        out_specs=pl.BlockSpec((tm, D_MODEL), lambda i, k: (i, 0)),
        out_shape=jax.ShapeDtypeStruct((t, D_MODEL), F32),
        compiler_params=_cparams("parallel", "arbitrary"),
        name="swiglu_ffn",
    )(*args)


def _layer(x, b, s, p, out_norm_g):
    *pas, pb, pc = _norm_proj(x, p["g1"], p["w_in"])
    branches = [_dilated_branch(pa.reshape(b, s // d, d * PA_W), bias, d)
                for pa, bias, d in zip(pas, p["bias_a"], A_DILATIONS)]
    r = _retention(pb.reshape(b, s, PB_W), p["cos"], p["sin"], p["lg_f"], p["lg_b"])
    c = _window_attn(pc.reshape(b, s, PC_W), p["bias_c"], p["sink"])
    x1, h2 = _merge_out(branches, r, c, x, p["w_out"], p["g2"])
    return _ffn(h2, x1, p["w_gate"], p["w_up"], p["w_down"], out_norm_g)


def kernel(x_prompt, x_sample, rel_bias, norm1_g, w_in, ret_decay_fwd, ret_decay_bwd, attn_sink, w_out, norm2_g,
           w_gate, w_up, w_down, final_norm_g):
    bias_a = [_pair_heads(_band_bias(rel_bias[:, :A_HEADS], A_TQ, A_TK, A_RADIUS, A_RADIUS, dil))
              for _, dil in A_BRANCHES]
    q_order = np.asarray(C_Q_ORDER)
    bias_c = _pair_heads(_band_bias(rel_bias[:, A_HEADS:][:, q_order], C_TQ, C_TK, C_TQ, C_RADIUS, 1))

    def reorder_c_heads(w, axis, start):
        take = lambda a, n: lax.slice_in_dim(w, a, a + n, axis=axis)
        heads = [take(start + h * HEAD_DIM, HEAD_DIM) for h in C_Q_ORDER]
        tail = start + C_W
        return jnp.concatenate([take(0, start)] + heads + [take(tail, w.shape[axis] - tail)], axis=axis)

    layers = []
    for i in range(DEPTH):
        layers.append(dict(
            g1=norm1_g[i], g2=norm2_g[i], sink=attn_sink[i].astype(F32)[q_order] * LOG2E, bias_a=bias_a, bias_c=bias_c,
            w_in=reorder_c_heads(w_in[i].astype(BF16), 1, PA_W + PB_W),
            w_out=reorder_c_heads(w_out[i].astype(BF16), 0, A_W + R_W),
            w_gate=w_gate[i].astype(BF16), w_up=w_up[i].astype(BF16), w_down=w_down[i].astype(BF16),
            lg_f=jnp.log1p(-jnp.exp2(-ret_decay_fwd[i].astype(F32))),
            lg_b=jnp.log1p(-jnp.exp2(-ret_decay_bwd[i].astype(F32))),
        ))

    def trunk(x):
        b, s, _ = x.shape
        cos, sin = _rope_tables(s)
        x = x.reshape(b * s, D_MODEL)
        for i, p in enumerate(layers):
            x = _layer(x, b, s, dict(p, cos=cos, sin=sin), final_norm_g if i == DEPTH - 1 else None)
        return x.reshape(b, s, D_MODEL)

    return trunk(x_prompt), trunk(x_sample)
```

```python
import functools
import math

import numpy as np
import jax
import jax.numpy as jnp
from jax import lax
from jax.experimental import pallas as pl
from jax.experimental.pallas import tpu as pltpu

D_MODEL = 2048
DEPTH = 2
HEAD_DIM = 64
A_HEADS = 12
A_BRANCHES = ((128, 1), (512, 4), (2048, 16))
A_RADIUS = 64
R_HEADS = 8
R_QK_DIM = 32
R_V_DIM = 64
R_CHUNK = 128
ROPE_BASE = 10000.0
C_HEADS = 12
C_KV_HEADS = 4
C_GROUP = C_HEADS // C_KV_HEADS
C_RADIUS = 128
REL_BUCKETS = 32
REL_MAX_DIST = 1024
D_FF = 5632
EPS = 1e-6
GN_EPS = 1e-5
NEG = -1e30

A_W = A_HEADS * HEAD_DIM
R_QK_W = R_HEADS * R_QK_DIM
R_W = R_HEADS * R_V_DIM
C_W = C_HEADS * HEAD_DIM
C_KV_W = C_KV_HEADS * HEAD_DIM
PA_W = 3 * A_W
PB_W = 2 * R_QK_W + 2 * R_W
PC_W = C_W + 2 * C_KV_W
IN_COLS = PA_W + PB_W + PC_W
MIX_W = A_W + R_W + C_W

LANES = 128
SUBLANES = 8
BF16_ROWS = 16
V7X_VMEM_BYTES = 64 * 1024 * 1024
VMEM_LIMIT = V7X_VMEM_BYTES - 4 * 1024 * 1024

V7X_MXU_WIDTH = 256
PROJ_TOKENS = 512
FFN_TOKENS = 1024
FFN_COLS = 2 * V7X_MXU_WIDTH
OUT_PROJ_COLS = 2 * V7X_MXU_WIDTH

A_TQ = 128
A_TK = A_TQ + 2 * A_RADIUS
A_SUB = 4
C_TQ = 128
C_TK = 3 * C_TQ
C_SUB = 4
C_Q_ORDER = tuple(kp * C_GROUP + j + e * C_GROUP
                  for kp in range(0, C_KV_HEADS, 2) for j in range(C_GROUP) for e in range(2))
R_STEP_CHUNKS = 8

LOG2E = 1.4426950408889634
QSCALE = HEAD_DIM ** -0.5 * LOG2E

BF16 = jnp.bfloat16
F32 = jnp.float32


def _cparams(*sem):
    return pltpu.CompilerParams(dimension_semantics=sem, vmem_limit_bytes=VMEM_LIMIT)


def _t5_bucket(rel):
    nb = REL_BUCKETS // 2
    max_exact = nb // 2
    ret = np.where(rel > 0, nb, 0)
    n = np.abs(rel)
    nf = np.maximum(n, 1).astype(np.float32)
    large = max_exact + (np.log(nf / max_exact) / math.log(REL_MAX_DIST / max_exact) * (nb - max_exact)).astype(np.int32)
    large = np.minimum(large, nb - 1)
    return (ret + np.where(n < max_exact, n, large)).astype(np.int32)


def _band_bias(table, tq, tk, lead, radius, dil):
    h = table.shape[1]
    n = tq + tk - 1
    offs = np.arange(n) - (tq - 1) - lead
    per_off = jnp.take(table.astype(F32), jnp.asarray(_t5_bucket(offs * dil)), axis=0).T * LOG2E
    padded = jnp.concatenate([per_off, jnp.zeros((h, 1), F32)], axis=1)
    toep = jnp.tile(padded, (1, tq))[:, :tq * n].reshape(h, tq, n)[:, :, tq - 1:tq - 1 + tk]
    off = np.arange(tk)[None, :] - lead - np.arange(tq)[:, None]
    band = np.abs(off) <= radius
    col = np.arange(tk)[None, :]
    first = band & (col >= lead)
    last = band & (col < tk - lead)
    masks = np.stack([band, first, last, first & last])
    return jnp.where(jnp.asarray(masks)[:, None], toep[None], NEG).swapaxes(-1, -2)


def _pair_heads(bias):
    v, h, tk, tq = bias.shape
    return bias.reshape(v, h // 2, 2, tk, tq).swapaxes(2, 3).reshape(v, h // 2, tk, 2 * tq)


def _proj_chunks():
    chunks = [(0, i * A_W, i * A_W, A_W, QSCALE if i == 0 else None) for i in range(PA_W // A_W)]
    width = 2 * V7X_MXU_WIDTH
    chunks += [(1, c, PA_W + c, width, None) for c in range(0, PB_W, width)]
    chunks += [(2, 0, PA_W + PB_W, C_W, QSCALE), (2, C_W, PA_W + PB_W + C_W, 2 * C_KV_W, None)]
    return tuple(chunks)


_PROJ_CHUNKS = _proj_chunks()


A_DILATIONS = tuple(d for _, d in A_BRANCHES)


def _norm_proj_kernel(x_ref, g_ref, w_ref, *refs):
    n_a = len(A_DILATIONS)
    pa_refs, (pb_ref, pc_ref), stage_refs = refs[:n_a], refs[n_a:n_a + 2], refs[n_a + 2:]
    tm = x_ref.shape[0]
    x = x_ref[...]
    h = (x * lax.rsqrt(jnp.mean(x * x, axis=-1, keepdims=True) + EPS) * g_ref[...]).astype(BF16)
    outs = (None, pb_ref, pc_ref)
    a_chunk = 0
    for oi, oc, wc, width, scale in _PROJ_CHUNKS:
        y = jnp.dot(h, w_ref[:, wc:wc + width], preferred_element_type=F32)
        if scale is not None:
            y = y * scale
        if oi != 0:
            outs[oi][:, oc:oc + width] = y.astype(BF16)
            continue
        stage, stage_mid = stage_refs[2 * a_chunk], stage_refs[2 * a_chunk + 1]
        a_chunk += 1
        _, d1, d2 = A_DILATIONS
        slabs = range(width // LANES)
        for c in slabs:
            stage[c] = y[:, c * LANES:(c + 1) * LANES]
        pa_refs[0][:, oc:oc + width] = y.astype(BF16)
        for r in range(d1):
            for c in slabs:
                part = stage[c, pl.ds(r, tm // d1, stride=d1), :]
                stage_mid[c, r] = part
                col = r * PA_W + oc + c * LANES
                pa_refs[1][:, col:col + LANES] = part.astype(BF16)
        for r in range(d1):
            for a in range(d2 // d1):
                for c in slabs:
                    col = (a * d1 + r) * PA_W + oc + c * LANES
                    part = stage_mid[c, r, pl.ds(a, tm // d2, stride=d2 // d1), :]
                    pa_refs[2][:, col:col + LANES] = part.astype(BF16)


def _norm_proj(x, g, w, tm=PROJ_TOKENS):
    t = x.shape[0]
    assert t % tm == 0 and all(tm % (BF16_ROWS * d) == 0 for d in A_DILATIONS)
    d0, d1, d2 = A_DILATIONS
    assert d0 == 1 and d2 % d1 == 0
    n_a_chunks = sum(1 for c in _PROJ_CHUNKS if c[0] == 0)
    stages = []
    for _ in range(n_a_chunks):
        stages += [pltpu.VMEM((A_W // LANES, tm, LANES), F32), pltpu.VMEM((A_W // LANES, d1, tm // d1, LANES), F32)]
    return pl.pallas_call(
        _norm_proj_kernel,
        grid=(t // tm,),
        in_specs=[
            pl.BlockSpec((tm, D_MODEL), lambda i: (i, 0)),
            pl.BlockSpec((1, D_MODEL), lambda i: (0, 0)),
            pl.BlockSpec((D_MODEL, IN_COLS), lambda i: (0, 0), pipeline_mode=pl.Buffered(1)),
        ],
        out_specs=[pl.BlockSpec((tm // d, d * PA_W), lambda i: (i, 0)) for d in A_DILATIONS] + [
            pl.BlockSpec((tm, PB_W), lambda i: (i, 0)),
            pl.BlockSpec((tm, PC_W), lambda i: (i, 0)),
        ],
        out_shape=[jax.ShapeDtypeStruct((t // d, d * PA_W), BF16) for d in A_DILATIONS] + [
            jax.ShapeDtypeStruct((t, PB_W), BF16),
            jax.ShapeDtypeStruct((t, PC_W), BF16),
        ],
        scratch_shapes=stages,
        compiler_params=_cparams("parallel"),
        name="norm_in_proj",
    )(x, g.reshape(1, D_MODEL), w)


def _nt_dot(a, b):
    return lax.dot_general(a, b, (((1,), (1,)), ((), ())), preferred_element_type=F32)


def _masked_heads(qg):
    head = lax.broadcasted_iota(jnp.int32, qg.shape, 1) // HEAD_DIM
    zero = jnp.zeros_like(qg)
    return jnp.concatenate([jnp.where(head == e, qg, zero) for e in range(qg.shape[1] // HEAD_DIM)], axis=0)


A_GROUP = 4


def _dilated_attn_kernel(q_ref, kp_ref, kc_ref, kn_ref, vp_ref, vc_ref, vn_ref, *refs):
    bias_refs = refs[:A_SUB]
    o_ref, lse_ref, s_even_ref, s_odd_ref, m_even_ref, m_odd_ref = refs[A_SUB:]
    t = pl.program_id(0)
    n_groups = A_HEADS // A_GROUP

    @pl.when(t == 0)
    def _():
        s_odd_ref[...] = jnp.zeros_like(s_odd_ref)
        m_odd_ref[...] = jnp.zeros_like(m_odd_ref)

    def step(s_write_ref, s_read_ref, m_write_ref, m_read_ref):
        kw = jnp.concatenate([kp_ref[0], kc_ref[0], kn_ref[0]], axis=0)
        for sub, bias_ref in enumerate(bias_refs):
            rows = slice(sub * A_TQ, (sub + 1) * A_TQ)
            keys = slice(sub * A_TQ, sub * A_TQ + A_TK)
            for j in range(n_groups):
                lanes = slice(j * A_GROUP * HEAD_DIM, (j + 1) * A_GROUP * HEAD_DIM)
                bias = jnp.concatenate([bias_ref[pr] for pr in range(j * A_GROUP // 2, (j + 1) * A_GROUP // 2)], axis=1)
                sc = _nt_dot(kw[keys, lanes], _masked_heads(q_ref[0, rows, lanes])) + bias
                s_write_ref[sub * n_groups + j] = sc
                m_write_ref[sub * n_groups + j] = jnp.broadcast_to(
                    jnp.max(sc, axis=0, keepdims=True), (SUBLANES, sc.shape[1]))

        vt = jnp.concatenate([vp_ref[0], vc_ref[0], vn_ref[0]], axis=0).T
        ones = jnp.ones((BF16_ROWS, A_TK), BF16)
        for sub in range(A_SUB):
            rows = slice(sub * A_TQ, (sub + 1) * A_TQ)
            keys = slice(sub * A_TQ, sub * A_TQ + A_TK)
            lses = []
            for j in range(n_groups):
                pairs = range(j * A_GROUP // 2, (j + 1) * A_GROUP // 2)
                s = s_read_ref[sub * n_groups + j]
                m = m_read_ref[sub * n_groups + j][0:1]
                pb = jnp.exp2(s - m).astype(BF16)
                o_t = []
                for e in range(A_GROUP):
                    h = A_GROUP * j + e
                    cols = slice(e * A_TQ, (e + 1) * A_TQ)
                    va = jnp.concatenate([vt[h * HEAD_DIM:(h + 1) * HEAD_DIM, keys], ones], axis=0)
                    o = jnp.dot(va, pb[:, cols], preferred_element_type=F32)
                    l = o[HEAD_DIM:HEAD_DIM + 1]
                    o_t.append(o[:HEAD_DIM] / l)
                    lses.append(m[:, cols] + jnp.log2(l))
                for i, pr in enumerate(pairs):
                    tile = jnp.concatenate(o_t[2 * i:2 * i + 2], axis=0)
                    o_ref[0, rows, pr * LANES:(pr + 1) * LANES] = tile.T.astype(BF16)
            lse_t = jnp.concatenate(lses + [jnp.zeros((LANES - A_HEADS, A_TQ), F32)], axis=0)
            lse_ref[0, rows, :] = lse_t.T

    pl.when(t % 2 == 0)(functools.partial(step, s_even_ref, s_odd_ref, m_even_ref, m_odd_ref))
    pl.when(t % 2 == 1)(functools.partial(step, s_odd_ref, s_even_ref, m_odd_ref, m_even_ref))


def _dilated_branch(pav, bias, dil):
    b, l, _ = pav.shape
    step = A_SUB * A_TQ
    assert A_SUB >= 2 and l % step == 0
    nt = l // step
    n_tiles = b * dil * nt
    halo = A_RADIUS
    hb = step // halo
    nblk = PA_W // A_W

    def decode(tt):
        return tt // (dil * nt), (tt // nt) % dil, tt % nt

    def scores_tile(t):
        return decode(jnp.minimum(t, n_tiles - 1))

    def finish_tile(t):
        return decode(jnp.maximum(t - 1, 0))

    def cur(tile, c):
        def index(t):
            bi, r, i = tile(t)
            return bi, i, r * nblk + c
        return pl.BlockSpec((1, step, A_W), index)

    def halo_spec(tile, c, side):
        def index(t):
            bi, r, i = tile(t)
            blk = jnp.maximum(i * hb - 1, 0) if side < 0 else jnp.minimum((i + 1) * hb, nt * hb - 1)
            return bi, blk, r * nblk + c
        return pl.BlockSpec((1, halo, A_W), index)

    def bias_spec(variant):
        def index(t):
            _, _, i = scores_tile(t)
            return variant(i), 0, 0, 0
        return pl.BlockSpec((None, A_HEADS // 2, A_TK, 2 * A_TQ), index)

    def sub_variant(sub):
        def variant(i):
            first = (i == 0).astype(jnp.int32) if sub == 0 else 0
            last = 2 * (i == nt - 1).astype(jnp.int32) if sub == A_SUB - 1 else 0
            return first + last
        return variant

    def out_spec(w):
        def index(t):
            bi, r, i = finish_tile(t)
            return bi, i, r
        return pl.BlockSpec((1, step, w), index)

    n_scores = A_SUB * (A_HEADS // A_GROUP)
    o, lse = pl.pallas_call(
        _dilated_attn_kernel,
        grid=(n_tiles + 1,),
        in_specs=[
            cur(scores_tile, 0),
            halo_spec(scores_tile, 1, -1), cur(scores_tile, 1), halo_spec(scores_tile, 1, 1),
            halo_spec(finish_tile, 2, -1), cur(finish_tile, 2), halo_spec(finish_tile, 2, 1),
        ] + [bias_spec(sub_variant(sub)) for sub in range(A_SUB)],
        out_specs=[out_spec(A_W), out_spec(LANES)],
        out_shape=[
            jax.ShapeDtypeStruct((b, l, dil * A_W), BF16),
            jax.ShapeDtypeStruct((b, l, dil * LANES), F32),
        ],
        scratch_shapes=([pltpu.VMEM((n_scores, A_TK, A_GROUP * A_TQ), F32) for _ in range(2)]
                        + [pltpu.VMEM((n_scores, SUBLANES, A_GROUP * A_TQ), F32) for _ in range(2)]),
        compiler_params=_cparams("arbitrary"),
        name=f"dilated_attn_d{dil}",
    )(*([pav] * 7 + [bias] * A_SUB))
    return o.reshape(b * l, dil * A_W), lse.reshape(b * l, dil * LANES)


def _window_attn_kernel(sink_ref, q_ref, kp_ref, kc_ref, kn_ref, vp_ref, vc_ref, vn_ref, *refs):
    bias_refs = refs[:C_SUB]
    o_ref, s_even_ref, s_odd_ref, m_even_ref, m_odd_ref = refs[C_SUB:]
    t = pl.program_id(0)

    @pl.when(t == 0)
    def _():
        s_odd_ref[...] = jnp.zeros_like(s_odd_ref)
        m_odd_ref[...] = jnp.zeros_like(m_odd_ref)

    def step(s_write_ref, s_read_ref, m_write_ref, m_read_ref):
        kw = jnp.concatenate([kp_ref[0], kc_ref[0], kn_ref[0]], axis=0)
        for sub, bias_ref in enumerate(bias_refs):
            rows = slice(sub * C_TQ, (sub + 1) * C_TQ)
            keys = slice(sub * C_TQ, sub * C_TQ + C_TK)
            for g in range(C_GROUP):
                pairs = (g, g + C_GROUP)
                qg = jnp.concatenate([q_ref[0, rows, pr * LANES:(pr + 1) * LANES] for pr in pairs], axis=1)
                bias = jnp.concatenate([bias_ref[pr] for pr in pairs], axis=1)
                sc = _nt_dot(kw[keys], _masked_heads(qg)) + bias
                s_write_ref[sub * C_GROUP + g] = sc
                m_write_ref[sub * C_GROUP + g] = jnp.broadcast_to(
                    jnp.max(sc, axis=0, keepdims=True), (SUBLANES, sc.shape[1]))

        vt = jnp.concatenate([vp_ref[0], vc_ref[0], vn_ref[0]], axis=0).T
        ones = jnp.ones((BF16_ROWS, C_TK), BF16)
        for sub in range(C_SUB):
            rows = slice(sub * C_TQ, (sub + 1) * C_TQ)
            keys = slice(sub * C_TQ, sub * C_TQ + C_TK)
            for g in range(C_GROUP):
                pairs = (g, g + C_GROUP)
                s = s_read_ref[sub * C_GROUP + g]
                sk = jnp.concatenate(
                    [jnp.full((1, C_TQ), sink_ref[2 * pr + e], F32) for pr in pairs for e in range(2)], axis=1)
                m = jnp.maximum(m_read_ref[sub * C_GROUP + g][0:1], sk)
                pb = jnp.exp2(s - m).astype(BF16)
                sink_p = jnp.exp2(sk - m)
                o_t = []
                for kv in range(C_KV_HEADS):
                    cols = slice(kv * C_TQ, (kv + 1) * C_TQ)
                    va = jnp.concatenate([vt[kv * HEAD_DIM:(kv + 1) * HEAD_DIM, keys], ones], axis=0)
                    o = jnp.dot(va, pb[:, cols], preferred_element_type=F32)
                    den = o[HEAD_DIM:HEAD_DIM + 1] + sink_p[:, cols]
                    o_t.append(o[:HEAD_DIM] / den)
                for i, pr in enumerate(pairs):
                    tile = jnp.concatenate(o_t[2 * i:2 * i + 2], axis=0)
                    o_ref[0, rows, pr * LANES:(pr + 1) * LANES] = tile.T.astype(BF16)

    pl.when(t % 2 == 0)(functools.partial(step, s_even_ref, s_odd_ref, m_even_ref, m_odd_ref))
    pl.when(t % 2 == 1)(functools.partial(step, s_odd_ref, s_even_ref, m_odd_ref, m_even_ref))


def _window_attn(pc, bias, sink):
    b, s, _ = pc.shape
    step = C_SUB * C_TQ
    assert C_SUB >= 2 and s % step == 0
    nt = s // step
    n_tiles = b * nt
    kblk = C_W // C_KV_W

    def scores_tile(t):
        tt = jnp.minimum(t, n_tiles - 1)
        return tt // nt, tt % nt

    def finish_tile(t):
        tt = jnp.maximum(t - 1, 0)
        return tt // nt, tt % nt

    def cur(tile, w, c):
        def index(t):
            bi, i = tile(t)
            return bi, i, c
        return pl.BlockSpec((1, step, w), index)

    def halo_spec(tile, c, side):
        def index(t):
            bi, i = tile(t)
            blk = jnp.maximum(i * C_SUB - 1, 0) if side < 0 else jnp.minimum((i + 1) * C_SUB, nt * C_SUB - 1)
            return bi, blk, c
        return pl.BlockSpec((1, C_TQ, C_KV_W), index)

    def bias_spec(sub):
        def index(t):
            _, i = scores_tile(t)
            first = (i == 0).astype(jnp.int32) if sub == 0 else 0
            last = 2 * (i == nt - 1).astype(jnp.int32) if sub == C_SUB - 1 else 0
            return first + last, 0, 0, 0
        return pl.BlockSpec((None, C_HEADS // 2, C_TK, 2 * C_TQ), index)

    o = pl.pallas_call(
        _window_attn_kernel,
        grid=(n_tiles + 1,),
        in_specs=[
            pl.BlockSpec(memory_space=pltpu.SMEM),
            cur(scores_tile, C_W, 0),
            halo_spec(scores_tile, kblk, -1), cur(scores_tile, C_KV_W, kblk), halo_spec(scores_tile, kblk, 1),
            halo_spec(finish_tile, kblk + 1, -1), cur(finish_tile, C_KV_W, kblk + 1),
            halo_spec(finish_tile, kblk + 1, 1),
        ] + [bias_spec(sub) for sub in range(C_SUB)],
        out_specs=cur(finish_tile, C_W, 0),
        out_shape=jax.ShapeDtypeStruct((b, s, C_W), BF16),
        scratch_shapes=([pltpu.VMEM((C_SUB * C_GROUP, C_TK, C_KV_HEADS * C_TQ), F32) for _ in range(2)]
                        + [pltpu.VMEM((C_SUB * C_GROUP, SUBLANES, C_KV_HEADS * C_TQ), F32) for _ in range(2)]),
        compiler_params=_cparams("arbitrary"),
        name="window_gqa",
    )(sink.astype(F32), *([pc] * 7 + [bias] * C_SUB))
    return o.reshape(b * s, C_W)


def _rope(x, cos, sin, lane_lo):
    parts = []
    for a in range(0, R_QK_W, LANES):
        xh = x[:, a:a + LANES]
        half = R_QK_DIM // 2
        parts.append(jnp.where(lane_lo, pltpu.roll(xh, LANES - half, 1), pltpu.roll(xh, half, 1)))
    return x * cos + jnp.concatenate(parts, axis=1) * sin


def _retention_scores(q, k, cos, sin, decay, qdec, kdec):
    c = R_CHUNK
    lane = lax.broadcasted_iota(jnp.int32, (c, LANES), 1)
    lane_lo = (lane % R_QK_DIM) < (R_QK_DIM // 2)
    q = _rope(q.astype(F32), cos, sin, lane_lo)
    k = _rope(k.astype(F32), cos, sin, lane_lo) * (R_QK_DIM ** -0.5)
    qb = q.astype(BF16)
    head = lax.broadcasted_iota(jnp.int32, (c, R_QK_W), 1) // R_QK_DIM
    qs = jnp.concatenate([jnp.where(head == h, qb, jnp.zeros_like(qb)) for h in range(R_HEADS)], axis=0)
    s = _nt_dot(qs, k.astype(BF16))
    return (s * decay).astype(BF16), (q * qdec).astype(BF16), (k * kdec).astype(BF16)


def _retention_values(p, qd, kd, v, state, rdec, mask):
    c = R_CHUNK
    cross = jnp.dot(qd, state.astype(BF16), preferred_element_type=F32)
    vlane = lax.broadcasted_iota(jnp.int32, (c, R_W), 1) % LANES
    v_lo = jnp.where(vlane < R_V_DIM, v, jnp.zeros_like(v))
    v_hi = jnp.where(vlane >= R_V_DIM, v, jnp.zeros_like(v))
    pieces = []
    for j in range(R_HEADS // 2):
        cs = slice(j * LANES, (j + 1) * LANES)
        a = jnp.dot(p[(2 * j) * c:(2 * j + 1) * c], v_lo[:, cs], preferred_element_type=F32)
        a = a + jnp.dot(p[(2 * j + 1) * c:(2 * j + 2) * c], v_hi[:, cs], preferred_element_type=F32)
        pieces.append(a)
    intra = jnp.concatenate(pieces, axis=1)
    kv = lax.dot_general(kd, v, (((0,), (0,)), ((), ())), preferred_element_type=F32)
    return cross + intra, state * rdec + kv * mask


def _retention_kernel(q_ref, k_ref, cos_ref, sin_ref, decay_ref, qdec_ref, kdec_ref, v_ref, rdec_ref, mask_ref,
                      *refs, n_chunks, steps_per_row, reverse, final):
    if final:
        fwd_ref, gate_ref, avg_ref = refs[:3]
        refs = refs[3:]
    o_ref, state_ref, p_even, p_odd, qd_even, qd_odd, kd_even, kd_odd = refs
    t = pl.program_id(0)

    @pl.when(t == 0)
    def _():
        for ref in (state_ref, p_odd, qd_odd, kd_odd):
            ref[...] = jnp.zeros_like(ref)

    restart = (jnp.maximum(t - 1, 0) % steps_per_row) == 0

    def step(write_refs, read_refs):
        p_w, qd_w, kd_w = write_refs
        p_r, qd_r, kd_r = read_refs
        for ci in range(n_chunks):
            rows = slice(ci * R_CHUNK, (ci + 1) * R_CHUNK)
            p_w[ci], qd_w[ci], kd_w[ci] = _retention_scores(
                q_ref[0, rows, :], k_ref[0, rows, :], cos_ref[rows, :], sin_ref[rows, :],
                decay_ref[...], qdec_ref[...], kdec_ref[...])

        state = jnp.where(restart, jnp.zeros_like(state_ref), state_ref[...])
        for ci in (reversed(range(n_chunks)) if reverse else range(n_chunks)):
            rows = slice(ci * R_CHUNK, (ci + 1) * R_CHUNK)
            o, state = _retention_values(p_r[ci], qd_r[ci], kd_r[ci], v_ref[0, rows, :], state,
                                         rdec_ref[...], mask_ref[...])
            if not final:
                o_ref[0, rows, :] = o
                continue
            o = o + fwd_ref[0, rows, :]
            avg = avg_ref[...]
            d = o - jnp.dot(o.astype(BF16), avg, preferred_element_type=F32)
            y = d * lax.rsqrt(jnp.dot((d * d).astype(BF16), avg, preferred_element_type=F32) + GN_EPS)
            g = gate_ref[0, rows, :].astype(F32)
            o_ref[0, rows, :] = (y * (g / (1.0 + jnp.exp(-g)))).astype(BF16)
        state_ref[...] = state

    even, odd = (p_even, qd_even, kd_even), (p_odd, qd_odd, kd_odd)
    pl.when(t % 2 == 0)(functools.partial(step, even, odd))
    pl.when(t % 2 == 1)(functools.partial(step, odd, even))


def _retention_tables(lg, reverse):
    c = R_CHUNK
    t = np.arange(c)
    diff = (t[None, :] - t[:, None]) if reverse else (t[:, None] - t[None, :])
    keep = (diff > 0) if reverse else (diff >= 0)
    decay = jnp.where(jnp.asarray(keep)[None], jnp.exp(lg[:, None, None] * np.maximum(diff, 0).astype(np.float32)), 0.0)
    tf = t.astype(np.float32)
    qpow = (c - tf) if reverse else (tf + 1.0)
    kpow = tf if reverse else (c - 1.0 - tf)
    qdec = jnp.repeat(jnp.exp(lg[None, :] * qpow[:, None]), R_QK_DIM, axis=1)
    kdec = jnp.repeat(jnp.exp(lg[None, :] * kpow[:, None]), R_QK_DIM, axis=1)
    rdec = jnp.broadcast_to(jnp.repeat(jnp.exp(lg * c), R_QK_DIM)[:, None], (R_QK_W, R_W))
    return decay.reshape(R_HEADS * c, c), qdec, kdec, rdec


def _retention_direction(pb, cos, sin, lg, reverse, final_inputs=None):
    b, s, _ = pb.shape
    rows = R_CHUNK * R_STEP_CHUNKS
    assert s % rows == 0
    nt = s // rows
    n_tiles = b * nt
    final = final_inputs is not None
    hd = np.arange(R_QK_W)[:, None] // R_QK_DIM == np.arange(R_W)[None, :] // R_V_DIM
    mask = jnp.asarray(hd.astype(np.float32))
    decay, qdec, kdec, rdec = _retention_tables(lg, reverse)

    def tile(t, finish):
        tt = jnp.maximum(t - 1, 0) if finish else jnp.minimum(t, n_tiles - 1)
        i = tt % nt
        return tt // nt, (nt - 1 - i) if reverse else i

    def seq(w, cb, finish):
        return pl.BlockSpec((1, rows, w), lambda t: (*tile(t, finish), cb))

    const = lambda shape: pl.BlockSpec(shape, lambda t: (0,) * len(shape))
    tab = pl.BlockSpec((rows, R_QK_W), lambda t: (tile(t, False)[1], 0))
    in_specs = [seq(R_QK_W, 0, False), seq(R_QK_W, 1, False), tab, tab,
                const((R_HEADS * R_CHUNK, R_CHUNK)), const((R_CHUNK, R_QK_W)), const((R_CHUNK, R_QK_W)),
                seq(R_W, 1, True), const((R_QK_W, R_W)), const((R_QK_W, R_W))]
    args = [pb, pb, cos, sin, decay, qdec, kdec, pb, rdec, mask]
    if final:
        o_fwd, avg = final_inputs
        in_specs += [seq(R_W, 0, True), seq(R_W, 2, True), const((R_W, R_W))]
        args += [o_fwd, pb, avg]
    stage = lambda shape: [pltpu.VMEM((R_STEP_CHUNKS,) + shape, BF16) for _ in range(2)]
    return pl.pallas_call(
        functools.partial(_retention_kernel, n_chunks=R_STEP_CHUNKS, steps_per_row=nt, reverse=reverse, final=final),
        grid=(n_tiles + 1,),
        in_specs=in_specs,
        out_specs=seq(R_W, 0, True),
        out_shape=jax.ShapeDtypeStruct((b, s, R_W), BF16 if final else F32),
        scratch_shapes=([pltpu.VMEM((R_QK_W, R_W), F32)] + stage((R_HEADS * R_CHUNK, R_CHUNK))
                        + stage((R_CHUNK, R_QK_W)) + stage((R_CHUNK, R_QK_W))),
        compiler_params=_cparams("arbitrary"),
        name="retention_bwd" if reverse else "retention_fwd",
    )(*args)


def _retention(pb, cos, sin, lg_f, lg_b):
    b, s, _ = pb.shape
    gh = np.arange(R_W)[:, None] // R_V_DIM == np.arange(R_W)[None, :] // R_V_DIM
    avg = jnp.asarray(gh.astype(np.float32) / R_V_DIM, dtype=BF16)
    o_fwd = _retention_direction(pb, cos, sin, lg_f, False)
    out = _retention_direction(pb, cos, sin, lg_b, True, (o_fwd, avg))
    return out.reshape(b * s, R_W)


def _rope_tables(s):
    half = R_QK_DIM // 2
    freqs = ROPE_BASE ** (-jnp.arange(half, dtype=F32) / half)
    ang = jnp.arange(s, dtype=F32)[:, None] * freqs[None]
    cos, sin = jnp.cos(ang), jnp.sin(ang)
    return (jnp.tile(jnp.concatenate([cos, cos], axis=1), (1, R_HEADS)),
            jnp.tile(jnp.concatenate([-sin, sin], axis=1), (1, R_HEADS)))


def _natural_order(o_ref, lse_ref, o_stage, lse_stage, d):
    if d == 1:
        return o_ref[...].astype(F32), lse_ref[...]
    n = o_ref.shape[0]
    for r in range(d):
        lse_stage[pl.ds(r, n, stride=d), :] = lse_ref[:, r * LANES:(r + 1) * LANES]
        for c in range(A_W // LANES):
            col = r * A_W + c * LANES
            o_stage[c, pl.ds(r, n, stride=d), :] = o_ref[:, col:col + LANES].astype(F32)
    return jnp.concatenate([o_stage[c] for c in range(A_W // LANES)], axis=1), lse_stage[...]


def _merge_out_kernel(*refs):
    n_a = len(A_DILATIONS)
    o_refs, lse_refs = refs[:n_a], refs[n_a:2 * n_a]
    r_ref, c_ref, x_ref, w_ref, g_ref, expand_ref, x1_ref, h2_ref = refs[2 * n_a:2 * n_a + 8]
    stages = refs[2 * n_a + 8:]
    outs, lses = [], []
    for b, d in enumerate(A_DILATIONS):
        o, lse = _natural_order(o_refs[b], lse_refs[b], stages[2 * b], stages[2 * b + 1], d)
        outs.append(o)
        lses.append(lse)
    top = functools.reduce(jnp.maximum, lses)
    ws = [jnp.exp2(l - top) for l in lses]
    den = functools.reduce(lambda u, v: u + v, ws)
    expand = expand_ref[...]
    is_head = lax.broadcasted_iota(jnp.int32, den.shape, 1) < A_HEADS
    a = None
    for wgt, o in zip(ws, outs):
        wn = jnp.where(is_head, wgt / den, 0.0)
        hi = wn.astype(BF16).astype(F32)
        packed = hi + pltpu.roll(wn - hi, A_HEADS, 1)
        term = jnp.dot(packed.astype(BF16), expand, preferred_element_type=F32) * o
        a = term if a is None else a + term
    a = a.astype(BF16)
    r = r_ref[...]
    c = c_ref[...]
    nc = OUT_PROJ_COLS
    for n0 in range(0, D_MODEL, nc):
        cols = slice(n0, n0 + nc)
        y = jnp.dot(a, w_ref[0:A_W, cols], preferred_element_type=F32)
        y = y + jnp.dot(r, w_ref[A_W:A_W + R_W, cols], preferred_element_type=F32)
        y = y + jnp.dot(c, w_ref[A_W + R_W:MIX_W, cols], preferred_element_type=F32)
        x1_ref[:, cols] = x_ref[:, cols] + y
    x1 = x1_ref[...]
    h2 = x1 * lax.rsqrt(jnp.mean(x1 * x1, axis=-1, keepdims=True) + EPS) * g_ref[...]
    h2_ref[...] = h2.astype(BF16)


def _merge_out(branches, r, c, x, w, g, tm=PROJ_TOKENS):
    t = x.shape[0]
    assert t % tm == 0 and all(tm % (BF16_ROWS * d) == 0 for d in A_DILATIONS)
    expand = jnp.asarray((np.arange(LANES)[:, None] % A_HEADS == np.arange(A_W)[None, :] // HEAD_DIM)
                         & (np.arange(LANES)[:, None] < 2 * A_HEADS), dtype=BF16)
    row = lambda w_: pl.BlockSpec((tm, w_), lambda i: (i, 0))
    dil_row = lambda w_, d: pl.BlockSpec((tm // d, d * w_), lambda i: (i, 0))
    stages = []
    for _ in A_DILATIONS:
        stages += [pltpu.VMEM((A_W // LANES, tm, LANES), F32), pltpu.VMEM((tm, LANES), F32)]
    return pl.pallas_call(
        _merge_out_kernel,
        grid=(t // tm,),
        in_specs=[dil_row(A_W, d) for d in A_DILATIONS] + [dil_row(LANES, d) for d in A_DILATIONS] + [
            row(R_W), row(C_W), row(D_MODEL),
            pl.BlockSpec((MIX_W, D_MODEL), lambda i: (0, 0), pipeline_mode=pl.Buffered(1)),
            pl.BlockSpec((1, D_MODEL), lambda i: (0, 0)),
            pl.BlockSpec((LANES, A_W), lambda i: (0, 0))],
        out_specs=[row(D_MODEL), row(D_MODEL)],
        out_shape=[jax.ShapeDtypeStruct((t, D_MODEL), F32), jax.ShapeDtypeStruct((t, D_MODEL), BF16)],
        scratch_shapes=stages,
        compiler_params=_cparams("parallel"),
        name="merge_out_proj",
    )(*[o for o, _ in branches], *[l for _, l in branches], r, c, x, w, g.reshape(1, D_MODEL), expand)


def _ffn_accumulate(h_ref, x_ref, wg_ref, wu_ref, wd_ref, o_ref):
    @pl.when(pl.program_id(1) == 0)
    def _():
        o_ref[...] = x_ref[...]

    h = h_ref[...]
    tf = wg_ref.shape[1]
    halves = [slice(c, c + tf // 2) for c in (0, tf // 2)]
    gs = [jnp.dot(h, wg_ref[:, c], preferred_element_type=F32) for c in halves]
    us = [jnp.dot(h, wu_ref[:, c], preferred_element_type=F32) for c in halves]
    y = None
    for g, u, c in zip(gs, us, halves):
        a = (g / (1.0 + jnp.exp(-g)) * u).astype(BF16)
        part = jnp.dot(a, wd_ref[c, :], preferred_element_type=F32)
        y = part if y is None else y + part
    o_ref[...] += y


def _ffn_kernel(h_ref, x_ref, wg_ref, wu_ref, wd_ref, o_ref):
    _ffn_accumulate(h_ref, x_ref, wg_ref, wu_ref, wd_ref, o_ref)


def _ffn_norm_kernel(h_ref, x_ref, wg_ref, wu_ref, wd_ref, g_ref, o_ref):
    _ffn_accumulate(h_ref, x_ref, wg_ref, wu_ref, wd_ref, o_ref)

    @pl.when(pl.program_id(1) == pl.num_programs(1) - 1)
    def _():
        x = o_ref[...]
        o_ref[...] = x * lax.rsqrt(jnp.mean(x * x, axis=-1, keepdims=True) + EPS) * g_ref[...]


def _ffn(h, x, wg, wu, wd, out_norm_g=None, tm=FFN_TOKENS, tf=FFN_COLS):
    t = x.shape[0]
    assert t % tm == 0 and D_FF % tf == 0
    in_specs = [
        pl.BlockSpec((tm, D_MODEL), lambda i, k: (i, 0)),
        pl.BlockSpec((tm, D_MODEL), lambda i, k: (i, 0)),
        pl.BlockSpec((D_MODEL, tf), lambda i, k: (0, k)),
        pl.BlockSpec((D_MODEL, tf), lambda i, k: (0, k)),
        pl.BlockSpec((tf, D_MODEL), lambda i, k: (k, 0)),
    ]
    args = [h, x, wg, wu, wd]
    body = _ffn_kernel
    if out_norm_g is not None:
        in_specs.append(pl.BlockSpec((1, D_MODEL), lambda i, k: (0, 0)))
        args.append(out_norm_g.reshape(1, D_MODEL))
        body = _ffn_norm_kernel
    return pl.pallas_call(
        body,
        grid=(t // tm, D_FF // tf),
        in_specs=in_specs,
        out_specs=pl.BlockSpec((tm, D_MODEL), lambda i, k: (i, 0)),
        out_shape=jax.ShapeDtypeStruct((t, D_MODEL), F32),
        compiler_params=_cparams("parallel", "arbitrary"),
        name="swiglu_ffn",
    )(*args)


def _layer(x, b, s, p, out_norm_g):
    *pas, pb, pc = _norm_proj(x, p["g1"], p["w_in"])
    branches = [_dilated_branch(pa.reshape(b, s // d, d * PA_W), bias, d)
                for pa, bias, d in zip(pas, p["bias_a"], A_DILATIONS)]
    r = _retention(pb.reshape(b, s, PB_W), p["cos"], p["sin"], p["lg_f"], p["lg_b"])
    c = _window_attn(pc.reshape(b, s, PC_W), p["bias_c"], p["sink"])
    x1, h2 = _merge_out(branches, r, c, x, p["w_out"], p["g2"])
    return _ffn(h2, x1, p["w_gate"], p["w_up"], p["w_down"], out_norm_g)


def kernel(x_prompt, x_sample, rel_bias, norm1_g, w_in, ret_decay_fwd, ret_decay_bwd, attn_sink, w_out, norm2_g,
           w_gate, w_up, w_down, final_norm_g):
    bias_a = [_pair_heads(_band_bias(rel_bias[:, :A_HEADS], A_TQ, A_TK, A_RADIUS, A_RADIUS, dil))
              for _, dil in A_BRANCHES]
    q_order = np.asarray(C_Q_ORDER)
    bias_c = _pair_heads(_band_bias(rel_bias[:, A_HEADS:][:, q_order], C_TQ, C_TK, C_TQ, C_RADIUS, 1))

    def reorder_c_heads(w, axis, start):
        take = lambda a, n: lax.slice_in_dim(w, a, a + n, axis=axis)
        heads = [take(start + h * HEAD_DIM, HEAD_DIM) for h in C_Q_ORDER]
        tail = start + C_W
        return jnp.concatenate([take(0, start)] + heads + [take(tail, w.shape[axis] - tail)], axis=axis)

    layers = []
    for i in range(DEPTH):
        layers.append(dict(
            g1=norm1_g[i], g2=norm2_g[i], sink=attn_sink[i].astype(F32)[q_order] * LOG2E, bias_a=bias_a, bias_c=bias_c,
            w_in=reorder_c_heads(w_in[i].astype(BF16), 1, PA_W + PB_W),
            w_out=reorder_c_heads(w_out[i].astype(BF16), 0, A_W + R_W),
            w_gate=w_gate[i].astype(BF16), w_up=w_up[i].astype(BF16), w_down=w_down[i].astype(BF16),
            lg_f=jnp.log1p(-jnp.exp2(-ret_decay_fwd[i].astype(F32))),
            lg_b=jnp.log1p(-jnp.exp2(-ret_decay_bwd[i].astype(F32))),
        ))

    def trunk(x):
        b, s, _ = x.shape
        cos, sin = _rope_tables(s)
        x = x.reshape(b * s, D_MODEL)
        for i, p in enumerate(layers):
            x = _layer(x, b, s, dict(p, cos=cos, sin=sin), final_norm_g if i == DEPTH - 1 else None)
        return x.reshape(b, s, D_MODEL)

    return trunk(x_prompt), trunk(x_sample)
```

```python
import functools
import math

import numpy as np
import jax
import jax.numpy as jnp
from jax import lax
from jax.experimental import pallas as pl
from jax.experimental.pallas import tpu as pltpu

D_MODEL = 2048
DEPTH = 2
HEAD_DIM = 64
A_HEADS = 12
A_BRANCHES = ((128, 1), (512, 4), (2048, 16))
A_RADIUS = 64
R_HEADS = 8
R_QK_DIM = 32
R_V_DIM = 64
R_CHUNK = 128
ROPE_BASE = 10000.0
C_HEADS = 12
C_KV_HEADS = 4
C_GROUP = C_HEADS // C_KV_HEADS
C_RADIUS = 128
REL_BUCKETS = 32
REL_MAX_DIST = 1024
D_FF = 5632
EPS = 1e-6
GN_EPS = 1e-5
NEG = -1e30

A_W = A_HEADS * HEAD_DIM
R_QK_W = R_HEADS * R_QK_DIM
R_W = R_HEADS * R_V_DIM
C_W = C_HEADS * HEAD_DIM
C_KV_W = C_KV_HEADS * HEAD_DIM
PA_W = 3 * A_W
PB_W = 2 * R_QK_W + 2 * R_W
PC_W = C_W + 2 * C_KV_W
IN_COLS = PA_W + PB_W + PC_W
MIX_W = A_W + R_W + C_W

LANES = 128
SUBLANES = 8
BF16_ROWS = 16
V7X_VMEM_BYTES = 64 * 1024 * 1024
VMEM_LIMIT = V7X_VMEM_BYTES - 4 * 1024 * 1024

V7X_MXU_WIDTH = 256
PROJ_TOKENS = 512
FFN_TOKENS = 1024
FFN_COLS = 2 * V7X_MXU_WIDTH
OUT_PROJ_COLS = 2 * V7X_MXU_WIDTH
CAST_ROWS = 256

A_TQ = 128
A_TK = A_TQ + 2 * A_RADIUS
A_SUB = 4
C_TQ = 128
C_TK = 3 * C_TQ
C_SUB = 4
C_Q_ORDER = tuple(kp * C_GROUP + j + e * C_GROUP
                  for kp in range(0, C_KV_HEADS, 2) for j in range(C_GROUP) for e in range(2))
R_STEP_CHUNKS = 8

LOG2E = 1.4426950408889634
QSCALE = HEAD_DIM ** -0.5 * LOG2E

BF16 = jnp.bfloat16
F32 = jnp.float32


def _cparams(*sem):
    return pltpu.CompilerParams(dimension_semantics=sem, vmem_limit_bytes=VMEM_LIMIT)


def _t5_bucket(rel):
    nb = REL_BUCKETS // 2
    max_exact = nb // 2
    ret = np.where(rel > 0, nb, 0)
    n = np.abs(rel)
    nf = np.maximum(n, 1).astype(np.float32)
    large = max_exact + (np.log(nf / max_exact) / math.log(REL_MAX_DIST / max_exact) * (nb - max_exact)).astype(np.int32)
    large = np.minimum(large, nb - 1)
    return (ret + np.where(n < max_exact, n, large)).astype(np.int32)


def _band_bias(table, tq, tk, lead, radius, dil):
    h = table.shape[1]
    n = tq + tk - 1
    offs = np.arange(n) - (tq - 1) - lead
    per_off = jnp.take(table.astype(F32), jnp.asarray(_t5_bucket(offs * dil)), axis=0).T * LOG2E
    padded = jnp.concatenate([per_off, jnp.zeros((h, 1), F32)], axis=1)
    toep = jnp.tile(padded, (1, tq))[:, :tq * n].reshape(h, tq, n)[:, :, tq - 1:tq - 1 + tk]
    off = np.arange(tk)[None, :] - lead - np.arange(tq)[:, None]
    band = np.abs(off) <= radius
    col = np.arange(tk)[None, :]
    first = band & (col >= lead)
    last = band & (col < tk - lead)
    masks = np.stack([band, first, last, first & last])
    return jnp.where(jnp.asarray(masks)[:, None], toep[None], NEG).swapaxes(-1, -2)


def _pair_heads(bias):
    v, h, tk, tq = bias.shape
    return bias.reshape(v, h // 2, 2, tk, tq).swapaxes(2, 3).reshape(v, h // 2, tk, 2 * tq)


def _proj_chunks():
    chunks = [(0, i * A_W, i * A_W, A_W, QSCALE if i == 0 else None) for i in range(PA_W // A_W)]
    width = 2 * V7X_MXU_WIDTH
    chunks += [(1, c, PA_W + c, width, None) for c in range(0, PB_W, width)]
    chunks += [(2, 0, PA_W + PB_W, C_W, QSCALE), (2, C_W, PA_W + PB_W + C_W, 2 * C_KV_W, None)]
    return tuple(chunks)


_PROJ_CHUNKS = _proj_chunks()


A_DILATIONS = tuple(d for _, d in A_BRANCHES)


def _norm_proj_kernel(x_ref, g_ref, w_ref, *refs):
    n_a = len(A_DILATIONS)
    pa_refs, (pb_ref, pc_ref), stage_refs = refs[:n_a], refs[n_a:n_a + 2], refs[n_a + 2:]
    tm = x_ref.shape[0]
    x = x_ref[...]
    h = (x * lax.rsqrt(jnp.mean(x * x, axis=-1, keepdims=True) + EPS) * g_ref[...]).astype(BF16)
    outs = (None, pb_ref, pc_ref)
    a_chunk = 0
    for oi, oc, wc, width, scale in _PROJ_CHUNKS:
        y = jnp.dot(h, w_ref[:, wc:wc + width], preferred_element_type=F32)
        if scale is not None:
            y = y * scale
        if oi != 0:
            outs[oi][:, oc:oc + width] = y.astype(BF16)
            continue
        stage, stage_mid = stage_refs[2 * a_chunk], stage_refs[2 * a_chunk + 1]
        a_chunk += 1
        _, d1, d2 = A_DILATIONS
        slabs = range(width // LANES)
        for c in slabs:
            stage[c] = y[:, c * LANES:(c + 1) * LANES]
        pa_refs[0][:, oc:oc + width] = y.astype(BF16)
        for r in range(d1):
            for c in slabs:
                part = stage[c, pl.ds(r, tm // d1, stride=d1), :]
                stage_mid[c, r] = part
                col = r * PA_W + oc + c * LANES
                pa_refs[1][:, col:col + LANES] = part.astype(BF16)
        for r in range(d1):
            for a in range(d2 // d1):
                for c in slabs:
                    col = (a * d1 + r) * PA_W + oc + c * LANES
                    part = stage_mid[c, r, pl.ds(a, tm // d2, stride=d2 // d1), :]
                    pa_refs[2][:, col:col + LANES] = part.astype(BF16)


def _norm_proj(x, g, w, tm=PROJ_TOKENS):
    t = x.shape[0]
    assert t % tm == 0 and all(tm % (BF16_ROWS * d) == 0 for d in A_DILATIONS)
    d0, d1, d2 = A_DILATIONS
    assert d0 == 1 and d2 % d1 == 0
    n_a_chunks = sum(1 for c in _PROJ_CHUNKS if c[0] == 0)
    stages = []
    for _ in range(n_a_chunks):
        stages += [pltpu.VMEM((A_W // LANES, tm, LANES), F32), pltpu.VMEM((A_W // LANES, d1, tm // d1, LANES), F32)]
    return pl.pallas_call(
        _norm_proj_kernel,
        grid=(t // tm,),
        in_specs=[
            pl.BlockSpec((tm, D_MODEL), lambda i: (i, 0)),
            pl.BlockSpec((1, D_MODEL), lambda i: (0, 0)),
            pl.BlockSpec((D_MODEL, IN_COLS), lambda i: (0, 0), pipeline_mode=pl.Buffered(1)),
        ],
        out_specs=[pl.BlockSpec((tm // d, d * PA_W), lambda i: (i, 0)) for d in A_DILATIONS] + [
            pl.BlockSpec((tm, PB_W), lambda i: (i, 0)),
            pl.BlockSpec((tm, PC_W), lambda i: (i, 0)),
        ],
        out_shape=[jax.ShapeDtypeStruct((t // d, d * PA_W), BF16) for d in A_DILATIONS] + [
            jax.ShapeDtypeStruct((t, PB_W), BF16),
            jax.ShapeDtypeStruct((t, PC_W), BF16),
        ],
        scratch_shapes=stages,
        compiler_params=_cparams("parallel"),
        name="norm_in_proj",
    )(x, g.reshape(1, D_MODEL), w)


def _nt_dot(a, b):
    return lax.dot_general(a, b, (((1,), (1,)), ((), ())), preferred_element_type=F32)


def _masked_heads(qg):
    head = lax.broadcasted_iota(jnp.int32, qg.shape, 1) // HEAD_DIM
    zero = jnp.zeros_like(qg)
    return jnp.concatenate([jnp.where(head == e, qg, zero) for e in range(qg.shape[1] // HEAD_DIM)], axis=0)


A_GROUP = 4


def _dilated_attn_kernel(q_ref, kp_ref, kc_ref, kn_ref, vp_ref, vc_ref, vn_ref, *refs):
    bias_refs = refs[:A_SUB]
    o_ref, lse_ref, s_even_ref, s_odd_ref, m_even_ref, m_odd_ref = refs[A_SUB:]
    t = pl.program_id(0)
    n_groups = A_HEADS // A_GROUP

    @pl.when(t == 0)
    def _():
        s_odd_ref[...] = jnp.zeros_like(s_odd_ref)
        m_odd_ref[...] = jnp.zeros_like(m_odd_ref)

    def step(s_write_ref, s_read_ref, m_write_ref, m_read_ref):
        kw = jnp.concatenate([kp_ref[0], kc_ref[0], kn_ref[0]], axis=0)
        for sub, bias_ref in enumerate(bias_refs):
            rows = slice(sub * A_TQ, (sub + 1) * A_TQ)
            keys = slice(sub * A_TQ, sub * A_TQ + A_TK)
            for j in range(n_groups):
                lanes = slice(j * A_GROUP * HEAD_DIM, (j + 1) * A_GROUP * HEAD_DIM)
                bias = jnp.concatenate([bias_ref[pr] for pr in range(j * A_GROUP // 2, (j + 1) * A_GROUP // 2)], axis=1)
                sc = _nt_dot(kw[keys, lanes], _masked_heads(q_ref[0, rows, lanes])) + bias
                s_write_ref[sub * n_groups + j] = sc
                m_write_ref[sub * n_groups + j] = jnp.broadcast_to(
                    jnp.max(sc, axis=0, keepdims=True), (SUBLANES, sc.shape[1]))

        vt = jnp.concatenate([vp_ref[0], vc_ref[0], vn_ref[0]], axis=0).T
        ones = jnp.ones((BF16_ROWS, A_TK), BF16)
        for sub in range(A_SUB):
            rows = slice(sub * A_TQ, (sub + 1) * A_TQ)
            keys = slice(sub * A_TQ, sub * A_TQ + A_TK)
            lses = []
            for j in range(n_groups):
                pairs = range(j * A_GROUP // 2, (j + 1) * A_GROUP // 2)
                s = s_read_ref[sub * n_groups + j]
                m = m_read_ref[sub * n_groups + j][0:1]
                pb = jnp.exp2(s - m).astype(BF16)
                o_t = []
                for e in range(A_GROUP):
                    h = A_GROUP * j + e
                    cols = slice(e * A_TQ, (e + 1) * A_TQ)
                    va = jnp.concatenate([vt[h * HEAD_DIM:(h + 1) * HEAD_DIM, keys], ones], axis=0)
                    o = jnp.dot(va, pb[:, cols], preferred_element_type=F32)
                    l = o[HEAD_DIM:HEAD_DIM + 1]
                    o_t.append(o[:HEAD_DIM] / l)
                    lses.append(m[:, cols] + jnp.log2(l))
                for i, pr in enumerate(pairs):
                    tile = jnp.concatenate(o_t[2 * i:2 * i + 2], axis=0)
                    o_ref[0, rows, pr * LANES:(pr + 1) * LANES] = tile.T.astype(BF16)
            lse_t = jnp.concatenate(lses + [jnp.zeros((LANES - A_HEADS, A_TQ), F32)], axis=0)
            lse_ref[0, rows, :] = lse_t.T

    pl.when(t % 2 == 0)(functools.partial(step, s_even_ref, s_odd_ref, m_even_ref, m_odd_ref))
    pl.when(t % 2 == 1)(functools.partial(step, s_odd_ref, s_even_ref, m_odd_ref, m_even_ref))


def _dilated_branch(pav, bias, dil):
    b, l, _ = pav.shape
    step = A_SUB * A_TQ
    assert A_SUB >= 2 and l % step == 0
    nt = l // step
    n_tiles = b * dil * nt
    halo = A_RADIUS
    hb = step // halo
    nblk = PA_W // A_W

    def decode(tt):
        return tt // (dil * nt), (tt // nt) % dil, tt % nt

    def scores_tile(t):
        return decode(jnp.minimum(t, n_tiles - 1))

    def finish_tile(t):
        return decode(jnp.maximum(t - 1, 0))

    def cur(tile, c):
        def index(t):
            bi, r, i = tile(t)
            return bi, i, r * nblk + c
        return pl.BlockSpec((1, step, A_W), index)

    def halo_spec(tile, c, side):
        def index(t):
            bi, r, i = tile(t)
            blk = jnp.maximum(i * hb - 1, 0) if side < 0 else jnp.minimum((i + 1) * hb, nt * hb - 1)
            return bi, blk, r * nblk + c
        return pl.BlockSpec((1, halo, A_W), index)

    def bias_spec(variant):
        def index(t):
            _, _, i = scores_tile(t)
            return variant(i), 0, 0, 0
        return pl.BlockSpec((None, A_HEADS // 2, A_TK, 2 * A_TQ), index)

    def sub_variant(sub):
        def variant(i):
            first = (i == 0).astype(jnp.int32) if sub == 0 else 0
            last = 2 * (i == nt - 1).astype(jnp.int32) if sub == A_SUB - 1 else 0
            return first + last
        return variant

    def out_spec(w):
        def index(t):
            bi, r, i = finish_tile(t)
            return bi, i, r
        return pl.BlockSpec((1, step, w), index)

    n_scores = A_SUB * (A_HEADS // A_GROUP)
    o, lse = pl.pallas_call(
        _dilated_attn_kernel,
        grid=(n_tiles + 1,),
        in_specs=[
            cur(scores_tile, 0),
            halo_spec(scores_tile, 1, -1), cur(scores_tile, 1), halo_spec(scores_tile, 1, 1),
            halo_spec(finish_tile, 2, -1), cur(finish_tile, 2), halo_spec(finish_tile, 2, 1),
        ] + [bias_spec(sub_variant(sub)) for sub in range(A_SUB)],
        out_specs=[out_spec(A_W), out_spec(LANES)],
        out_shape=[
            jax.ShapeDtypeStruct((b, l, dil * A_W), BF16),
            jax.ShapeDtypeStruct((b, l, dil * LANES), F32),
        ],
        scratch_shapes=([pltpu.VMEM((n_scores, A_TK, A_GROUP * A_TQ), F32) for _ in range(2)]
                        + [pltpu.VMEM((n_scores, SUBLANES, A_GROUP * A_TQ), F32) for _ in range(2)]),
        compiler_params=_cparams("arbitrary"),
        name=f"dilated_attn_d{dil}",
    )(*([pav] * 7 + [bias] * A_SUB))
    return o.reshape(b * l, dil * A_W), lse.reshape(b * l, dil * LANES)


def _window_attn_kernel(sink_ref, q_ref, kp_ref, kc_ref, kn_ref, vp_ref, vc_ref, vn_ref, *refs):
    bias_refs = refs[:C_SUB]
    o_ref, s_even_ref, s_odd_ref, m_even_ref, m_odd_ref = refs[C_SUB:]
    t = pl.program_id(0)

    @pl.when(t == 0)
    def _():
        s_odd_ref[...] = jnp.zeros_like(s_odd_ref)
        m_odd_ref[...] = jnp.zeros_like(m_odd_ref)

    def step(s_write_ref, s_read_ref, m_write_ref, m_read_ref):
        kw = jnp.concatenate([kp_ref[0], kc_ref[0], kn_ref[0]], axis=0)
        for sub, bias_ref in enumerate(bias_refs):
            rows = slice(sub * C_TQ, (sub + 1) * C_TQ)
            keys = slice(sub * C_TQ, sub * C_TQ + C_TK)
            for g in range(C_GROUP):
                pairs = (g, g + C_GROUP)
                qg = jnp.concatenate([q_ref[0, rows, pr * LANES:(pr + 1) * LANES] for pr in pairs], axis=1)
                bias = jnp.concatenate([bias_ref[pr] for pr in pairs], axis=1)
                sc = _nt_dot(kw[keys], _masked_heads(qg)) + bias
                s_write_ref[sub * C_GROUP + g] = sc
                m_write_ref[sub * C_GROUP + g] = jnp.broadcast_to(
                    jnp.max(sc, axis=0, keepdims=True), (SUBLANES, sc.shape[1]))

        vt = jnp.concatenate([vp_ref[0], vc_ref[0], vn_ref[0]], axis=0).T
        ones = jnp.ones((BF16_ROWS, C_TK), BF16)
        for sub in range(C_SUB):
            rows = slice(sub * C_TQ, (sub + 1) * C_TQ)
            keys = slice(sub * C_TQ, sub * C_TQ + C_TK)
            for g in range(C_GROUP):
                pairs = (g, g + C_GROUP)
                s = s_read_ref[sub * C_GROUP + g]
                sk = jnp.concatenate(
                    [jnp.full((1, C_TQ), sink_ref[2 * pr + e], F32) for pr in pairs for e in range(2)], axis=1)
                m = jnp.maximum(m_read_ref[sub * C_GROUP + g][0:1], sk)
                pb = jnp.exp2(s - m).astype(BF16)
                sink_p = jnp.exp2(sk - m)
                o_t = []
                for kv in range(C_KV_HEADS):
                    cols = slice(kv * C_TQ, (kv + 1) * C_TQ)
                    va = jnp.concatenate([vt[kv * HEAD_DIM:(kv + 1) * HEAD_DIM, keys], ones], axis=0)
                    o = jnp.dot(va, pb[:, cols], preferred_element_type=F32)
                    den = o[HEAD_DIM:HEAD_DIM + 1] + sink_p[:, cols]
                    o_t.append(o[:HEAD_DIM] / den)
                for i, pr in enumerate(pairs):
                    tile = jnp.concatenate(o_t[2 * i:2 * i + 2], axis=0)
                    o_ref[0, rows, pr * LANES:(pr + 1) * LANES] = tile.T.astype(BF16)

    pl.when(t % 2 == 0)(functools.partial(step, s_even_ref, s_odd_ref, m_even_ref, m_odd_ref))
    pl.when(t % 2 == 1)(functools.partial(step, s_odd_ref, s_even_ref, m_odd_ref, m_even_ref))


def _window_attn(pc, bias, sink):
    b, s, _ = pc.shape
    step = C_SUB * C_TQ
    assert C_SUB >= 2 and s % step == 0
    nt = s // step
    n_tiles = b * nt
    kblk = C_W // C_KV_W

    def scores_tile(t):
        tt = jnp.minimum(t, n_tiles - 1)
        return tt // nt, tt % nt

    def finish_tile(t):
        tt = jnp.maximum(t - 1, 0)
        return tt // nt, tt % nt

    def cur(tile, w, c):
        def index(t):
            bi, i = tile(t)
            return bi, i, c
        return pl.BlockSpec((1, step, w), index)

    def halo_spec(tile, c, side):
        def index(t):
            bi, i = tile(t)
            blk = jnp.maximum(i * C_SUB - 1, 0) if side < 0 else jnp.minimum((i + 1) * C_SUB, nt * C_SUB - 1)
            return bi, blk, c
        return pl.BlockSpec((1, C_TQ, C_KV_W), index)

    def bias_spec(sub):
        def index(t):
            _, i = scores_tile(t)
            first = (i == 0).astype(jnp.int32) if sub == 0 else 0
            last = 2 * (i == nt - 1).astype(jnp.int32) if sub == C_SUB - 1 else 0
            return first + last, 0, 0, 0
        return pl.BlockSpec((None, C_HEADS // 2, C_TK, 2 * C_TQ), index)

    o = pl.pallas_call(
        _window_attn_kernel,
        grid=(n_tiles + 1,),
        in_specs=[
            pl.BlockSpec(memory_space=pltpu.SMEM),
            cur(scores_tile, C_W, 0),
            halo_spec(scores_tile, kblk, -1), cur(scores_tile, C_KV_W, kblk), halo_spec(scores_tile, kblk, 1),
            halo_spec(finish_tile, kblk + 1, -1), cur(finish_tile, C_KV_W, kblk + 1),
            halo_spec(finish_tile, kblk + 1, 1),
        ] + [bias_spec(sub) for sub in range(C_SUB)],
        out_specs=cur(finish_tile, C_W, 0),
        out_shape=jax.ShapeDtypeStruct((b, s, C_W), BF16),
        scratch_shapes=([pltpu.VMEM((C_SUB * C_GROUP, C_TK, C_KV_HEADS * C_TQ), F32) for _ in range(2)]
                        + [pltpu.VMEM((C_SUB * C_GROUP, SUBLANES, C_KV_HEADS * C_TQ), F32) for _ in range(2)]),
        compiler_params=_cparams("arbitrary"),
        name="window_gqa",
    )(sink.astype(F32), *([pc] * 7 + [bias] * C_SUB))
    return o.reshape(b * s, C_W)


def _rope(x, cos, sin, lane_lo):
    parts = []
    for a in range(0, R_QK_W, LANES):
        xh = x[:, a:a + LANES]
        half = R_QK_DIM // 2
        parts.append(jnp.where(lane_lo, pltpu.roll(xh, LANES - half, 1), pltpu.roll(xh, half, 1)))
    return x * cos + jnp.concatenate(parts, axis=1) * sin


def _retention_scores(q, k, cos, sin, decay, qdec, kdec):
    c = R_CHUNK
    lane = lax.broadcasted_iota(jnp.int32, (c, LANES), 1)
    lane_lo = (lane % R_QK_DIM) < (R_QK_DIM // 2)
    q = _rope(q.astype(F32), cos, sin, lane_lo)
    k = _rope(k.astype(F32), cos, sin, lane_lo) * (R_QK_DIM ** -0.5)
    qb = q.astype(BF16)
    head = lax.broadcasted_iota(jnp.int32, (c, R_QK_W), 1) // R_QK_DIM
    qs = jnp.concatenate([jnp.where(head == h, qb, jnp.zeros_like(qb)) for h in range(R_HEADS)], axis=0)
    s = _nt_dot(qs, k.astype(BF16))
    return (s * decay).astype(BF16), (q * qdec).astype(BF16), (k * kdec).astype(BF16)


def _retention_values(p, qd, kd, v, state, rdec, mask):
    c = R_CHUNK
    cross = jnp.dot(qd, state.astype(BF16), preferred_element_type=F32)
    vlane = lax.broadcasted_iota(jnp.int32, (c, R_W), 1) % LANES
    v_lo = jnp.where(vlane < R_V_DIM, v, jnp.zeros_like(v))
    v_hi = jnp.where(vlane >= R_V_DIM, v, jnp.zeros_like(v))
    pieces = []
    for j in range(R_HEADS // 2):
        cs = slice(j * LANES, (j + 1) * LANES)
        a = jnp.dot(p[(2 * j) * c:(2 * j + 1) * c], v_lo[:, cs], preferred_element_type=F32)
        a = a + jnp.dot(p[(2 * j + 1) * c:(2 * j + 2) * c], v_hi[:, cs], preferred_element_type=F32)
        pieces.append(a)
    intra = jnp.concatenate(pieces, axis=1)
    kv = lax.dot_general(kd, v, (((0,), (0,)), ((), ())), preferred_element_type=F32)
    return cross + intra, state * rdec + kv * mask


def _retention_kernel(q_ref, k_ref, cos_ref, sin_ref, decay_ref, qdec_ref, kdec_ref, v_ref, rdec_ref, mask_ref,
                      *refs, n_chunks, steps_per_row, reverse, final):
    if final:
        fwd_ref, gate_ref, avg_ref = refs[:3]
        refs = refs[3:]
    o_ref, state_ref, p_even, p_odd, qd_even, qd_odd, kd_even, kd_odd = refs
    t = pl.program_id(0)

    @pl.when(t == 0)
    def _():
        for ref in (state_ref, p_odd, qd_odd, kd_odd):
            ref[...] = jnp.zeros_like(ref)

    restart = (jnp.maximum(t - 1, 0) % steps_per_row) == 0

    def step(write_refs, read_refs):
        p_w, qd_w, kd_w = write_refs
        p_r, qd_r, kd_r = read_refs
        for ci in range(n_chunks):
            rows = slice(ci * R_CHUNK, (ci + 1) * R_CHUNK)
            p_w[ci], qd_w[ci], kd_w[ci] = _retention_scores(
                q_ref[0, rows, :], k_ref[0, rows, :], cos_ref[rows, :], sin_ref[rows, :],
                decay_ref[...], qdec_ref[...], kdec_ref[...])

        state = jnp.where(restart, jnp.zeros_like(state_ref), state_ref[...])
        for ci in (reversed(range(n_chunks)) if reverse else range(n_chunks)):
            rows = slice(ci * R_CHUNK, (ci + 1) * R_CHUNK)
            o, state = _retention_values(p_r[ci], qd_r[ci], kd_r[ci], v_ref[0, rows, :], state,
                                         rdec_ref[...], mask_ref[...])
            if not final:
                o_ref[0, rows, :] = o
                continue
            o = o + fwd_ref[0, rows, :]
            avg = avg_ref[...]
            d = o - jnp.dot(o.astype(BF16), avg, preferred_element_type=F32)
            y = d * lax.rsqrt(jnp.dot((d * d).astype(BF16), avg, preferred_element_type=F32) + GN_EPS)
            g = gate_ref[0, rows, :].astype(F32)
            o_ref[0, rows, :] = (y * (g / (1.0 + jnp.exp(-g)))).astype(BF16)
        state_ref[...] = state

    even, odd = (p_even, qd_even, kd_even), (p_odd, qd_odd, kd_odd)
    pl.when(t % 2 == 0)(functools.partial(step, even, odd))
    pl.when(t % 2 == 1)(functools.partial(step, odd, even))


def _retention_tables(lg, reverse):
    c = R_CHUNK
    t = np.arange(c)
    diff = (t[None, :] - t[:, None]) if reverse else (t[:, None] - t[None, :])
    keep = (diff > 0) if reverse else (diff >= 0)
    decay = jnp.where(jnp.asarray(keep)[None], jnp.exp(lg[:, None, None] * np.maximum(diff, 0).astype(np.float32)), 0.0)
    tf = t.astype(np.float32)
    qpow = (c - tf) if reverse else (tf + 1.0)
    kpow = tf if reverse else (c - 1.0 - tf)
    qdec = jnp.repeat(jnp.exp(lg[None, :] * qpow[:, None]), R_QK_DIM, axis=1)
    kdec = jnp.repeat(jnp.exp(lg[None, :] * kpow[:, None]), R_QK_DIM, axis=1)
    rdec = jnp.broadcast_to(jnp.repeat(jnp.exp(lg * c), R_QK_DIM)[:, None], (R_QK_W, R_W))
    return decay.reshape(R_HEADS * c, c), qdec, kdec, rdec


def _retention_direction(pb, cos, sin, lg, reverse, final_inputs=None):
    b, s, _ = pb.shape
    rows = R_CHUNK * R_STEP_CHUNKS
    assert s % rows == 0
    nt = s // rows
    n_tiles = b * nt
    final = final_inputs is not None
    hd = np.arange(R_QK_W)[:, None] // R_QK_DIM == np.arange(R_W)[None, :] // R_V_DIM
    mask = jnp.asarray(hd.astype(np.float32))
    decay, qdec, kdec, rdec = _retention_tables(lg, reverse)

    def tile(t, finish):
        tt = jnp.maximum(t - 1, 0) if finish else jnp.minimum(t, n_tiles - 1)
        i = tt % nt
        return tt // nt, (nt - 1 - i) if reverse else i

    def seq(w, cb, finish):
        return pl.BlockSpec((1, rows, w), lambda t: (*tile(t, finish), cb))

    const = lambda shape: pl.BlockSpec(shape, lambda t: (0,) * len(shape))
    tab = pl.BlockSpec((rows, R_QK_W), lambda t: (tile(t, False)[1], 0))
    in_specs = [seq(R_QK_W, 0, False), seq(R_QK_W, 1, False), tab, tab,
                const((R_HEADS * R_CHUNK, R_CHUNK)), const((R_CHUNK, R_QK_W)), const((R_CHUNK, R_QK_W)),
                seq(R_W, 1, True), const((R_QK_W, R_W)), const((R_QK_W, R_W))]
    args = [pb, pb, cos, sin, decay, qdec, kdec, pb, rdec, mask]
    if final:
        o_fwd, avg = final_inputs
        in_specs += [seq(R_W, 0, True), seq(R_W, 2, True), const((R_W, R_W))]
        args += [o_fwd, pb, avg]
    stage = lambda shape: [pltpu.VMEM((R_STEP_CHUNKS,) + shape, BF16) for _ in range(2)]
    return pl.pallas_call(
        functools.partial(_retention_kernel, n_chunks=R_STEP_CHUNKS, steps_per_row=nt, reverse=reverse, final=final),
        grid=(n_tiles + 1,),
        in_specs=in_specs,
        out_specs=seq(R_W, 0, True),
        out_shape=jax.ShapeDtypeStruct((b, s, R_W), BF16 if final else F32),
        scratch_shapes=([pltpu.VMEM((R_QK_W, R_W), F32)] + stage((R_HEADS * R_CHUNK, R_CHUNK))
                        + stage((R_CHUNK, R_QK_W)) + stage((R_CHUNK, R_QK_W))),
        compiler_params=_cparams("arbitrary"),
        name="retention_bwd" if reverse else "retention_fwd",
    )(*args)


def _retention(pb, cos, sin, lg_f, lg_b):
    b, s, _ = pb.shape
    gh = np.arange(R_W)[:, None] // R_V_DIM == np.arange(R_W)[None, :] // R_V_DIM
    avg = jnp.asarray(gh.astype(np.float32) / R_V_DIM, dtype=BF16)
    o_fwd = _retention_direction(pb, cos, sin, lg_f, False)
    out = _retention_direction(pb, cos, sin, lg_b, True, (o_fwd, avg))
    return out.reshape(b * s, R_W)


def _rope_tables(s):
    half = R_QK_DIM // 2
    freqs = ROPE_BASE ** (-jnp.arange(half, dtype=F32) / half)
    ang = jnp.arange(s, dtype=F32)[:, None] * freqs[None]
    cos, sin = jnp.cos(ang), jnp.sin(ang)
    return (jnp.tile(jnp.concatenate([cos, cos], axis=1), (1, R_HEADS)),
            jnp.tile(jnp.concatenate([-sin, sin], axis=1), (1, R_HEADS)))


def _natural_order(o_ref, lse_ref, o_stage, lse_stage, d):
    if d == 1:
        return o_ref[...].astype(F32), lse_ref[...]
    n = o_ref.shape[0]
    for r in range(d):
        lse_stage[pl.ds(r, n, stride=d), :] = lse_ref[:, r * LANES:(r + 1) * LANES]
        for c in range(A_W // LANES):
            col = r * A_W + c * LANES
            o_stage[c, pl.ds(r, n, stride=d), :] = o_ref[:, col:col + LANES].astype(F32)
    return jnp.concatenate([o_stage[c] for c in range(A_W // LANES)], axis=1), lse_stage[...]


def _merge_out_kernel(*refs):
    n_a = len(A_DILATIONS)
    o_refs, lse_refs = refs[:n_a], refs[n_a:2 * n_a]
    r_ref, c_ref, x_ref, w_ref, g_ref, expand_ref, x1_ref, h2_ref = refs[2 * n_a:2 * n_a + 8]
    stages = refs[2 * n_a + 8:]
    outs, lses = [], []
    for b, d in enumerate(A_DILATIONS):
        o, lse = _natural_order(o_refs[b], lse_refs[b], stages[2 * b], stages[2 * b + 1], d)
        outs.append(o)
        lses.append(lse)
    top = functools.reduce(jnp.maximum, lses)
    ws = [jnp.exp2(l - top) for l in lses]
    den = functools.reduce(lambda u, v: u + v, ws)
    expand = expand_ref[...]
    is_head = lax.broadcasted_iota(jnp.int32, den.shape, 1) < A_HEADS
    a = None
    for wgt, o in zip(ws, outs):
        wn = jnp.where(is_head, wgt / den, 0.0)
        hi = wn.astype(BF16).astype(F32)
        packed = hi + pltpu.roll(wn - hi, A_HEADS, 1)
        term = jnp.dot(packed.astype(BF16), expand, preferred_element_type=F32) * o
        a = term if a is None else a + term
    a = a.astype(BF16)
    r = r_ref[...]
    c = c_ref[...]
    nc = OUT_PROJ_COLS
    for n0 in range(0, D_MODEL, nc):
        cols = slice(n0, n0 + nc)
        y = jnp.dot(a, w_ref[0:A_W, cols], preferred_element_type=F32)
        y = y + jnp.dot(r, w_ref[A_W:A_W + R_W, cols], preferred_element_type=F32)
        y = y + jnp.dot(c, w_ref[A_W + R_W:MIX_W, cols], preferred_element_type=F32)
        x1_ref[:, cols] = x_ref[:, cols] + y
    x1 = x1_ref[...]
    h2 = x1 * lax.rsqrt(jnp.mean(x1 * x1, axis=-1, keepdims=True) + EPS) * g_ref[...]
    h2_ref[...] = h2.astype(BF16)


def _merge_out(branches, r, c, x, w, g, tm=PROJ_TOKENS):
    t = x.shape[0]
    assert t % tm == 0 and all(tm % (BF16_ROWS * d) == 0 for d in A_DILATIONS)
    expand = jnp.asarray((np.arange(LANES)[:, None] % A_HEADS == np.arange(A_W)[None, :] // HEAD_DIM)
                         & (np.arange(LANES)[:, None] < 2 * A_HEADS), dtype=BF16)
    row = lambda w_: pl.BlockSpec((tm, w_), lambda i: (i, 0))
    dil_row = lambda w_, d: pl.BlockSpec((tm // d, d * w_), lambda i: (i, 0))
    stages = []
    for _ in A_DILATIONS:
        stages += [pltpu.VMEM((A_W // LANES, tm, LANES), F32), pltpu.VMEM((tm, LANES), F32)]
    return pl.pallas_call(
        _merge_out_kernel,
        grid=(t // tm,),
        in_specs=[dil_row(A_W, d) for d in A_DILATIONS] + [dil_row(LANES, d) for d in A_DILATIONS] + [
            row(R_W), row(C_W), row(D_MODEL),
            pl.BlockSpec((MIX_W, D_MODEL), lambda i: (0, 0), pipeline_mode=pl.Buffered(1)),
            pl.BlockSpec((1, D_MODEL), lambda i: (0, 0)),
            pl.BlockSpec((LANES, A_W), lambda i: (0, 0))],
        out_specs=[row(D_MODEL), row(D_MODEL)],
        out_shape=[jax.ShapeDtypeStruct((t, D_MODEL), F32), jax.ShapeDtypeStruct((t, D_MODEL), BF16)],
        scratch_shapes=stages,
        compiler_params=_cparams("parallel"),
        name="merge_out_proj",
    )(*[o for o, _ in branches], *[l for _, l in branches], r, c, x, w, g.reshape(1, D_MODEL), expand)


def _ffn_accumulate(h_ref, x_ref, wg_ref, wu_ref, wd_ref, o_ref):
    @pl.when(pl.program_id(1) == 0)
    def _():
        o_ref[...] = x_ref[...]

    h = h_ref[...]
    tf = wg_ref.shape[1]
    halves = [slice(c, c + tf // 2) for c in (0, tf // 2)]
    gs = [jnp.dot(h, wg_ref[:, c], preferred_element_type=F32) for c in halves]
    us = [jnp.dot(h, wu_ref[:, c], preferred_element_type=F32) for c in halves]
    y = None
    for g, u, c in zip(gs, us, halves):
        a = (g / (1.0 + jnp.exp(-g)) * u).astype(BF16)
        part = jnp.dot(a, wd_ref[c, :], preferred_element_type=F32)
        y = part if y is None else y + part
    o_ref[...] += y


def _ffn_kernel(h_ref, x_ref, wg_ref, wu_ref, wd_ref, o_ref):
    _ffn_accumulate(h_ref, x_ref, wg_ref, wu_ref, wd_ref, o_ref)


def _ffn_norm_kernel(h_ref, x_ref, wg_ref, wu_ref, wd_ref, g_ref, o_ref):
    _ffn_accumulate(h_ref, x_ref, wg_ref, wu_ref, wd_ref, o_ref)

    @pl.when(pl.program_id(1) == pl.num_programs(1) - 1)
    def _():
        x = o_ref[...]
        o_ref[...] = x * lax.rsqrt(jnp.mean(x * x, axis=-1, keepdims=True) + EPS) * g_ref[...]


def _ffn(h, x, wg, wu, wd, out_norm_g=None, tm=FFN_TOKENS, tf=FFN_COLS):
    t = x.shape[0]
    assert t % tm == 0 and D_FF % tf == 0
    in_specs = [
        pl.BlockSpec((tm, D_MODEL), lambda i, k: (i, 0)),
        pl.BlockSpec((tm, D_MODEL), lambda i, k: (i, 0)),
        pl.BlockSpec((D_MODEL, tf), lambda i, k: (0, k)),
        pl.BlockSpec((D_MODEL, tf), lambda i, k: (0, k)),
        pl.BlockSpec((tf, D_MODEL), lambda i, k: (k, 0)),
    ]
    args = [h, x, wg, wu, wd]
    body = _ffn_kernel
    if out_norm_g is not None:
        in_specs.append(pl.BlockSpec((1, D_MODEL), lambda i, k: (0, 0)))
        args.append(out_norm_g.reshape(1, D_MODEL))
        body = _ffn_norm_kernel
    return pl.pallas_call(
        body,
        grid=(t // tm, D_FF // tf),
        in_specs=in_specs,
        out_specs=pl.BlockSpec((tm, D_MODEL), lambda i, k: (i, 0)),
        out_shape=jax.ShapeDtypeStruct((t, D_MODEL), F32),
        compiler_params=_cparams("parallel", "arbitrary"),
        name="swiglu_ffn",
    )(*args)


def _cast_kernel(w_ref, o_ref):
    o_ref[...] = w_ref[...].astype(BF16)


def _layer_weight_bf16(w, layer):
    _, r, c = w.shape
    tr = CAST_ROWS
    assert r % tr == 0
    return pl.pallas_call(
        _cast_kernel,
        grid=(r // tr,),
        in_specs=[pl.BlockSpec((None, tr, c), lambda i: (layer, i, 0))],
        out_specs=pl.BlockSpec((tr, c), lambda i: (i, 0)),
        out_shape=jax.ShapeDtypeStruct((r, c), BF16),
        compiler_params=_cparams("parallel"),
        name="weight_to_bf16",
    )(w)


def _layer(x, b, s, p, out_norm_g):
    *pas, pb, pc = _norm_proj(x, p["g1"], p["w_in"])
    branches = [_dilated_branch(pa.reshape(b, s // d, d * PA_W), bias, d)
                for pa, bias, d in zip(pas, p["bias_a"], A_DILATIONS)]
    r = _retention(pb.reshape(b, s, PB_W), p["cos"], p["sin"], p["lg_f"], p["lg_b"])
    c = _window_attn(pc.reshape(b, s, PC_W), p["bias_c"], p["sink"])
    x1, h2 = _merge_out(branches, r, c, x, p["w_out"], p["g2"])
    return _ffn(h2, x1, p["w_gate"], p["w_up"], p["w_down"], out_norm_g)


def kernel(x_prompt, x_sample, rel_bias, norm1_g, w_in, ret_decay_fwd, ret_decay_bwd, attn_sink, w_out, norm2_g,
           w_gate, w_up, w_down, final_norm_g):
    bias_a = [_pair_heads(_band_bias(rel_bias[:, :A_HEADS], A_TQ, A_TK, A_RADIUS, A_RADIUS, dil))
              for _, dil in A_BRANCHES]
    q_order = np.asarray(C_Q_ORDER)
    bias_c = _pair_heads(_band_bias(rel_bias[:, A_HEADS:][:, q_order], C_TQ, C_TK, C_TQ, C_RADIUS, 1))

    def reorder_c_heads(w, axis, start):
        take = lambda a, n: lax.slice_in_dim(w, a, a + n, axis=axis)
        heads = [take(start + h * HEAD_DIM, HEAD_DIM) for h in C_Q_ORDER]
        tail = start + C_W
        return jnp.concatenate([take(0, start)] + heads + [take(tail, w.shape[axis] - tail)], axis=axis)

    layers = []
    for i in range(DEPTH):
        layers.append(dict(
            g1=norm1_g[i], g2=norm2_g[i], sink=attn_sink[i].astype(F32)[q_order] * LOG2E, bias_a=bias_a, bias_c=bias_c,
            w_in=reorder_c_heads(_layer_weight_bf16(w_in, i), 1, PA_W + PB_W),
            w_out=reorder_c_heads(_layer_weight_bf16(w_out, i), 0, A_W + R_W),
            w_gate=_layer_weight_bf16(w_gate, i), w_up=_layer_weight_bf16(w_up, i),
            w_down=_layer_weight_bf16(w_down, i),
            lg_f=jnp.log1p(-jnp.exp2(-ret_decay_fwd[i].astype(F32))),
            lg_b=jnp.log1p(-jnp.exp2(-ret_decay_bwd[i].astype(F32))),
        ))

    def trunk(x):
        b, s, _ = x.shape
        cos, sin = _rope_tables(s)
        x = x.reshape(b * s, D_MODEL)
        for i, p in enumerate(layers):
            x = _layer(x, b, s, dict(p, cos=cos, sin=sin), final_norm_g if i == DEPTH - 1 else None)
        return x.reshape(b, s, D_MODEL)

    return trunk(x_prompt), trunk(x_sample)
```

```python
import functools
import math

import numpy as np
import jax
import jax.numpy as jnp
from jax import lax
from jax.experimental import pallas as pl
from jax.experimental.pallas import tpu as pltpu

D_MODEL = 2048
DEPTH = 2
HEAD_DIM = 64
A_HEADS = 12
A_BRANCHES = ((128, 1), (512, 4), (2048, 16))
A_RADIUS = 64
R_HEADS = 8
R_QK_DIM = 32
R_V_DIM = 64
R_CHUNK = 128
ROPE_BASE = 10000.0
C_HEADS = 12
C_KV_HEADS = 4
C_GROUP = C_HEADS // C_KV_HEADS
C_RADIUS = 128
REL_BUCKETS = 32
REL_MAX_DIST = 1024
D_FF = 5632
EPS = 1e-6
GN_EPS = 1e-5
NEG = -1e30

A_W = A_HEADS * HEAD_DIM
R_QK_W = R_HEADS * R_QK_DIM
R_W = R_HEADS * R_V_DIM
C_W = C_HEADS * HEAD_DIM
C_KV_W = C_KV_HEADS * HEAD_DIM
PA_W = 3 * A_W
PB_W = 2 * R_QK_W + 2 * R_W
PC_W = C_W + 2 * C_KV_W
IN_COLS = PA_W + PB_W + PC_W
MIX_W = A_W + R_W + C_W

LANES = 128
SUBLANES = 8
BF16_ROWS = 16
V7X_VMEM_BYTES = 64 * 1024 * 1024
VMEM_LIMIT = V7X_VMEM_BYTES - 4 * 1024 * 1024

V7X_MXU_WIDTH = 256
PROJ_TOKENS = 512
FFN_TOKENS = 1024
FFN_COLS = 2 * V7X_MXU_WIDTH
OUT_PROJ_COLS = 2 * V7X_MXU_WIDTH
CAST_ROWS = 256

A_TQ = 128
A_TK = A_TQ + 2 * A_RADIUS
A_SUB = 4
C_TQ = 128
C_TK = 3 * C_TQ
C_SUB = 4
C_Q_ORDER = tuple(kp * C_GROUP + j + e * C_GROUP
                  for kp in range(0, C_KV_HEADS, 2) for j in range(C_GROUP) for e in range(2))
R_STEP_CHUNKS = 8

LOG2E = 1.4426950408889634
QSCALE = HEAD_DIM ** -0.5 * LOG2E

BF16 = jnp.bfloat16
F32 = jnp.float32


def _cparams(*sem):
    return pltpu.CompilerParams(dimension_semantics=sem, vmem_limit_bytes=VMEM_LIMIT)


def _t5_bucket(rel):
    nb = REL_BUCKETS // 2
    max_exact = nb // 2
    ret = np.where(rel > 0, nb, 0)
    n = np.abs(rel)
    nf = np.maximum(n, 1).astype(np.float32)
    large = max_exact + (np.log(nf / max_exact) / math.log(REL_MAX_DIST / max_exact) * (nb - max_exact)).astype(np.int32)
    large = np.minimum(large, nb - 1)
    return (ret + np.where(n < max_exact, n, large)).astype(np.int32)


def _band_bias(table, tq, tk, lead, radius, dil):
    h = table.shape[1]
    n = tq + tk - 1
    offs = np.arange(n) - (tq - 1) - lead
    per_off = jnp.take(table.astype(F32), jnp.asarray(_t5_bucket(offs * dil)), axis=0).T * LOG2E
    padded = jnp.concatenate([per_off, jnp.zeros((h, 1), F32)], axis=1)
    toep = jnp.tile(padded, (1, tq))[:, :tq * n].reshape(h, tq, n)[:, :, tq - 1:tq - 1 + tk]
    off = np.arange(tk)[None, :] - lead - np.arange(tq)[:, None]
    band = np.abs(off) <= radius
    col = np.arange(tk)[None, :]
    first = band & (col >= lead)
    last = band & (col < tk - lead)
    masks = np.stack([band, first, last, first & last])
    return jnp.where(jnp.asarray(masks)[:, None], toep[None], NEG).swapaxes(-1, -2)


def _pair_heads(bias):
    v, h, tk, tq = bias.shape
    return bias.reshape(v, h // 2, 2, tk, tq).swapaxes(2, 3).reshape(v, h // 2, tk, 2 * tq)


def _proj_chunks():
    chunks = [(0, i * A_W, i * A_W, A_W, QSCALE if i == 0 else None) for i in range(PA_W // A_W)]
    width = 2 * V7X_MXU_WIDTH
    chunks += [(1, c, PA_W + c, width, None) for c in range(0, PB_W, width)]
    chunks += [(2, 0, PA_W + PB_W, C_W, QSCALE), (2, C_W, PA_W + PB_W + C_W, 2 * C_KV_W, None)]
    return tuple(chunks)


_PROJ_CHUNKS = _proj_chunks()


A_DILATIONS = tuple(d for _, d in A_BRANCHES)


def _norm_proj_kernel(x_ref, g_ref, w_ref, *refs):
    n_a = len(A_DILATIONS)
    pa_refs, (pb_ref, pc_ref), stage_refs = refs[:n_a], refs[n_a:n_a + 2], refs[n_a + 2:]
    tm = x_ref.shape[0]
    x = x_ref[...]
    h = (x * lax.rsqrt(jnp.mean(x * x, axis=-1, keepdims=True) + EPS) * g_ref[...]).astype(BF16)
    outs = (None, pb_ref, pc_ref)
    a_chunk = 0
    for oi, oc, wc, width, scale in _PROJ_CHUNKS:
        y = jnp.dot(h, w_ref[:, wc:wc + width], preferred_element_type=F32)
        if scale is not None:
            y = y * scale
        if oi != 0:
            outs[oi][:, oc:oc + width] = y.astype(BF16)
            continue
        stage, stage_mid = stage_refs[2 * a_chunk], stage_refs[2 * a_chunk + 1]
        a_chunk += 1
        _, d1, d2 = A_DILATIONS
        slabs = range(width // LANES)
        for c in slabs:
            stage[c] = y[:, c * LANES:(c + 1) * LANES]
        pa_refs[0][:, oc:oc + width] = y.astype(BF16)
        for r in range(d1):
            for c in slabs:
                part = stage[c, pl.ds(r, tm // d1, stride=d1), :]
                stage_mid[c, r] = part
                col = r * PA_W + oc + c * LANES
                pa_refs[1][:, col:col + LANES] = part.astype(BF16)
        for r in range(d1):
            for a in range(d2 // d1):
                for c in slabs:
                    col = (a * d1 + r) * PA_W + oc + c * LANES
                    part = stage_mid[c, r, pl.ds(a, tm // d2, stride=d2 // d1), :]
                    pa_refs[2][:, col:col + LANES] = part.astype(BF16)


def _norm_proj(x, g, w, tm=PROJ_TOKENS):
    t = x.shape[0]
    assert t % tm == 0 and all(tm % (BF16_ROWS * d) == 0 for d in A_DILATIONS)
    d0, d1, d2 = A_DILATIONS
    assert d0 == 1 and d2 % d1 == 0
    n_a_chunks = sum(1 for c in _PROJ_CHUNKS if c[0] == 0)
    stages = []
    for _ in range(n_a_chunks):
        stages += [pltpu.VMEM((A_W // LANES, tm, LANES), F32), pltpu.VMEM((A_W // LANES, d1, tm // d1, LANES), F32)]
    return pl.pallas_call(
        _norm_proj_kernel,
        grid=(t // tm,),
        in_specs=[
            pl.BlockSpec((tm, D_MODEL), lambda i: (i, 0)),
            pl.BlockSpec((1, D_MODEL), lambda i: (0, 0)),
            pl.BlockSpec((D_MODEL, IN_COLS), lambda i: (0, 0), pipeline_mode=pl.Buffered(1)),
        ],
        out_specs=[pl.BlockSpec((tm // d, d * PA_W), lambda i: (i, 0)) for d in A_DILATIONS] + [
            pl.BlockSpec((tm, PB_W), lambda i: (i, 0)),
            pl.BlockSpec((tm, PC_W), lambda i: (i, 0)),
        ],
        out_shape=[jax.ShapeDtypeStruct((t // d, d * PA_W), BF16) for d in A_DILATIONS] + [
            jax.ShapeDtypeStruct((t, PB_W), BF16),
            jax.ShapeDtypeStruct((t, PC_W), BF16),
        ],
        scratch_shapes=stages,
        compiler_params=_cparams("parallel"),
        name="norm_in_proj",
    )(x, g.reshape(1, D_MODEL), w)


def _nt_dot(a, b):
    return lax.dot_general(a, b, (((1,), (1,)), ((), ())), preferred_element_type=F32)


def _masked_heads(qg):
    head = lax.broadcasted_iota(jnp.int32, qg.shape, 1) // HEAD_DIM
    zero = jnp.zeros_like(qg)
    return jnp.concatenate([jnp.where(head == e, qg, zero) for e in range(qg.shape[1] // HEAD_DIM)], axis=0)


A_GROUP = 4


def _dilated_attn_kernel(q_ref, kp_ref, kc_ref, kn_ref, vp_ref, vc_ref, vn_ref, *refs):
    bias_refs = refs[:A_SUB]
    o_ref, lse_ref, s_even_ref, s_odd_ref, m_even_ref, m_odd_ref = refs[A_SUB:]
    t = pl.program_id(0)
    n_groups = A_HEADS // A_GROUP

    @pl.when(t == 0)
    def _():
        s_odd_ref[...] = jnp.zeros_like(s_odd_ref)
        m_odd_ref[...] = jnp.zeros_like(m_odd_ref)

    def step(s_write_ref, s_read_ref, m_write_ref, m_read_ref):
        kw = jnp.concatenate([kp_ref[0], kc_ref[0], kn_ref[0]], axis=0)
        for sub, bias_ref in enumerate(bias_refs):
            rows = slice(sub * A_TQ, (sub + 1) * A_TQ)
            keys = slice(sub * A_TQ, sub * A_TQ + A_TK)
            for j in range(n_groups):
                lanes = slice(j * A_GROUP * HEAD_DIM, (j + 1) * A_GROUP * HEAD_DIM)
                bias = jnp.concatenate([bias_ref[pr] for pr in range(j * A_GROUP // 2, (j + 1) * A_GROUP // 2)], axis=1)
                sc = _nt_dot(kw[keys, lanes], _masked_heads(q_ref[0, rows, lanes])) + bias
                s_write_ref[sub * n_groups + j] = sc
                m_write_ref[sub * n_groups + j] = jnp.broadcast_to(
                    jnp.max(sc, axis=0, keepdims=True), (SUBLANES, sc.shape[1]))

        vt = jnp.concatenate([vp_ref[0], vc_ref[0], vn_ref[0]], axis=0).T
        ones = jnp.ones((BF16_ROWS, A_TK), BF16)
        for sub in range(A_SUB):
            rows = slice(sub * A_TQ, (sub + 1) * A_TQ)
            keys = slice(sub * A_TQ, sub * A_TQ + A_TK)
            lses = []
            for j in range(n_groups):
                pairs = range(j * A_GROUP // 2, (j + 1) * A_GROUP // 2)
                s = s_read_ref[sub * n_groups + j]
                m = m_read_ref[sub * n_groups + j][0:1]
                pb = jnp.exp2(s - m).astype(BF16)
                o_t = []
                for e in range(A_GROUP):
                    h = A_GROUP * j + e
                    cols = slice(e * A_TQ, (e + 1) * A_TQ)
                    va = jnp.concatenate([vt[h * HEAD_DIM:(h + 1) * HEAD_DIM, keys], ones], axis=0)
                    o = jnp.dot(va, pb[:, cols], preferred_element_type=F32)
                    l = o[HEAD_DIM:HEAD_DIM + 1]
                    o_t.append(o[:HEAD_DIM] / l)
                    lses.append(m[:, cols] + jnp.log2(l))
                for i, pr in enumerate(pairs):
                    tile = jnp.concatenate(o_t[2 * i:2 * i + 2], axis=0)
                    o_ref[0, rows, pr * LANES:(pr + 1) * LANES] = tile.T.astype(BF16)
            lse_t = jnp.concatenate(lses + [jnp.zeros((LANES - A_HEADS, A_TQ), F32)], axis=0)
            lse_ref[0, rows, :] = lse_t.T

    pl.when(t % 2 == 0)(functools.partial(step, s_even_ref, s_odd_ref, m_even_ref, m_odd_ref))
    pl.when(t % 2 == 1)(functools.partial(step, s_odd_ref, s_even_ref, m_odd_ref, m_even_ref))


def _dilated_branch(pav, bias, dil):
    b, l, _ = pav.shape
    step = A_SUB * A_TQ
    assert A_SUB >= 2 and l % step == 0
    nt = l // step
    n_tiles = b * dil * nt
    halo = A_RADIUS
    hb = step // halo
    nblk = PA_W // A_W

    def decode(tt):
        return tt // (dil * nt), (tt // nt) % dil, tt % nt

    def scores_tile(t):
        return decode(jnp.minimum(t, n_tiles - 1))

    def finish_tile(t):
        return decode(jnp.maximum(t - 1, 0))

    def cur(tile, c):
        def index(t):
            bi, r, i = tile(t)
            return bi, i, r * nblk + c
        return pl.BlockSpec((1, step, A_W), index)

    def halo_spec(tile, c, side):
        def index(t):
            bi, r, i = tile(t)
            blk = jnp.maximum(i * hb - 1, 0) if side < 0 else jnp.minimum((i + 1) * hb, nt * hb - 1)
            return bi, blk, r * nblk + c
        return pl.BlockSpec((1, halo, A_W), index)

    def bias_spec(variant):
        def index(t):
            _, _, i = scores_tile(t)
            return variant(i), 0, 0, 0
        return pl.BlockSpec((None, A_HEADS // 2, A_TK, 2 * A_TQ), index)

    def sub_variant(sub):
        def variant(i):
            first = (i == 0).astype(jnp.int32) if sub == 0 else 0
            last = 2 * (i == nt - 1).astype(jnp.int32) if sub == A_SUB - 1 else 0
            return first + last
        return variant

    def out_spec(w):
        def index(t):
            bi, r, i = finish_tile(t)
            return bi, i, r
        return pl.BlockSpec((1, step, w), index)

    n_scores = A_SUB * (A_HEADS // A_GROUP)
    o, lse = pl.pallas_call(
        _dilated_attn_kernel,
        grid=(n_tiles + 1,),
        in_specs=[
            cur(scores_tile, 0),
            halo_spec(scores_tile, 1, -1), cur(scores_tile, 1), halo_spec(scores_tile, 1, 1),
            halo_spec(finish_tile, 2, -1), cur(finish_tile, 2), halo_spec(finish_tile, 2, 1),
        ] + [bias_spec(sub_variant(sub)) for sub in range(A_SUB)],
        out_specs=[out_spec(A_W), out_spec(LANES)],
        out_shape=[
            jax.ShapeDtypeStruct((b, l, dil * A_W), BF16),
            jax.ShapeDtypeStruct((b, l, dil * LANES), F32),
        ],
        scratch_shapes=([pltpu.VMEM((n_scores, A_TK, A_GROUP * A_TQ), F32) for _ in range(2)]
                        + [pltpu.VMEM((n_scores, SUBLANES, A_GROUP * A_TQ), F32) for _ in range(2)]),
        compiler_params=_cparams("arbitrary"),
        name=f"dilated_attn_d{dil}",
    )(*([pav] * 7 + [bias] * A_SUB))
    return o.reshape(b * l, dil * A_W), lse.reshape(b * l, dil * LANES)


def _window_attn_kernel(sink_ref, q_ref, kp_ref, kc_ref, kn_ref, vp_ref, vc_ref, vn_ref, *refs):
    bias_refs = refs[:C_SUB]
    o_ref, s_even_ref, s_odd_ref, m_even_ref, m_odd_ref = refs[C_SUB:]
    t = pl.program_id(0)

    @pl.when(t == 0)
    def _():
        s_odd_ref[...] = jnp.zeros_like(s_odd_ref)
        m_odd_ref[...] = jnp.zeros_like(m_odd_ref)

    def step(s_write_ref, s_read_ref, m_write_ref, m_read_ref):
        kw = jnp.concatenate([kp_ref[0], kc_ref[0], kn_ref[0]], axis=0)
        for sub, bias_ref in enumerate(bias_refs):
            rows = slice(sub * C_TQ, (sub + 1) * C_TQ)
            keys = slice(sub * C_TQ, sub * C_TQ + C_TK)
            for g in range(C_GROUP):
                pairs = (g, g + C_GROUP)
                qg = jnp.concatenate([q_ref[0, rows, pr * LANES:(pr + 1) * LANES] for pr in pairs], axis=1)
                bias = jnp.concatenate([bias_ref[pr] for pr in pairs], axis=1)
                sc = _nt_dot(kw[keys], _masked_heads(qg)) + bias
                s_write_ref[sub * C_GROUP + g] = sc
                m_write_ref[sub * C_GROUP + g] = jnp.broadcast_to(
                    jnp.max(sc, axis=0, keepdims=True), (SUBLANES, sc.shape[1]))

        vt = jnp.concatenate([vp_ref[0], vc_ref[0], vn_ref[0]], axis=0).T
        ones = jnp.ones((BF16_ROWS, C_TK), BF16)
        for sub in range(C_SUB):
            rows = slice(sub * C_TQ, (sub + 1) * C_TQ)
            keys = slice(sub * C_TQ, sub * C_TQ + C_TK)
            for g in range(C_GROUP):
                pairs = (g, g + C_GROUP)
                s = s_read_ref[sub * C_GROUP + g]
                sk = jnp.concatenate(
                    [jnp.full((1, C_TQ), sink_ref[2 * pr + e], F32) for pr in pairs for e in range(2)], axis=1)
                m = jnp.maximum(m_read_ref[sub * C_GROUP + g][0:1], sk)
                pb = jnp.exp2(s - m).astype(BF16)
                sink_p = jnp.exp2(sk - m)
                o_t = []
                for kv in range(C_KV_HEADS):
                    cols = slice(kv * C_TQ, (kv + 1) * C_TQ)
                    va = jnp.concatenate([vt[kv * HEAD_DIM:(kv + 1) * HEAD_DIM, keys], ones], axis=0)
                    o = jnp.dot(va, pb[:, cols], preferred_element_type=F32)
                    den = o[HEAD_DIM:HEAD_DIM + 1] + sink_p[:, cols]
                    o_t.append(o[:HEAD_DIM] / den)
                for i, pr in enumerate(pairs):
                    tile = jnp.concatenate(o_t[2 * i:2 * i + 2], axis=0)
                    o_ref[0, rows, pr * LANES:(pr + 1) * LANES] = tile.T.astype(BF16)

    pl.when(t % 2 == 0)(functools.partial(step, s_even_ref, s_odd_ref, m_even_ref, m_odd_ref))
    pl.when(t % 2 == 1)(functools.partial(step, s_odd_ref, s_even_ref, m_odd_ref, m_even_ref))


def _window_attn(pc, bias, sink):
    b, s, _ = pc.shape
    step = C_SUB * C_TQ
    assert C_SUB >= 2 and s % step == 0
    nt = s // step
    n_tiles = b * nt
    kblk = C_W // C_KV_W

    def scores_tile(t):
        tt = jnp.minimum(t, n_tiles - 1)
        return tt // nt, tt % nt

    def finish_tile(t):
        tt = jnp.maximum(t - 1, 0)
        return tt // nt, tt % nt

    def cur(tile, w, c):
        def index(t):
            bi, i = tile(t)
            return bi, i, c
        return pl.BlockSpec((1, step, w), index)

    def halo_spec(tile, c, side):
        def index(t):
            bi, i = tile(t)
            blk = jnp.maximum(i * C_SUB - 1, 0) if side < 0 else jnp.minimum((i + 1) * C_SUB, nt * C_SUB - 1)
            return bi, blk, c
        return pl.BlockSpec((1, C_TQ, C_KV_W), index)

    def bias_spec(sub):
        def index(t):
            _, i = scores_tile(t)
            first = (i == 0).astype(jnp.int32) if sub == 0 else 0
            last = 2 * (i == nt - 1).astype(jnp.int32) if sub == C_SUB - 1 else 0
            return first + last, 0, 0, 0
        return pl.BlockSpec((None, C_HEADS // 2, C_TK, 2 * C_TQ), index)

    o = pl.pallas_call(
        _window_attn_kernel,
        grid=(n_tiles + 1,),
        in_specs=[
            pl.BlockSpec(memory_space=pltpu.SMEM),
            cur(scores_tile, C_W, 0),
            halo_spec(scores_tile, kblk, -1), cur(scores_tile, C_KV_W, kblk), halo_spec(scores_tile, kblk, 1),
            halo_spec(finish_tile, kblk + 1, -1), cur(finish_tile, C_KV_W, kblk + 1),
            halo_spec(finish_tile, kblk + 1, 1),
        ] + [bias_spec(sub) for sub in range(C_SUB)],
        out_specs=cur(finish_tile, C_W, 0),
        out_shape=jax.ShapeDtypeStruct((b, s, C_W), BF16),
        scratch_shapes=([pltpu.VMEM((C_SUB * C_GROUP, C_TK, C_KV_HEADS * C_TQ), F32) for _ in range(2)]
                        + [pltpu.VMEM((C_SUB * C_GROUP, SUBLANES, C_KV_HEADS * C_TQ), F32) for _ in range(2)]),
        compiler_params=_cparams("arbitrary"),
        name="window_gqa",
    )(sink.astype(F32), *([pc] * 7 + [bias] * C_SUB))
    return o.reshape(b * s, C_W)


def _rope(x, cos, sin, lane_lo):
    parts = []
    for a in range(0, R_QK_W, LANES):
        xh = x[:, a:a + LANES]
        half = R_QK_DIM // 2
        parts.append(jnp.where(lane_lo, pltpu.roll(xh, LANES - half, 1), pltpu.roll(xh, half, 1)))
    return x * cos + jnp.concatenate(parts, axis=1) * sin


def _retention_scores(q, k, cos, sin, decay, qdec, kdec):
    c = R_CHUNK
    lane = lax.broadcasted_iota(jnp.int32, (c, LANES), 1)
    lane_lo = (lane % R_QK_DIM) < (R_QK_DIM // 2)
    q = _rope(q.astype(F32), cos, sin, lane_lo)
    k = _rope(k.astype(F32), cos, sin, lane_lo) * (R_QK_DIM ** -0.5)
    qb = q.astype(BF16)
    head = lax.broadcasted_iota(jnp.int32, (c, R_QK_W), 1) // R_QK_DIM
    qs = jnp.concatenate([jnp.where(head == h, qb, jnp.zeros_like(qb)) for h in range(R_HEADS)], axis=0)
    s = _nt_dot(qs, k.astype(BF16))
    return (s * decay).astype(BF16), (q * qdec).astype(BF16), (k * kdec).astype(BF16)


def _retention_values(p, qd, kd, v, state, rdec, mask):
    c = R_CHUNK
    cross = jnp.dot(qd, state.astype(BF16), preferred_element_type=F32)
    vlane = lax.broadcasted_iota(jnp.int32, (c, R_W), 1) % LANES
    v_lo = jnp.where(vlane < R_V_DIM, v, jnp.zeros_like(v))
    v_hi = jnp.where(vlane >= R_V_DIM, v, jnp.zeros_like(v))
    pieces = []
    for j in range(R_HEADS // 2):
        cs = slice(j * LANES, (j + 1) * LANES)
        a = jnp.dot(p[(2 * j) * c:(2 * j + 1) * c], v_lo[:, cs], preferred_element_type=F32)
        a = a + jnp.dot(p[(2 * j + 1) * c:(2 * j + 2) * c], v_hi[:, cs], preferred_element_type=F32)
        pieces.append(a)
    intra = jnp.concatenate(pieces, axis=1)
    kv = lax.dot_general(kd, v, (((0,), (0,)), ((), ())), preferred_element_type=F32)
    return cross + intra, state * rdec + kv * mask


def _retention_kernel(q_ref, k_ref, cos_ref, sin_ref, decay_ref, qdec_ref, kdec_ref, v_ref, rdec_ref, mask_ref,
                      *refs, n_chunks, steps_per_row, reverse, final):
    if final:
        fwd_ref, gate_ref, avg_ref = refs[:3]
        refs = refs[3:]
    o_ref, state_ref, p_even, p_odd, qd_even, qd_odd, kd_even, kd_odd = refs
    t = pl.program_id(0)

    @pl.when(t == 0)
    def _():
        for ref in (state_ref, p_odd, qd_odd, kd_odd):
            ref[...] = jnp.zeros_like(ref)

    restart = (jnp.maximum(t - 1, 0) % steps_per_row) == 0

    def step(write_refs, read_refs):
        p_w, qd_w, kd_w = write_refs
        p_r, qd_r, kd_r = read_refs
        for ci in range(n_chunks):
            rows = slice(ci * R_CHUNK, (ci + 1) * R_CHUNK)
            p_w[ci], qd_w[ci], kd_w[ci] = _retention_scores(
                q_ref[0, rows, :], k_ref[0, rows, :], cos_ref[rows, :], sin_ref[rows, :],
                decay_ref[...], qdec_ref[...], kdec_ref[...])

        state = jnp.where(restart, jnp.zeros_like(state_ref), state_ref[...])
        for ci in (reversed(range(n_chunks)) if reverse else range(n_chunks)):
            rows = slice(ci * R_CHUNK, (ci + 1) * R_CHUNK)
            o, state = _retention_values(p_r[ci], qd_r[ci], kd_r[ci], v_ref[0, rows, :], state,
                                         rdec_ref[...], mask_ref[...])
            if not final:
                o_ref[0, rows, :] = o
                continue
            o = o + fwd_ref[0, rows, :]
            avg = avg_ref[...]
            d = o - jnp.dot(o.astype(BF16), avg, preferred_element_type=F32)
            y = d * lax.rsqrt(jnp.dot((d * d).astype(BF16), avg, preferred_element_type=F32) + GN_EPS)
            g = gate_ref[0, rows, :].astype(F32)
            o_ref[0, rows, :] = (y * (g / (1.0 + jnp.exp(-g)))).astype(BF16)
        state_ref[...] = state

    even, odd = (p_even, qd_even, kd_even), (p_odd, qd_odd, kd_odd)
    pl.when(t % 2 == 0)(functools.partial(step, even, odd))
    pl.when(t % 2 == 1)(functools.partial(step, odd, even))


def _retention_tables(lg, reverse):
    c = R_CHUNK
    t = np.arange(c)
    diff = (t[None, :] - t[:, None]) if reverse else (t[:, None] - t[None, :])
    keep = (diff > 0) if reverse else (diff >= 0)
    decay = jnp.where(jnp.asarray(keep)[None], jnp.exp(lg[:, None, None] * np.maximum(diff, 0).astype(np.float32)), 0.0)
    tf = t.astype(np.float32)
    qpow = (c - tf) if reverse else (tf + 1.0)
    kpow = tf if reverse else (c - 1.0 - tf)
    qdec = jnp.repeat(jnp.exp(lg[None, :] * qpow[:, None]), R_QK_DIM, axis=1)
    kdec = jnp.repeat(jnp.exp(lg[None, :] * kpow[:, None]), R_QK_DIM, axis=1)
    rdec = jnp.broadcast_to(jnp.repeat(jnp.exp(lg * c), R_QK_DIM)[:, None], (R_QK_W, R_W))
    return decay.reshape(R_HEADS * c, c), qdec, kdec, rdec


def _retention_direction(pb, cos, sin, lg, reverse, final_inputs=None):
    b, s, _ = pb.shape
    rows = R_CHUNK * R_STEP_CHUNKS
    assert s % rows == 0
    nt = s // rows
    n_tiles = b * nt
    final = final_inputs is not None
    hd = np.arange(R_QK_W)[:, None] // R_QK_DIM == np.arange(R_W)[None, :] // R_V_DIM
    mask = jnp.asarray(hd.astype(np.float32))
    decay, qdec, kdec, rdec = _retention_tables(lg, reverse)

    def tile(t, finish):
        tt = jnp.maximum(t - 1, 0) if finish else jnp.minimum(t, n_tiles - 1)
        i = tt % nt
        return tt // nt, (nt - 1 - i) if reverse else i

    def seq(w, cb, finish):
        return pl.BlockSpec((1, rows, w), lambda t: (*tile(t, finish), cb))

    const = lambda shape: pl.BlockSpec(shape, lambda t: (0,) * len(shape))
    tab = pl.BlockSpec((rows, R_QK_W), lambda t: (tile(t, False)[1], 0))
    in_specs = [seq(R_QK_W, 0, False), seq(R_QK_W, 1, False), tab, tab,
                const((R_HEADS * R_CHUNK, R_CHUNK)), const((R_CHUNK, R_QK_W)), const((R_CHUNK, R_QK_W)),
                seq(R_W, 1, True), const((R_QK_W, R_W)), const((R_QK_W, R_W))]
    args = [pb, pb, cos, sin, decay, qdec, kdec, pb, rdec, mask]
    if final:
        o_fwd, avg = final_inputs
        in_specs += [seq(R_W, 0, True), seq(R_W, 2, True), const((R_W, R_W))]
        args += [o_fwd, pb, avg]
    stage = lambda shape: [pltpu.VMEM((R_STEP_CHUNKS,) + shape, BF16) for _ in range(2)]
    return pl.pallas_call(
        functools.partial(_retention_kernel, n_chunks=R_STEP_CHUNKS, steps_per_row=nt, reverse=reverse, final=final),
        grid=(n_tiles + 1,),
        in_specs=in_specs,
        out_specs=seq(R_W, 0, True),
        out_shape=jax.ShapeDtypeStruct((b, s, R_W), BF16 if final else F32),
        scratch_shapes=([pltpu.VMEM((R_QK_W, R_W), F32)] + stage((R_HEADS * R_CHUNK, R_CHUNK))
                        + stage((R_CHUNK, R_QK_W)) + stage((R_CHUNK, R_QK_W))),
        compiler_params=_cparams("arbitrary"),
        name="retention_bwd" if reverse else "retention_fwd",
    )(*args)


def _retention(pb, cos, sin, lg_f, lg_b):
    b, s, _ = pb.shape
    gh = np.arange(R_W)[:, None] // R_V_DIM == np.arange(R_W)[None, :] // R_V_DIM
    avg = jnp.asarray(gh.astype(np.float32) / R_V_DIM, dtype=BF16)
    o_fwd = _retention_direction(pb, cos, sin, lg_f, False)
    out = _retention_direction(pb, cos, sin, lg_b, True, (o_fwd, avg))
    return out.reshape(b * s, R_W)


def _rope_tables(s):
    half = R_QK_DIM // 2
    freqs = ROPE_BASE ** (-jnp.arange(half, dtype=F32) / half)
    ang = jnp.arange(s, dtype=F32)[:, None] * freqs[None]
    cos, sin = jnp.cos(ang), jnp.sin(ang)
    return (jnp.tile(jnp.concatenate([cos, cos], axis=1), (1, R_HEADS)),
            jnp.tile(jnp.concatenate([-sin, sin], axis=1), (1, R_HEADS)))


def _natural_order(o_ref, lse_ref, o_stage, lse_stage, d):
    if d == 1:
        return o_ref[...].astype(F32), lse_ref[...]
    n = o_ref.shape[0]
    for r in range(d):
        lse_stage[pl.ds(r, n, stride=d), :] = lse_ref[:, r * LANES:(r + 1) * LANES]
        for c in range(A_W // LANES):
            col = r * A_W + c * LANES
            o_stage[c, pl.ds(r, n, stride=d), :] = o_ref[:, col:col + LANES].astype(F32)
    return jnp.concatenate([o_stage[c] for c in range(A_W // LANES)], axis=1), lse_stage[...]


def _merge_out_kernel(*refs):
    n_a = len(A_DILATIONS)
    o_refs, lse_refs = refs[:n_a], refs[n_a:2 * n_a]
    r_ref, c_ref, x_ref, w_ref, g_ref, expand_ref, x1_ref, h2_ref = refs[2 * n_a:2 * n_a + 8]
    stages = refs[2 * n_a + 8:]
    outs, lses = [], []
    for b, d in enumerate(A_DILATIONS):
        o, lse = _natural_order(o_refs[b], lse_refs[b], stages[2 * b], stages[2 * b + 1], d)
        outs.append(o)
        lses.append(lse)
    top = functools.reduce(jnp.maximum, lses)
    ws = [jnp.exp2(l - top) for l in lses]
    den = functools.reduce(lambda u, v: u + v, ws)
    expand = expand_ref[...]
    is_head = lax.broadcasted_iota(jnp.int32, den.shape, 1) < A_HEADS
    a = None
    for wgt, o in zip(ws, outs):
        wn = jnp.where(is_head, wgt / den, 0.0)
        hi = wn.astype(BF16).astype(F32)
        packed = hi + pltpu.roll(wn - hi, A_HEADS, 1)
        term = jnp.dot(packed.astype(BF16), expand, preferred_element_type=F32) * o
        a = term if a is None else a + term
    a = a.astype(BF16)
    r = r_ref[...]
    c = c_ref[...]
    nc = OUT_PROJ_COLS
    for n0 in range(0, D_MODEL, nc):
        cols = slice(n0, n0 + nc)
        y = jnp.dot(a, w_ref[0:A_W, cols], preferred_element_type=F32)
        y = y + jnp.dot(r, w_ref[A_W:A_W + R_W, cols], preferred_element_type=F32)
        y = y + jnp.dot(c, w_ref[A_W + R_W:MIX_W, cols], preferred_element_type=F32)
        x1_ref[:, cols] = x_ref[:, cols] + y
    x1 = x1_ref[...]
    h2 = x1 * lax.rsqrt(jnp.mean(x1 * x1, axis=-1, keepdims=True) + EPS) * g_ref[...]
    h2_ref[...] = h2.astype(BF16)


def _merge_out(branches, r, c, x, w, g, tm=PROJ_TOKENS):
    t = x.shape[0]
    assert t % tm == 0 and all(tm % (BF16_ROWS * d) == 0 for d in A_DILATIONS)
    expand = jnp.asarray((np.arange(LANES)[:, None] % A_HEADS == np.arange(A_W)[None, :] // HEAD_DIM)
                         & (np.arange(LANES)[:, None] < 2 * A_HEADS), dtype=BF16)
    row = lambda w_: pl.BlockSpec((tm, w_), lambda i: (i, 0))
    dil_row = lambda w_, d: pl.BlockSpec((tm // d, d * w_), lambda i: (i, 0))
    stages = []
    for _ in A_DILATIONS:
        stages += [pltpu.VMEM((A_W // LANES, tm, LANES), F32), pltpu.VMEM((tm, LANES), F32)]
    return pl.pallas_call(
        _merge_out_kernel,
        grid=(t // tm,),
        in_specs=[dil_row(A_W, d) for d in A_DILATIONS] + [dil_row(LANES, d) for d in A_DILATIONS] + [
            row(R_W), row(C_W), row(D_MODEL),
            pl.BlockSpec((MIX_W, D_MODEL), lambda i: (0, 0), pipeline_mode=pl.Buffered(1)),
            pl.BlockSpec((1, D_MODEL), lambda i: (0, 0)),
            pl.BlockSpec((LANES, A_W), lambda i: (0, 0))],
        out_specs=[row(D_MODEL), row(D_MODEL)],
        out_shape=[jax.ShapeDtypeStruct((t, D_MODEL), F32), jax.ShapeDtypeStruct((t, D_MODEL), BF16)],
        scratch_shapes=stages,
        compiler_params=_cparams("parallel"),
        name="merge_out_proj",
    )(*[o for o, _ in branches], *[l for _, l in branches], r, c, x, w, g.reshape(1, D_MODEL), expand)


def _ffn_accumulate(h_ref, x_ref, wg_ref, wu_ref, wd_ref, o_ref):
    @pl.when(pl.program_id(1) == 0)
    def _():
        o_ref[...] = x_ref[...]

    h = h_ref[...]
    tf = wg_ref.shape[1]
    halves = [slice(c, c + tf // 2) for c in (0, tf // 2)]
    gs = [jnp.dot(h, wg_ref[:, c], preferred_element_type=F32) for c in halves]
    us = [jnp.dot(h, wu_ref[:, c], preferred_element_type=F32) for c in halves]
    y = None
    for g, u, c in zip(gs, us, halves):
        a = (g / (1.0 + jnp.exp(-g)) * u).astype(BF16)
        part = jnp.dot(a, wd_ref[c, :], preferred_element_type=F32)
        y = part if y is None else y + part
    o_ref[...] += y


def _ffn_kernel(h_ref, x_ref, wg_ref, wu_ref, wd_ref, o_ref):
    _ffn_accumulate(h_ref, x_ref, wg_ref, wu_ref, wd_ref, o_ref)


def _ffn_norm_kernel(h_ref, x_ref, wg_ref, wu_ref, wd_ref, g_ref, o_ref):
    _ffn_accumulate(h_ref, x_ref, wg_ref, wu_ref, wd_ref, o_ref)

    @pl.when(pl.program_id(1) == pl.num_programs(1) - 1)
    def _():
        x = o_ref[...]
        o_ref[...] = x * lax.rsqrt(jnp.mean(x * x, axis=-1, keepdims=True) + EPS) * g_ref[...]


def _ffn(h, x, wg, wu, wd, out_norm_g=None, tm=FFN_TOKENS, tf=FFN_COLS):
    t = x.shape[0]
    assert t % tm == 0 and D_FF % tf == 0
    in_specs = [
        pl.BlockSpec((tm, D_MODEL), lambda i, k: (i, 0)),
        pl.BlockSpec((tm, D_MODEL), lambda i, k: (i, 0)),
        pl.BlockSpec((D_MODEL, tf), lambda i, k: (0, k)),
        pl.BlockSpec((D_MODEL, tf), lambda i, k: (0, k)),
        pl.BlockSpec((tf, D_MODEL), lambda i, k: (k, 0)),
    ]
    args = [h, x, wg, wu, wd]
    body = _ffn_kernel
    if out_norm_g is not None:
        in_specs.append(pl.BlockSpec((1, D_MODEL), lambda i, k: (0, 0)))
        args.append(out_norm_g.reshape(1, D_MODEL))
        body = _ffn_norm_kernel
    return pl.pallas_call(
        body,
        grid=(t // tm, D_FF // tf),
        in_specs=in_specs,
        out_specs=pl.BlockSpec((tm, D_MODEL), lambda i, k: (i, 0)),
        out_shape=jax.ShapeDtypeStruct((t, D_MODEL), F32),
        compiler_params=_cparams("parallel", "arbitrary"),
        name="swiglu_ffn",
    )(*args)


def _cast_kernel(w_ref, o_ref, *, axis, start):
    w = w_ref[...]
    if axis is None:
        o_ref[...] = w.astype(BF16)
        return
    take = lambda a, n: lax.slice_in_dim(w, a, a + n, axis=axis)
    parts = [take(0, start)] + [take(start + h * HEAD_DIM, HEAD_DIM) for h in C_Q_ORDER]
    tail = start + C_W
    if tail < w.shape[axis]:
        parts.append(take(tail, w.shape[axis] - tail))
    o_ref[...] = jnp.concatenate(parts, axis=axis).astype(BF16)


def _layer_weight_bf16(w, layer, rows_per_step=CAST_ROWS, reorder_axis=None, reorder_start=0):
    _, r, c = w.shape
    tr = rows_per_step
    assert r % tr == 0 and (reorder_axis != 0 or tr == r)
    return pl.pallas_call(
        functools.partial(_cast_kernel, axis=reorder_axis, start=reorder_start),
        grid=(r // tr,),
        in_specs=[pl.BlockSpec((None, tr, c), lambda i: (layer, i, 0))],
        out_specs=pl.BlockSpec((tr, c), lambda i: (i, 0)),
        out_shape=jax.ShapeDtypeStruct((r, c), BF16),
        compiler_params=_cparams("parallel"),
        name="weight_to_bf16",
    )(w)


def _layer(x, b, s, p, out_norm_g):
    *pas, pb, pc = _norm_proj(x, p["g1"], p["w_in"])
    branches = [_dilated_branch(pa.reshape(b, s // d, d * PA_W), bias, d)
                for pa, bias, d in zip(pas, p["bias_a"], A_DILATIONS)]
    r = _retention(pb.reshape(b, s, PB_W), p["cos"], p["sin"], p["lg_f"], p["lg_b"])
    c = _window_attn(pc.reshape(b, s, PC_W), p["bias_c"], p["sink"])
    x1, h2 = _merge_out(branches, r, c, x, p["w_out"], p["g2"])
    return _ffn(h2, x1, p["w_gate"], p["w_up"], p["w_down"], out_norm_g)


def kernel(x_prompt, x_sample, rel_bias, norm1_g, w_in, ret_decay_fwd, ret_decay_bwd, attn_sink, w_out, norm2_g,
           w_gate, w_up, w_down, final_norm_g):
    bias_a = [_pair_heads(_band_bias(rel_bias[:, :A_HEADS], A_TQ, A_TK, A_RADIUS, A_RADIUS, dil))
              for _, dil in A_BRANCHES]
    q_order = np.asarray(C_Q_ORDER)
    bias_c = _pair_heads(_band_bias(rel_bias[:, A_HEADS:][:, q_order], C_TQ, C_TK, C_TQ, C_RADIUS, 1))
    layers = []
    for i in range(DEPTH):
        layers.append(dict(
            g1=norm1_g[i], g2=norm2_g[i], sink=attn_sink[i].astype(F32)[q_order] * LOG2E, bias_a=bias_a, bias_c=bias_c,
            w_in=_layer_weight_bf16(w_in, i, reorder_axis=1, reorder_start=PA_W + PB_W),
            w_out=_layer_weight_bf16(w_out, i, rows_per_step=MIX_W, reorder_axis=0, reorder_start=A_W + R_W),
            w_gate=_layer_weight_bf16(w_gate, i), w_up=_layer_weight_bf16(w_up, i),
            w_down=_layer_weight_bf16(w_down, i),
            lg_f=jnp.log1p(-jnp.exp2(-ret_decay_fwd[i].astype(F32))),
            lg_b=jnp.log1p(-jnp.exp2(-ret_decay_bwd[i].astype(F32))),
        ))

    def trunk(x):
        b, s, _ = x.shape
        cos, sin = _rope_tables(s)
        x = x.reshape(b * s, D_MODEL)
        for i, p in enumerate(layers):
            x = _layer(x, b, s, dict(p, cos=cos, sin=sin), final_norm_g if i == DEPTH - 1 else None)
        return x.reshape(b, s, D_MODEL)

    return trunk(x_prompt), trunk(x_sample)
```

```python
import functools
import math

import numpy as np
import jax
import jax.numpy as jnp
from jax import lax
from jax.experimental import pallas as pl
from jax.experimental.pallas import tpu as pltpu

D_MODEL = 2048
DEPTH = 2
HEAD_DIM = 64
A_HEADS = 12
A_BRANCHES = ((128, 1), (512, 4), (2048, 16))
A_RADIUS = 64
R_HEADS = 8
R_QK_DIM = 32
R_V_DIM = 64
R_CHUNK = 128
ROPE_BASE = 10000.0
C_HEADS = 12
C_KV_HEADS = 4
C_GROUP = C_HEADS // C_KV_HEADS
C_RADIUS = 128
REL_BUCKETS = 32
REL_MAX_DIST = 1024
D_FF = 5632
EPS = 1e-6
GN_EPS = 1e-5
NEG = -1e30

A_W = A_HEADS * HEAD_DIM
R_QK_W = R_HEADS * R_QK_DIM
R_W = R_HEADS * R_V_DIM
C_W = C_HEADS * HEAD_DIM
C_KV_W = C_KV_HEADS * HEAD_DIM
PA_W = 3 * A_W
PB_W = 2 * R_QK_W + 2 * R_W
PC_W = C_W + 2 * C_KV_W
IN_COLS = PA_W + PB_W + PC_W
MIX_W = A_W + R_W + C_W

LANES = 128
SUBLANES = 8
BF16_ROWS = 16
V7X_VMEM_BYTES = 64 * 1024 * 1024
VMEM_LIMIT = V7X_VMEM_BYTES - 4 * 1024 * 1024

V7X_MXU_WIDTH = 256
PROJ_TOKENS = 512
FFN_TOKENS = 1024
FFN_COLS = 2 * V7X_MXU_WIDTH
OUT_PROJ_COLS = 2 * V7X_MXU_WIDTH
CAST_ROWS = 256

A_TQ = 128
A_TK = A_TQ + 2 * A_RADIUS
A_SUB = 4
C_TQ = 128
C_TK = 3 * C_TQ
C_SUB = 4
C_Q_ORDER = tuple(kp * C_GROUP + j + e * C_GROUP
                  for kp in range(0, C_KV_HEADS, 2) for j in range(C_GROUP) for e in range(2))
R_STEP_CHUNKS = 8

LOG2E = 1.4426950408889634
QSCALE = HEAD_DIM ** -0.5 * LOG2E

BF16 = jnp.bfloat16
F32 = jnp.float32


def _cparams(*sem):
    return pltpu.CompilerParams(dimension_semantics=sem, vmem_limit_bytes=VMEM_LIMIT)


def _t5_bucket(rel):
    nb = REL_BUCKETS // 2
    max_exact = nb // 2
    ret = np.where(rel > 0, nb, 0)
    n = np.abs(rel)
    nf = np.maximum(n, 1).astype(np.float32)
    large = max_exact + (np.log(nf / max_exact) / math.log(REL_MAX_DIST / max_exact) * (nb - max_exact)).astype(np.int32)
    large = np.minimum(large, nb - 1)
    return (ret + np.where(n < max_exact, n, large)).astype(np.int32)


def _band_bias(table, tq, tk, lead, radius, dil):
    h = table.shape[1]
    n = tq + tk - 1
    offs = np.arange(n) - (tq - 1) - lead
    per_off = jnp.take(table.astype(F32), jnp.asarray(_t5_bucket(offs * dil)), axis=0).T * LOG2E
    padded = jnp.concatenate([per_off, jnp.zeros((h, 1), F32)], axis=1)
    toep = jnp.tile(padded, (1, tq))[:, :tq * n].reshape(h, tq, n)[:, :, tq - 1:tq - 1 + tk]
    off = np.arange(tk)[None, :] - lead - np.arange(tq)[:, None]
    band = np.abs(off) <= radius
    col = np.arange(tk)[None, :]
    first = band & (col >= lead)
    last = band & (col < tk - lead)
    masks = np.stack([band, first, last, first & last])
    return jnp.where(jnp.asarray(masks)[:, None], toep[None], NEG).swapaxes(-1, -2)


def _pair_heads(bias):
    v, h, tk, tq = bias.shape
    return bias.reshape(v, h // 2, 2, tk, tq).swapaxes(2, 3).reshape(v, h // 2, tk, 2 * tq)


def _proj_chunks():
    chunks = [(0, i * A_W, i * A_W, A_W, QSCALE if i == 0 else None) for i in range(PA_W // A_W)]
    width = 2 * V7X_MXU_WIDTH
    chunks += [(1, c, PA_W + c, width, None) for c in range(0, PB_W, width)]
    chunks += [(2, 0, PA_W + PB_W, C_W, QSCALE), (2, C_W, PA_W + PB_W + C_W, 2 * C_KV_W, None)]
    return tuple(chunks)


_PROJ_CHUNKS = _proj_chunks()


A_DILATIONS = tuple(d for _, d in A_BRANCHES)


def _norm_proj_kernel(x_ref, g_ref, w_ref, *refs):
    n_a = len(A_DILATIONS)
    pa_refs, (pb_ref, pc_ref), stage_refs = refs[:n_a], refs[n_a:n_a + 2], refs[n_a + 2:]
    tm = x_ref.shape[0]
    x = x_ref[...]
    h = (x * lax.rsqrt(jnp.mean(x * x, axis=-1, keepdims=True) + EPS) * g_ref[...]).astype(BF16)
    outs = (None, pb_ref, pc_ref)
    a_chunk = 0
    for oi, oc, wc, width, scale in _PROJ_CHUNKS:
        y = jnp.dot(h, w_ref[:, wc:wc + width], preferred_element_type=F32)
        if scale is not None:
            y = y * scale
        if oi != 0:
            outs[oi][:, oc:oc + width] = y.astype(BF16)
            continue
        stage, stage_mid = stage_refs[2 * a_chunk], stage_refs[2 * a_chunk + 1]
        a_chunk += 1
        _, d1, d2 = A_DILATIONS
        slabs = range(width // LANES)
        for c in slabs:
            stage[c] = y[:, c * LANES:(c + 1) * LANES]
        pa_refs[0][:, oc:oc + width] = y.astype(BF16)
        for r in range(d1):
            for c in slabs:
                part = stage[c, pl.ds(r, tm // d1, stride=d1), :]
                stage_mid[c, r] = part
                col = r * PA_W + oc + c * LANES
                pa_refs[1][:, col:col + LANES] = part.astype(BF16)
        for r in range(d1):
            for a in range(d2 // d1):
                for c in slabs:
                    col = (a * d1 + r) * PA_W + oc + c * LANES
                    part = stage_mid[c, r, pl.ds(a, tm // d2, stride=d2 // d1), :]
                    pa_refs[2][:, col:col + LANES] = part.astype(BF16)


def _norm_proj(x, g, w, tm=PROJ_TOKENS):
    t = x.shape[0]
    assert t % tm == 0 and all(tm % (BF16_ROWS * d) == 0 for d in A_DILATIONS)
    d0, d1, d2 = A_DILATIONS
    assert d0 == 1 and d2 % d1 == 0
    n_a_chunks = sum(1 for c in _PROJ_CHUNKS if c[0] == 0)
    stages = []
    for _ in range(n_a_chunks):
        stages += [pltpu.VMEM((A_W // LANES, tm, LANES), F32), pltpu.VMEM((A_W // LANES, d1, tm // d1, LANES), F32)]
    return pl.pallas_call(
        _norm_proj_kernel,
        grid=(t // tm,),
        in_specs=[
            pl.BlockSpec((tm, D_MODEL), lambda i: (i, 0)),
            pl.BlockSpec((1, D_MODEL), lambda i: (0, 0)),
            pl.BlockSpec((D_MODEL, IN_COLS), lambda i: (0, 0), pipeline_mode=pl.Buffered(1)),
        ],
        out_specs=[pl.BlockSpec((tm // d, d * PA_W), lambda i: (i, 0)) for d in A_DILATIONS] + [
            pl.BlockSpec((tm, PB_W), lambda i: (i, 0)),
            pl.BlockSpec((tm, PC_W), lambda i: (i, 0)),
        ],
        out_shape=[jax.ShapeDtypeStruct((t // d, d * PA_W), BF16) for d in A_DILATIONS] + [
            jax.ShapeDtypeStruct((t, PB_W), BF16),
            jax.ShapeDtypeStruct((t, PC_W), BF16),
        ],
        scratch_shapes=stages,
        compiler_params=_cparams("parallel"),
        name="norm_in_proj",
    )(x, g.reshape(1, D_MODEL), w)


def _nt_dot(a, b):
    return lax.dot_general(a, b, (((1,), (1,)), ((), ())), preferred_element_type=F32)


def _masked_heads(qg):
    head = lax.broadcasted_iota(jnp.int32, qg.shape, 1) // HEAD_DIM
    zero = jnp.zeros_like(qg)
    return jnp.concatenate([jnp.where(head == e, qg, zero) for e in range(qg.shape[1] // HEAD_DIM)], axis=0)


A_GROUP = 4


def _dilated_attn_kernel(q_ref, kp_ref, kc_ref, kn_ref, vp_ref, vc_ref, vn_ref, *refs):
    bias_refs = refs[:A_SUB]
    o_ref, lse_ref, s_even_ref, s_odd_ref, m_even_ref, m_odd_ref = refs[A_SUB:]
    t = pl.program_id(0)
    n_groups = A_HEADS // A_GROUP

    @pl.when(t == 0)
    def _():
        s_odd_ref[...] = jnp.zeros_like(s_odd_ref)
        m_odd_ref[...] = jnp.zeros_like(m_odd_ref)

    def step(s_write_ref, s_read_ref, m_write_ref, m_read_ref):
        kw = jnp.concatenate([kp_ref[0], kc_ref[0], kn_ref[0]], axis=0)
        for sub, bias_ref in enumerate(bias_refs):
            rows = slice(sub * A_TQ, (sub + 1) * A_TQ)
            keys = slice(sub * A_TQ, sub * A_TQ + A_TK)
            for j in range(n_groups):
                lanes = slice(j * A_GROUP * HEAD_DIM, (j + 1) * A_GROUP * HEAD_DIM)
                bias = jnp.concatenate([bias_ref[pr] for pr in range(j * A_GROUP // 2, (j + 1) * A_GROUP // 2)], axis=1)
                sc = _nt_dot(kw[keys, lanes], _masked_heads(q_ref[0, rows, lanes])) + bias
                s_write_ref[sub * n_groups + j] = sc
                m_write_ref[sub * n_groups + j] = jnp.broadcast_to(
                    jnp.max(sc, axis=0, keepdims=True), (SUBLANES, sc.shape[1]))

        vt = jnp.concatenate([vp_ref[0], vc_ref[0], vn_ref[0]], axis=0).T
        ones = jnp.ones((BF16_ROWS, A_TK), BF16)
        for sub in range(A_SUB):
            rows = slice(sub * A_TQ, (sub + 1) * A_TQ)
            keys = slice(sub * A_TQ, sub * A_TQ + A_TK)
            lses = []
            for j in range(n_groups):
                pairs = range(j * A_GROUP // 2, (j + 1) * A_GROUP // 2)
                s = s_read_ref[sub * n_groups + j]
                m = m_read_ref[sub * n_groups + j][0:1]
                pb = jnp.exp2(s - m).astype(BF16)
                o_t = []
                for e in range(A_GROUP):
                    h = A_GROUP * j + e
                    cols = slice(e * A_TQ, (e + 1) * A_TQ)
                    va = jnp.concatenate([vt[h * HEAD_DIM:(h + 1) * HEAD_DIM, keys], ones], axis=0)
                    o = jnp.dot(va, pb[:, cols], preferred_element_type=F32)
                    l = o[HEAD_DIM:HEAD_DIM + 1]
                    o_t.append(o[:HEAD_DIM] / l)
                    lses.append(m[:, cols] + jnp.log2(l))
                for i, pr in enumerate(pairs):
                    tile = jnp.concatenate(o_t[2 * i:2 * i + 2], axis=0)
                    o_ref[0, rows, pr * LANES:(pr + 1) * LANES] = tile.T.astype(BF16)
            lse_t = jnp.concatenate(lses + [jnp.zeros((LANES - A_HEADS, A_TQ), F32)], axis=0)
            lse_ref[0, rows, :] = lse_t.T

    pl.when(t % 2 == 0)(functools.partial(step, s_even_ref, s_odd_ref, m_even_ref, m_odd_ref))
    pl.when(t % 2 == 1)(functools.partial(step, s_odd_ref, s_even_ref, m_odd_ref, m_even_ref))


def _dilated_branch(pav, bias, dil):
    b, l, _ = pav.shape
    step = A_SUB * A_TQ
    assert A_SUB >= 2 and l % step == 0
    nt = l // step
    n_tiles = b * dil * nt
    halo = A_RADIUS
    hb = step // halo
    nblk = PA_W // A_W

    def decode(tt):
        return tt // (dil * nt), (tt // nt) % dil, tt % nt

    def scores_tile(t):
        return decode(jnp.minimum(t, n_tiles - 1))

    def finish_tile(t):
        return decode(jnp.maximum(t - 1, 0))

    def cur(tile, c):
        def index(t):
            bi, r, i = tile(t)
            return bi, i, r * nblk + c
        return pl.BlockSpec((1, step, A_W), index)

    def halo_spec(tile, c, side):
        def index(t):
            bi, r, i = tile(t)
            blk = jnp.maximum(i * hb - 1, 0) if side < 0 else jnp.minimum((i + 1) * hb, nt * hb - 1)
            return bi, blk, r * nblk + c
        return pl.BlockSpec((1, halo, A_W), index)

    def bias_spec(variant):
        def index(t):
            _, _, i = scores_tile(t)
            return variant(i), 0, 0, 0
        return pl.BlockSpec((None, A_HEADS // 2, A_TK, 2 * A_TQ), index, pipeline_mode=pl.Buffered(1))

    def sub_variant(sub):
        def variant(i):
            first = (i == 0).astype(jnp.int32) if sub == 0 else 0
            last = 2 * (i == nt - 1).astype(jnp.int32) if sub == A_SUB - 1 else 0
            return first + last
        return variant

    def out_spec(w):
        def index(t):
            bi, r, i = finish_tile(t)
            return bi, i, r
        return pl.BlockSpec((1, step, w), index)

    n_scores = A_SUB * (A_HEADS // A_GROUP)
    o, lse = pl.pallas_call(
        _dilated_attn_kernel,
        grid=(n_tiles + 1,),
        in_specs=[
            cur(scores_tile, 0),
            halo_spec(scores_tile, 1, -1), cur(scores_tile, 1), halo_spec(scores_tile, 1, 1),
            halo_spec(finish_tile, 2, -1), cur(finish_tile, 2), halo_spec(finish_tile, 2, 1),
        ] + [bias_spec(sub_variant(sub)) for sub in range(A_SUB)],
        out_specs=[out_spec(A_W), out_spec(LANES)],
        out_shape=[
            jax.ShapeDtypeStruct((b, l, dil * A_W), BF16),
            jax.ShapeDtypeStruct((b, l, dil * LANES), F32),
        ],
        scratch_shapes=([pltpu.VMEM((n_scores, A_TK, A_GROUP * A_TQ), F32) for _ in range(2)]
                        + [pltpu.VMEM((n_scores, SUBLANES, A_GROUP * A_TQ), F32) for _ in range(2)]),
        compiler_params=_cparams("arbitrary"),
        name=f"dilated_attn_d{dil}",
    )(*([pav] * 7 + [bias] * A_SUB))
    return o.reshape(b * l, dil * A_W), lse.reshape(b * l, dil * LANES)


def _window_attn_kernel(sink_ref, q_ref, kp_ref, kc_ref, kn_ref, vp_ref, vc_ref, vn_ref, *refs):
    bias_refs = refs[:C_SUB]
    o_ref, s_even_ref, s_odd_ref, m_even_ref, m_odd_ref = refs[C_SUB:]
    t = pl.program_id(0)

    @pl.when(t == 0)
    def _():
        s_odd_ref[...] = jnp.zeros_like(s_odd_ref)
        m_odd_ref[...] = jnp.zeros_like(m_odd_ref)

    def step(s_write_ref, s_read_ref, m_write_ref, m_read_ref):
        kw = jnp.concatenate([kp_ref[0], kc_ref[0], kn_ref[0]], axis=0)
        for sub, bias_ref in enumerate(bias_refs):
            rows = slice(sub * C_TQ, (sub + 1) * C_TQ)
            keys = slice(sub * C_TQ, sub * C_TQ + C_TK)
            for g in range(C_GROUP):
                pairs = (g, g + C_GROUP)
                qg = jnp.concatenate([q_ref[0, rows, pr * LANES:(pr + 1) * LANES] for pr in pairs], axis=1)
                bias = jnp.concatenate([bias_ref[pr] for pr in pairs], axis=1)
                sc = _nt_dot(kw[keys], _masked_heads(qg)) + bias
                s_write_ref[sub * C_GROUP + g] = sc
                m_write_ref[sub * C_GROUP + g] = jnp.broadcast_to(
                    jnp.max(sc, axis=0, keepdims=True), (SUBLANES, sc.shape[1]))

        vt = jnp.concatenate([vp_ref[0], vc_ref[0], vn_ref[0]], axis=0).T
        ones = jnp.ones((BF16_ROWS, C_TK), BF16)
        for sub in range(C_SUB):
            rows = slice(sub * C_TQ, (sub + 1) * C_TQ)
            keys = slice(sub * C_TQ, sub * C_TQ + C_TK)
            for g in range(C_GROUP):
                pairs = (g, g + C_GROUP)
                s = s_read_ref[sub * C_GROUP + g]
                sk = jnp.concatenate(
                    [jnp.full((1, C_TQ), sink_ref[2 * pr + e], F32) for pr in pairs for e in range(2)], axis=1)
                m = jnp.maximum(m_read_ref[sub * C_GROUP + g][0:1], sk)
                pb = jnp.exp2(s - m).astype(BF16)
                sink_p = jnp.exp2(sk - m)
                o_t = []
                for kv in range(C_KV_HEADS):
                    cols = slice(kv * C_TQ, (kv + 1) * C_TQ)
                    va = jnp.concatenate([vt[kv * HEAD_DIM:(kv + 1) * HEAD_DIM, keys], ones], axis=0)
                    o = jnp.dot(va, pb[:, cols], preferred_element_type=F32)
                    den = o[HEAD_DIM:HEAD_DIM + 1] + sink_p[:, cols]
                    o_t.append(o[:HEAD_DIM] / den)
                for i, pr in enumerate(pairs):
                    tile = jnp.concatenate(o_t[2 * i:2 * i + 2], axis=0)
                    o_ref[0, rows, pr * LANES:(pr + 1) * LANES] = tile.T.astype(BF16)

    pl.when(t % 2 == 0)(functools.partial(step, s_even_ref, s_odd_ref, m_even_ref, m_odd_ref))
    pl.when(t % 2 == 1)(functools.partial(step, s_odd_ref, s_even_ref, m_odd_ref, m_even_ref))


def _window_attn(pc, bias, sink):
    b, s, _ = pc.shape
    step = C_SUB * C_TQ
    assert C_SUB >= 2 and s % step == 0
    nt = s // step
    n_tiles = b * nt
    kblk = C_W // C_KV_W

    def scores_tile(t):
        tt = jnp.minimum(t, n_tiles - 1)
        return tt // nt, tt % nt

    def finish_tile(t):
        tt = jnp.maximum(t - 1, 0)
        return tt // nt, tt % nt

    def cur(tile, w, c):
        def index(t):
            bi, i = tile(t)
            return bi, i, c
        return pl.BlockSpec((1, step, w), index)

    def halo_spec(tile, c, side):
        def index(t):
            bi, i = tile(t)
            blk = jnp.maximum(i * C_SUB - 1, 0) if side < 0 else jnp.minimum((i + 1) * C_SUB, nt * C_SUB - 1)
            return bi, blk, c
        return pl.BlockSpec((1, C_TQ, C_KV_W), index)

    def bias_spec(sub):
        def index(t):
            _, i = scores_tile(t)
            first = (i == 0).astype(jnp.int32) if sub == 0 else 0
            last = 2 * (i == nt - 1).astype(jnp.int32) if sub == C_SUB - 1 else 0
            return first + last, 0, 0, 0
        return pl.BlockSpec((None, C_HEADS // 2, C_TK, 2 * C_TQ), index, pipeline_mode=pl.Buffered(1))

    o = pl.pallas_call(
        _window_attn_kernel,
        grid=(n_tiles + 1,),
        in_specs=[
            pl.BlockSpec(memory_space=pltpu.SMEM),
            cur(scores_tile, C_W, 0),
            halo_spec(scores_tile, kblk, -1), cur(scores_tile, C_KV_W, kblk), halo_spec(scores_tile, kblk, 1),
            halo_spec(finish_tile, kblk + 1, -1), cur(finish_tile, C_KV_W, kblk + 1),
            halo_spec(finish_tile, kblk + 1, 1),
        ] + [bias_spec(sub) for sub in range(C_SUB)],
        out_specs=cur(finish_tile, C_W, 0),
        out_shape=jax.ShapeDtypeStruct((b, s, C_W), BF16),
        scratch_shapes=([pltpu.VMEM((C_SUB * C_GROUP, C_TK, C_KV_HEADS * C_TQ), F32) for _ in range(2)]
                        + [pltpu.VMEM((C_SUB * C_GROUP, SUBLANES, C_KV_HEADS * C_TQ), F32) for _ in range(2)]),
        compiler_params=_cparams("arbitrary"),
        name="window_gqa",
    )(sink.astype(F32), *([pc] * 7 + [bias] * C_SUB))
    return o.reshape(b * s, C_W)


def _rope(x, cos, sin, lane_lo):
    parts = []
    for a in range(0, R_QK_W, LANES):
        xh = x[:, a:a + LANES]
        half = R_QK_DIM // 2
        parts.append(jnp.where(lane_lo, pltpu.roll(xh, LANES - half, 1), pltpu.roll(xh, half, 1)))
    return x * cos + jnp.concatenate(parts, axis=1) * sin


def _retention_scores(q, k, cos, sin, decay, qdec, kdec):
    c = R_CHUNK
    lane = lax.broadcasted_iota(jnp.int32, (c, LANES), 1)
    lane_lo = (lane % R_QK_DIM) < (R_QK_DIM // 2)
    q = _rope(q.astype(F32), cos, sin, lane_lo)
    k = _rope(k.astype(F32), cos, sin, lane_lo) * (R_QK_DIM ** -0.5)
    qb = q.astype(BF16)
    head = lax.broadcasted_iota(jnp.int32, (c, R_QK_W), 1) // R_QK_DIM
    qs = jnp.concatenate([jnp.where(head == h, qb, jnp.zeros_like(qb)) for h in range(R_HEADS)], axis=0)
    s = _nt_dot(qs, k.astype(BF16))
    return (s * decay).astype(BF16), (q * qdec).astype(BF16), (k * kdec).astype(BF16)


def _retention_values(p, qd, kd, v, state, rdec, mask):
    c = R_CHUNK
    cross = jnp.dot(qd, state.astype(BF16), preferred_element_type=F32)
    vlane = lax.broadcasted_iota(jnp.int32, (c, R_W), 1) % LANES
    v_lo = jnp.where(vlane < R_V_DIM, v, jnp.zeros_like(v))
    v_hi = jnp.where(vlane >= R_V_DIM, v, jnp.zeros_like(v))
    pieces = []
    for j in range(R_HEADS // 2):
        cs = slice(j * LANES, (j + 1) * LANES)
        a = jnp.dot(p[(2 * j) * c:(2 * j + 1) * c], v_lo[:, cs], preferred_element_type=F32)
        a = a + jnp.dot(p[(2 * j + 1) * c:(2 * j + 2) * c], v_hi[:, cs], preferred_element_type=F32)
        pieces.append(a)
    intra = jnp.concatenate(pieces, axis=1)
    kv = lax.dot_general(kd, v, (((0,), (0,)), ((), ())), preferred_element_type=F32)
    return cross + intra, state * rdec + kv * mask


def _retention_kernel(q_ref, k_ref, cos_ref, sin_ref, decay_ref, qdec_ref, kdec_ref, v_ref, rdec_ref, mask_ref,
                      *refs, n_chunks, steps_per_row, reverse, final):
    if final:
        fwd_ref, gate_ref, avg_ref = refs[:3]
        refs = refs[3:]
    o_ref, state_ref, p_even, p_odd, qd_even, qd_odd, kd_even, kd_odd = refs
    t = pl.program_id(0)

    @pl.when(t == 0)
    def _():
        for ref in (state_ref, p_odd, qd_odd, kd_odd):
            ref[...] = jnp.zeros_like(ref)

    restart = (jnp.maximum(t - 1, 0) % steps_per_row) == 0

    def step(write_refs, read_refs):
        p_w, qd_w, kd_w = write_refs
        p_r, qd_r, kd_r = read_refs
        for ci in range(n_chunks):
            rows = slice(ci * R_CHUNK, (ci + 1) * R_CHUNK)
            p_w[ci], qd_w[ci], kd_w[ci] = _retention_scores(
                q_ref[0, rows, :], k_ref[0, rows, :], cos_ref[rows, :], sin_ref[rows, :],
                decay_ref[...], qdec_ref[...], kdec_ref[...])

        state = jnp.where(restart, jnp.zeros_like(state_ref), state_ref[...])
        for ci in (reversed(range(n_chunks)) if reverse else range(n_chunks)):
            rows = slice(ci * R_CHUNK, (ci + 1) * R_CHUNK)
            o, state = _retention_values(p_r[ci], qd_r[ci], kd_r[ci], v_ref[0, rows, :], state,
                                         rdec_ref[...], mask_ref[...])
            if not final:
                o_ref[0, rows, :] = o
                continue
            o = o + fwd_ref[0, rows, :]
            avg = avg_ref[...]
            d = o - jnp.dot(o.astype(BF16), avg, preferred_element_type=F32)
            y = d * lax.rsqrt(jnp.dot((d * d).astype(BF16), avg, preferred_element_type=F32) + GN_EPS)
            g = gate_ref[0, rows, :].astype(F32)
            o_ref[0, rows, :] = (y * (g / (1.0 + jnp.exp(-g)))).astype(BF16)
        state_ref[...] = state

    even, odd = (p_even, qd_even, kd_even), (p_odd, qd_odd, kd_odd)
    pl.when(t % 2 == 0)(functools.partial(step, even, odd))
    pl.when(t % 2 == 1)(functools.partial(step, odd, even))


def _retention_tables(lg, reverse):
    c = R_CHUNK
    t = np.arange(c)
    diff = (t[None, :] - t[:, None]) if reverse else (t[:, None] - t[None, :])
    keep = (diff > 0) if reverse else (diff >= 0)
    decay = jnp.where(jnp.asarray(keep)[None], jnp.exp(lg[:, None, None] * np.maximum(diff, 0).astype(np.float32)), 0.0)
    tf = t.astype(np.float32)
    qpow = (c - tf) if reverse else (tf + 1.0)
    kpow = tf if reverse else (c - 1.0 - tf)
    qdec = jnp.repeat(jnp.exp(lg[None, :] * qpow[:, None]), R_QK_DIM, axis=1)
    kdec = jnp.repeat(jnp.exp(lg[None, :] * kpow[:, None]), R_QK_DIM, axis=1)
    rdec = jnp.broadcast_to(jnp.repeat(jnp.exp(lg * c), R_QK_DIM)[:, None], (R_QK_W, R_W))
    return decay.reshape(R_HEADS * c, c), qdec, kdec, rdec


def _retention_direction(pb, cos, sin, lg, reverse, final_inputs=None):
    b, s, _ = pb.shape
    rows = R_CHUNK * R_STEP_CHUNKS
    assert s % rows == 0
    nt = s // rows
    n_tiles = b * nt
    final = final_inputs is not None
    hd = np.arange(R_QK_W)[:, None] // R_QK_DIM == np.arange(R_W)[None, :] // R_V_DIM
    mask = jnp.asarray(hd.astype(np.float32))
    decay, qdec, kdec, rdec = _retention_tables(lg, reverse)

    def tile(t, finish):
        tt = jnp.maximum(t - 1, 0) if finish else jnp.minimum(t, n_tiles - 1)
        i = tt % nt
        return tt // nt, (nt - 1 - i) if reverse else i

    def seq(w, cb, finish):
        return pl.BlockSpec((1, rows, w), lambda t: (*tile(t, finish), cb))

    const = lambda shape: pl.BlockSpec(shape, lambda t: (0,) * len(shape), pipeline_mode=pl.Buffered(1))
    tab = pl.BlockSpec((rows, R_QK_W), lambda t: (tile(t, False)[1], 0))
    in_specs = [seq(R_QK_W, 0, False), seq(R_QK_W, 1, False), tab, tab,
                const((R_HEADS * R_CHUNK, R_CHUNK)), const((R_CHUNK, R_QK_W)), const((R_CHUNK, R_QK_W)),
                seq(R_W, 1, True), const((R_QK_W, R_W)), const((R_QK_W, R_W))]
    args = [pb, pb, cos, sin, decay, qdec, kdec, pb, rdec, mask]
    if final:
        o_fwd, avg = final_inputs
        in_specs += [seq(R_W, 0, True), seq(R_W, 2, True), const((R_W, R_W))]
        args += [o_fwd, pb, avg]
    stage = lambda shape: [pltpu.VMEM((R_STEP_CHUNKS,) + shape, BF16) for _ in range(2)]
    return pl.pallas_call(
        functools.partial(_retention_kernel, n_chunks=R_STEP_CHUNKS, steps_per_row=nt, reverse=reverse, final=final),
        grid=(n_tiles + 1,),
        in_specs=in_specs,
        out_specs=seq(R_W, 0, True),
        out_shape=jax.ShapeDtypeStruct((b, s, R_W), BF16 if final else F32),
        scratch_shapes=([pltpu.VMEM((R_QK_W, R_W), F32)] + stage((R_HEADS * R_CHUNK, R_CHUNK))
                        + stage((R_CHUNK, R_QK_W)) + stage((R_CHUNK, R_QK_W))),
        compiler_params=_cparams("arbitrary"),
        name="retention_bwd" if reverse else "retention_fwd",
    )(*args)


def _retention(pb, cos, sin, lg_f, lg_b):
    b, s, _ = pb.shape
    gh = np.arange(R_W)[:, None] // R_V_DIM == np.arange(R_W)[None, :] // R_V_DIM
    avg = jnp.asarray(gh.astype(np.float32) / R_V_DIM, dtype=BF16)
    o_fwd = _retention_direction(pb, cos, sin, lg_f, False)
    out = _retention_direction(pb, cos, sin, lg_b, True, (o_fwd, avg))
    return out.reshape(b * s, R_W)


def _rope_tables(s):
    half = R_QK_DIM // 2
    freqs = ROPE_BASE ** (-jnp.arange(half, dtype=F32) / half)
    ang = jnp.arange(s, dtype=F32)[:, None] * freqs[None]
    cos, sin = jnp.cos(ang), jnp.sin(ang)
    return (jnp.tile(jnp.concatenate([cos, cos], axis=1), (1, R_HEADS)),
            jnp.tile(jnp.concatenate([-sin, sin], axis=1), (1, R_HEADS)))


def _natural_order(o_ref, lse_ref, o_stage, lse_stage, d):
    if d == 1:
        return o_ref[...].astype(F32), lse_ref[...]
    n = o_ref.shape[0]
    for r in range(d):
        lse_stage[pl.ds(r, n, stride=d), :] = lse_ref[:, r * LANES:(r + 1) * LANES]
        for c in range(A_W // LANES):
            col = r * A_W + c * LANES
            o_stage[c, pl.ds(r, n, stride=d), :] = o_ref[:, col:col + LANES].astype(F32)
    return jnp.concatenate([o_stage[c] for c in range(A_W // LANES)], axis=1), lse_stage[...]


def _merge_out_kernel(*refs):
    n_a = len(A_DILATIONS)
    o_refs, lse_refs = refs[:n_a], refs[n_a:2 * n_a]
    r_ref, c_ref, x_ref, w_ref, g_ref, expand_ref, x1_ref, h2_ref = refs[2 * n_a:2 * n_a + 8]
    stages = refs[2 * n_a + 8:]
    outs, lses = [], []
    for b, d in enumerate(A_DILATIONS):
        o, lse = _natural_order(o_refs[b], lse_refs[b], stages[2 * b], stages[2 * b + 1], d)
        outs.append(o)
        lses.append(lse)
    top = functools.reduce(jnp.maximum, lses)
    ws = [jnp.exp2(l - top) for l in lses]
    den = functools.reduce(lambda u, v: u + v, ws)
    expand = expand_ref[...]
    is_head = lax.broadcasted_iota(jnp.int32, den.shape, 1) < A_HEADS
    a = None
    for wgt, o in zip(ws, outs):
        wn = jnp.where(is_head, wgt / den, 0.0)
        hi = wn.astype(BF16).astype(F32)
        packed = hi + pltpu.roll(wn - hi, A_HEADS, 1)
        term = jnp.dot(packed.astype(BF16), expand, preferred_element_type=F32) * o
        a = term if a is None else a + term
    a = a.astype(BF16)
    r = r_ref[...]
    c = c_ref[...]
    nc = OUT_PROJ_COLS
    for n0 in range(0, D_MODEL, nc):
        cols = slice(n0, n0 + nc)
        y = jnp.dot(a, w_ref[0:A_W, cols], preferred_element_type=F32)
        y = y + jnp.dot(r, w_ref[A_W:A_W + R_W, cols], preferred_element_type=F32)
        y = y + jnp.dot(c, w_ref[A_W + R_W:MIX_W, cols], preferred_element_type=F32)
        x1_ref[:, cols] = x_ref[:, cols] + y
    x1 = x1_ref[...]
    h2 = x1 * lax.rsqrt(jnp.mean(x1 * x1, axis=-1, keepdims=True) + EPS) * g_ref[...]
    h2_ref[...] = h2.astype(BF16)


def _merge_out(branches, r, c, x, w, g, tm=PROJ_TOKENS):
    t = x.shape[0]
    assert t % tm == 0 and all(tm % (BF16_ROWS * d) == 0 for d in A_DILATIONS)
    expand = jnp.asarray((np.arange(LANES)[:, None] % A_HEADS == np.arange(A_W)[None, :] // HEAD_DIM)
                         & (np.arange(LANES)[:, None] < 2 * A_HEADS), dtype=BF16)
    row = lambda w_: pl.BlockSpec((tm, w_), lambda i: (i, 0))
    dil_row = lambda w_, d: pl.BlockSpec((tm // d, d * w_), lambda i: (i, 0))
    stages = []
    for _ in A_DILATIONS:
        stages += [pltpu.VMEM((A_W // LANES, tm, LANES), F32), pltpu.VMEM((tm, LANES), F32)]
    return pl.pallas_call(
        _merge_out_kernel,
        grid=(t // tm,),
        in_specs=[dil_row(A_W, d) for d in A_DILATIONS] + [dil_row(LANES, d) for d in A_DILATIONS] + [
            row(R_W), row(C_W), row(D_MODEL),
            pl.BlockSpec((MIX_W, D_MODEL), lambda i: (0, 0), pipeline_mode=pl.Buffered(1)),
            pl.BlockSpec((1, D_MODEL), lambda i: (0, 0)),
            pl.BlockSpec((LANES, A_W), lambda i: (0, 0))],
        out_specs=[row(D_MODEL), row(D_MODEL)],
        out_shape=[jax.ShapeDtypeStruct((t, D_MODEL), F32), jax.ShapeDtypeStruct((t, D_MODEL), BF16)],
        scratch_shapes=stages,
        compiler_params=_cparams("parallel"),
        name="merge_out_proj",
    )(*[o for o, _ in branches], *[l for _, l in branches], r, c, x, w, g.reshape(1, D_MODEL), expand)


def _ffn_accumulate(h_ref, x_ref, wg_ref, wu_ref, wd_ref, o_ref):
    @pl.when(pl.program_id(1) == 0)
    def _():
        o_ref[...] = x_ref[...]

    h = h_ref[...]
    tf = wg_ref.shape[1]
    halves = [slice(c, c + tf // 2) for c in (0, tf // 2)]
    gs = [jnp.dot(h, wg_ref[:, c], preferred_element_type=F32) for c in halves]
    us = [jnp.dot(h, wu_ref[:, c], preferred_element_type=F32) for c in halves]
    y = None
    for g, u, c in zip(gs, us, halves):
        a = (g / (1.0 + jnp.exp(-g)) * u).astype(BF16)
        part = jnp.dot(a, wd_ref[c, :], preferred_element_type=F32)
        y = part if y is None else y + part
    o_ref[...] += y


def _ffn_kernel(h_ref, x_ref, wg_ref, wu_ref, wd_ref, o_ref):
    _ffn_accumulate(h_ref, x_ref, wg_ref, wu_ref, wd_ref, o_ref)


def _ffn_norm_kernel(h_ref, x_ref, wg_ref, wu_ref, wd_ref, g_ref, o_ref):
    _ffn_accumulate(h_ref, x_ref, wg_ref, wu_ref, wd_ref, o_ref)

    @pl.when(pl.program_id(1) == pl.num_programs(1) - 1)
    def _():
        x = o_ref[...]
        o_ref[...] = x * lax.rsqrt(jnp.mean(x * x, axis=-1, keepdims=True) + EPS) * g_ref[...]


def _ffn(h, x, wg, wu, wd, out_norm_g=None, tm=FFN_TOKENS, tf=FFN_COLS):
    t = x.shape[0]
    assert t % tm == 0 and D_FF % tf == 0
    in_specs = [
        pl.BlockSpec((tm, D_MODEL), lambda i, k: (i, 0)),
        pl.BlockSpec((tm, D_MODEL), lambda i, k: (i, 0)),
        pl.BlockSpec((D_MODEL, tf), lambda i, k: (0, k)),
        pl.BlockSpec((D_MODEL, tf), lambda i, k: (0, k)),
        pl.BlockSpec((tf, D_MODEL), lambda i, k: (k, 0)),
    ]
    args = [h, x, wg, wu, wd]
    body = _ffn_kernel
    if out_norm_g is not None:
        in_specs.append(pl.BlockSpec((1, D_MODEL), lambda i, k: (0, 0)))
        args.append(out_norm_g.reshape(1, D_MODEL))
        body = _ffn_norm_kernel
    return pl.pallas_call(
        body,
        grid=(t // tm, D_FF // tf),
        in_specs=in_specs,
        out_specs=pl.BlockSpec((tm, D_MODEL), lambda i, k: (i, 0)),
        out_shape=jax.ShapeDtypeStruct((t, D_MODEL), F32),
        compiler_params=_cparams("parallel", "arbitrary"),
        name="swiglu_ffn",
    )(*args)


def _cast_kernel(w_ref, o_ref, *, axis, start):
    w = w_ref[...]
    if axis is None:
        o_ref[...] = w.astype(BF16)
        return
    take = lambda a, n: lax.slice_in_dim(w, a, a + n, axis=axis)
    parts = [take(0, start)] + [take(start + h * HEAD_DIM, HEAD_DIM) for h in C_Q_ORDER]
    tail = start + C_W
    if tail < w.shape[axis]:
        parts.append(take(tail, w.shape[axis] - tail))
    o_ref[...] = jnp.concatenate(parts, axis=axis).astype(BF16)


def _layer_weight_bf16(w, layer, rows_per_step=CAST_ROWS, reorder_axis=None, reorder_start=0):
    _, r, c = w.shape
    tr = rows_per_step
    assert r % tr == 0 and (reorder_axis != 0 or tr == r)
    return pl.pallas_call(
        functools.partial(_cast_kernel, axis=reorder_axis, start=reorder_start),
        grid=(r // tr,),
        in_specs=[pl.BlockSpec((None, tr, c), lambda i: (layer, i, 0))],
        out_specs=pl.BlockSpec((tr, c), lambda i: (i, 0)),
        out_shape=jax.ShapeDtypeStruct((r, c), BF16),
        compiler_params=_cparams("parallel"),
        name="weight_to_bf16",
    )(w)


def _layer(x, b, s, p, out_norm_g):
    *pas, pb, pc = _norm_proj(x, p["g1"], p["w_in"])
    branches = [_dilated_branch(pa.reshape(b, s // d, d * PA_W), bias, d)
                for pa, bias, d in zip(pas, p["bias_a"], A_DILATIONS)]
    r = _retention(pb.reshape(b, s, PB_W), p["cos"], p["sin"], p["lg_f"], p["lg_b"])
    c = _window_attn(pc.reshape(b, s, PC_W), p["bias_c"], p["sink"])
    x1, h2 = _merge_out(branches, r, c, x, p["w_out"], p["g2"])
    return _ffn(h2, x1, p["w_gate"], p["w_up"], p["w_down"], out_norm_g)


def kernel(x_prompt, x_sample, rel_bias, norm1_g, w_in, ret_decay_fwd, ret_decay_bwd, attn_sink, w_out, norm2_g,
           w_gate, w_up, w_down, final_norm_g):
    bias_a = [_pair_heads(_band_bias(rel_bias[:, :A_HEADS], A_TQ, A_TK, A_RADIUS, A_RADIUS, dil))
              for _, dil in A_BRANCHES]
    q_order = np.asarray(C_Q_ORDER)
    bias_c = _pair_heads(_band_bias(rel_bias[:, A_HEADS:][:, q_order], C_TQ, C_TK, C_TQ, C_RADIUS, 1))
    layers = []
    for i in range(DEPTH):
        layers.append(dict(
            g1=norm1_g[i], g2=norm2_g[i], sink=attn_sink[i].astype(F32)[q_order] * LOG2E, bias_a=bias_a, bias_c=bias_c,
            w_in=_layer_weight_bf16(w_in, i, reorder_axis=1, reorder_start=PA_W + PB_W),
            w_out=_layer_weight_bf16(w_out, i, rows_per_step=MIX_W, reorder_axis=0, reorder_start=A_W + R_W),
            w_gate=_layer_weight_bf16(w_gate, i), w_up=_layer_weight_bf16(w_up, i),
            w_down=_layer_weight_bf16(w_down, i),
            lg_f=jnp.log1p(-jnp.exp2(-ret_decay_fwd[i].astype(F32))),
            lg_b=jnp.log1p(-jnp.exp2(-ret_decay_bwd[i].astype(F32))),
        ))

    def trunk(x):
        b, s, _ = x.shape
        cos, sin = _rope_tables(s)
        x = x.reshape(b * s, D_MODEL)
        for i, p in enumerate(layers):
            x = _layer(x, b, s, dict(p, cos=cos, sin=sin), final_norm_g if i == DEPTH - 1 else None)
        return x.reshape(b, s, D_MODEL)

    return trunk(x_prompt), trunk(x_sample)
```
